```python
import math
import jax
import jax.numpy as jnp
from jax import lax
import numpy as np

D_MODEL = 2048
BATCH = 2
SEQ = 4096
DEPTH = 2
DEC_BATCH = 128
DEC_SEQ = 8
PAST_LEN = 2048
PAGE_SIZE = 128

N_EVEN = (DEPTH + 1) // 2
N_ODD = DEPTH // 2

RWKV_HEAD_DIM = 64
D_A = D_MODEL // 2
RWKV_HEADS = D_A // RWKV_HEAD_DIM
LORA_W = 64
LORA_A = 64
LORA_G = 128
D_A_PROJ = 3 * D_A + LORA_W + LORA_A + LORA_G
A_SPLITS = (D_A, 2 * D_A, 3 * D_A, 3 * D_A + LORA_W, 3 * D_A + LORA_W + LORA_A)
RWKV_GN_EPS = 64e-5

DIFF_HEADS = 8
DIFF_QK_DIM = 64
DIFF_V_DIM = 2 * DIFF_QK_DIM
D_B_Q = 2 * DIFF_HEADS * DIFF_QK_DIM
D_B_V = DIFF_HEADS * DIFF_V_DIM
D_B_PROJ = 2 * D_B_Q + D_B_V
ROT_DIM = DIFF_QK_DIM // 4
ROPE_THETA = 500000.0
ATTN_SCALE = DIFF_QK_DIM ** -0.5
Q_BLOCK = 128
NEG_INF = -1e30

D_IN_EVEN = D_A_PROJ + D_B_PROJ
D_MIX_EVEN = D_A + D_B_V

D_C = D_MODEL
GMLP_GROUPS = 8
GMLP_GROUP_DIM = D_C // GMLP_GROUPS
CHUNK = 128

N_EXP_GROUPS = 4
EXP_PER_GROUP = 8
N_EXPERTS = N_EXP_GROUPS * EXP_PER_GROUP
TOP_K_INNER = 2
D_EXPERT = 512

NORM_EPS = 1e-6

kernel_name = 'hybrid_rwkv7_diffattn_gmlp_hmoe_step'


def _rmsnorm(x, g, eps=NORM_EPS):
    xf = x.astype(jnp.float32)
    return (xf * lax.rsqrt(jnp.mean(xf * xf, axis=-1, keepdims=True) + eps)).astype(x.dtype) * g


def _partial_rotary(x, pos):
    half = ROT_DIM // 2
    inv_freq = ROPE_THETA ** (-jnp.arange(half, dtype=jnp.float32) / half)
    ang = pos.astype(jnp.float32)[:, None] * inv_freq[None, :]
    cos = jnp.cos(ang)[None, :, None, :]
    sin = jnp.sin(ang)[None, :, None, :]
    xf = x.astype(jnp.float32)
    x1 = xf[..., :half]
    x2 = xf[..., half:ROT_DIM]
    rot = jnp.concatenate([x1 * cos - x2 * sin, x2 * cos + x1 * sin, xf[..., ROT_DIM:]], axis=-1)
    return rot.astype(x.dtype)


def _rwkv7_mix(pa, shift_prev, wkv_prev, mu, w0, w2, a0, a2, g2, k_k, k_a, r_k, lnx_g, lnx_b):
    B, T, _ = pa.shape
    f32 = jnp.float32
    H, N = RWKV_HEADS, RWKV_HEAD_DIM
    prev = jnp.concatenate([shift_prev[:, None, :].astype(pa.dtype), pa[:, :-1]], axis=1)
    xm = pa + (prev - pa) * mu
    r, k, v, wd, ad, gd = jnp.split(xm, A_SPLITS, axis=-1)
    w = -jax.nn.softplus(-(w0 + jnp.tanh(wd) @ w2).astype(f32)) - 0.5
    decay = jnp.exp(-jnp.exp(w))
    a = jax.nn.sigmoid((a0 + ad @ a2).astype(f32))
    g = jax.nn.sigmoid(gd) @ g2
    heads = lambda t: t.astype(f32).reshape(B, T, H, N)
    kk = heads(k * k_k)
    kk = kk / jnp.maximum(jnp.sqrt(jnp.sum(kk * kk, axis=-1, keepdims=True)), 1e-12)
    a_h = a.reshape(B, T, H, N)
    k_h = heads(k) * (1.0 + (a_h - 1.0) * k_a.astype(f32).reshape(H, N))
    r_h, v_h, w_h = heads(r), heads(v), decay.reshape(B, T, H, N)
    b_h = kk * a_h

    def step(S, inp):
        r_t, w_t, k_t, v_t, kk_t, b_t = inp
        sa = jnp.einsum('bhij,bhj->bhi', S, -kk_t)
        S = S * w_t[:, :, None, :] + sa[..., None] * b_t[:, :, None, :] + v_t[..., None] * k_t[:, :, None, :]
        return S, jnp.einsum('bhij,bhj->bhi', S, r_t)

    tm = lambda t: jnp.swapaxes(t, 0, 1)
    S_fin, o = lax.scan(step, wkv_prev.astype(f32), tuple(tm(t) for t in (r_h, w_h, k_h, v_h, kk, b_h)))
    o = tm(o)
    mean = jnp.mean(o, axis=-1, keepdims=True)
    var = jnp.mean((o - mean) ** 2, axis=-1, keepdims=True)
    o = ((o - mean) * lax.rsqrt(var + RWKV_GN_EPS)).reshape(B, T, D_A)
    o = o * lnx_g.astype(f32) + lnx_b.astype(f32)
    bonus = jnp.sum(r_h * k_h * r_k.astype(f32), axis=-1, keepdims=True) * v_h
    o = (o + bonus.reshape(B, T, D_A)) * g.astype(f32)
    return o.astype(pa.dtype), pa[:, -1], S_fin.astype(wkv_prev.dtype)


def _diff_probs(s, lam):
    p = jax.nn.softmax(s, axis=-1)
    Bn, M, Tq, Tk = p.shape
    p = p.reshape(Bn, M // 2, 2, Tq, Tk)
    return p[:, :, 0] - lam * p[:, :, 1]


def _diff_attend_prompt(q, k, v, lam):
    B, T = q.shape[:2]
    nb = T // Q_BLOCK
    qb = jnp.swapaxes(q.reshape(B, nb, Q_BLOCK, 2 * DIFF_HEADS, DIFF_QK_DIM), 0, 1)
    k_pos = jnp.arange(T)

    def block(args):
        q_i, i = args
        q_pos = i * Q_BLOCK + jnp.arange(Q_BLOCK)
        s = jnp.einsum('bqmd,bkmd->bmqk', q_i, k).astype(jnp.float32) * ATTN_SCALE
        s = jnp.where(q_pos[:, None] >= k_pos[None, :], s, NEG_INF)
        a = _diff_probs(s, lam).astype(v.dtype)
        return jnp.einsum('bhqk,bkhe->bqhe', a, v)

    o = lax.map(block, (qb, jnp.arange(nb)))
    return jnp.swapaxes(o, 0, 1).reshape(B, T, DIFF_HEADS, DIFF_V_DIM)


def _diff_attend_sample(q, k_new, v_new, k_past, v_past, lam):
    Tq = q.shape[1]
    P = k_past.shape[1]
    s_past = jnp.einsum('bqmd,bkmd->bmqk', q, k_past).astype(jnp.float32) * ATTN_SCALE
    s_new = jnp.einsum('bqmd,bkmd->bmqk', q, k_new).astype(jnp.float32) * ATTN_SCALE
    s_new = jnp.where(jnp.tril(jnp.ones((Tq, Tq), dtype=bool)), s_new, NEG_INF)
    a = _diff_probs(jnp.concatenate([s_past, s_new], axis=-1), lam).astype(v_new.dtype)
    return (jnp.einsum('bhqk,bkhe->bqhe', a[..., :P], v_past)
            + jnp.einsum('bhqk,bkhe->bqhe', a[..., P:], v_new))


def _even_mixer(xn, pos, shift_prev, wkv_prev, k_past, v_past, layer, w_in, w_out,
                mu, w0, w2, a0, a2, g2, k_k, k_a, r_k, lnx_g, lnx_b,
                lq1, lk1, lq2, lk2, subln_g):
    B, T, _ = xn.shape
    f32 = jnp.float32
    proj = xn @ w_in
    pa, q, k, v = jnp.split(proj, (D_A_PROJ, D_A_PROJ + D_B_Q, D_A_PROJ + 2 * D_B_Q), axis=-1)
    oa, shift_new, wkv_new = _rwkv7_mix(pa, shift_prev, wkv_prev, mu, w0, w2, a0, a2, g2,
                                        k_k, k_a, r_k, lnx_g, lnx_b)
    q = _partial_rotary(q.reshape(B, T, 2 * DIFF_HEADS, DIFF_QK_DIM), pos)
    k = _partial_rotary(k.reshape(B, T, 2 * DIFF_HEADS, DIFF_QK_DIM), pos)
    v = v.reshape(B, T, DIFF_HEADS, DIFF_V_DIM)
    lam_init = 0.8 - 0.6 * math.exp(-0.3 * layer)
    lam = (jnp.exp(jnp.sum(lq1.astype(f32) * lk1.astype(f32)))
           - jnp.exp(jnp.sum(lq2.astype(f32) * lk2.astype(f32))) + lam_init)
    if k_past is None:
        ob = _diff_attend_prompt(q, k, v, lam)
    else:
        ob = _diff_attend_sample(q, k, v, k_past, v_past, lam)
    ob = _rmsnorm(ob, subln_g, 1e-5) * (1.0 - lam_init)
    y = jnp.concatenate([oa, ob.reshape(B, T, D_B_V)], axis=-1) @ w_out
    return y, shift_new, wkv_new, k, v


def _odd_mixer(xn, w_in, ln_g, ln_b, w_s, b_s, w_out):
    B, T, _ = xn.shape
    z = jax.nn.gelu(xn @ w_in, approximate=False)
    u, v = jnp.split(z, 2, axis=-1)
    vf = v.astype(jnp.float32)
    m = jnp.mean(vf, axis=-1, keepdims=True)
    var = jnp.mean((vf - m) ** 2, axis=-1, keepdims=True)
    v = ((vf - m) * lax.rsqrt(var + 1e-5)).astype(xn.dtype) * ln_g + ln_b
    L = min(T, CHUNK)
    nc = T // L
    vg = v.reshape(B, nc, L, GMLP_GROUPS, GMLP_GROUP_DIM)
    ws = jnp.tril(w_s[:, :L, :L])
    mixed = jnp.einsum('gts,bcsgd->bctgd', ws, vg) + jnp.transpose(b_s[:, :L])[:, :, None]
    y = (u * mixed.reshape(B, T, D_C)) @ w_out
    return y, v


def _hier_moe(x, w_grp, b_grp, w_exp, b_exp, w1, w3, w2):
    B, T, D = x.shape
    n = B * T
    f32 = jnp.float32
    xf = x.reshape(n, D)
    g_logits = (xf @ w_grp).astype(f32) + b_grp.astype(f32)
    g_prob = jax.nn.softmax(g_logits, axis=-1)
    g_top = jnp.argmax(g_logits, axis=-1)
    g_gate = jnp.take_along_axis(g_prob, g_top[:, None], axis=-1)
    e_logits = ((xf @ w_exp).astype(f32) + b_exp.astype(f32)).reshape(n, N_EXP_GROUPS, EXP_PER_GROUP)
    e_sel = jnp.take_along_axis(e_logits, g_top[:, None, None], axis=1)[:, 0]
    e_prob = jax.nn.softmax(e_sel, axis=-1)
    top_w, top_i = lax.top_k(e_prob, TOP_K_INNER)
    top_w = top_w / jnp.sum(top_w, axis=-1, keepdims=True)
    e_comb = jnp.sum(jax.nn.one_hot(top_i, EXP_PER_GROUP, dtype=f32) * top_w[..., None], axis=1)
    comb = jax.nn.one_hot(g_top, N_EXP_GROUPS, dtype=f32)[:, :, None] * (g_gate[:, :, None] * e_comb[:, None, :])
    comb = comb.reshape(n, N_EXPERTS).astype(x.dtype)
    h = jax.nn.silu(jnp.einsum('nd,edf->nef', xf, w1)) * jnp.einsum('nd,edf->nef', xf, w3)
    y = jnp.einsum('nef,efd->nd', h * comb[:, :, None], w2)
    return y.reshape(B, T, D)


def setup_inputs(seed: int = 0) -> dict:
    key = jax.random.key(seed)
    ks = iter(jax.random.split(key, 64))
    f32 = jnp.float32

    def nrm(shape, scale):
        return jax.random.normal(next(ks), shape, f32) * scale

    def gain(shape):
        return 1.0 + 0.02 * jax.random.normal(next(ks), shape, f32)

    n_pages = PAST_LEN // PAGE_SIZE
    n_used = DEC_BATCH * n_pages
    n_pool = n_used + max(n_used // 4, 1)
    page_table = jax.random.permutation(next(ks), n_pool)[:n_used].reshape(DEC_BATCH, n_pages).astype(jnp.int32)
    decay_base = jnp.linspace(-6.5, -1.5, D_A, dtype=f32)[None, :]
    return {
        'x_prompt': nrm((BATCH, SEQ, D_MODEL), 1.0),
        'x_sample': nrm((DEC_BATCH, DEC_SEQ, D_MODEL), 1.0),
        'cache_k': nrm((N_EVEN, n_pool, PAGE_SIZE, 2 * DIFF_HEADS, DIFF_QK_DIM), 1.0),
        'cache_v': nrm((N_EVEN, n_pool, PAGE_SIZE, DIFF_HEADS, DIFF_V_DIM), 1.0),
        'page_table': page_table,
        'state_wkv': nrm((N_EVEN, DEC_BATCH, RWKV_HEADS, RWKV_HEAD_DIM, RWKV_HEAD_DIM), 0.3),
        'state_shift': nrm((N_EVEN, DEC_BATCH, D_A_PROJ), 1.0),
        'norm_mix': gain((DEPTH, D_MODEL)),
        'norm_ffn': gain((DEPTH, D_MODEL)),
        'norm_final': gain((D_MODEL,)),
        'even_w_in': nrm((N_EVEN, D_MODEL, D_IN_EVEN), D_MODEL ** -0.5),
        'even_w_out': nrm((N_EVEN, D_MIX_EVEN, D_MODEL), D_MIX_EVEN ** -0.5),
        'rwkv_mu': jax.random.uniform(next(ks), (N_EVEN, D_A_PROJ), f32),
        'rwkv_w0': decay_base + nrm((N_EVEN, D_A), 0.1),
        'rwkv_w2': nrm((N_EVEN, LORA_W, D_A), 0.1 * LORA_W ** -0.5),
        'rwkv_a0': nrm((N_EVEN, D_A), 0.1),
        'rwkv_a2': nrm((N_EVEN, LORA_A, D_A), 0.1 * LORA_A ** -0.5),
        'rwkv_g2': nrm((N_EVEN, LORA_G, D_A), LORA_G ** -0.5),
        'rwkv_k_k': 0.85 + nrm((N_EVEN, D_A), 0.02),
        'rwkv_k_a': gain((N_EVEN, D_A)),
        'rwkv_r_k': nrm((N_EVEN, RWKV_HEADS, RWKV_HEAD_DIM), 0.1),
        'rwkv_lnx_g': gain((N_EVEN, D_A)),
        'rwkv_lnx_b': nrm((N_EVEN, D_A), 0.02),
        'diff_lam_q1': nrm((N_EVEN, DIFF_QK_DIM), 0.1),
        'diff_lam_k1': nrm((N_EVEN, DIFF_QK_DIM), 0.1),
        'diff_lam_q2': nrm((N_EVEN, DIFF_QK_DIM), 0.1),
        'diff_lam_k2': nrm((N_EVEN, DIFF_QK_DIM), 0.1),
        'diff_subln_g': gain((N_EVEN, DIFF_V_DIM)),
        'gmlp_w_in': nrm((N_ODD, D_MODEL, 2 * D_C), D_MODEL ** -0.5),
        'gmlp_ln_g': gain((N_ODD, D_C)),
        'gmlp_ln_b': nrm((N_ODD, D_C), 0.02),
        'gmlp_w_s': nrm((N_ODD, GMLP_GROUPS, CHUNK, CHUNK), 0.5 * CHUNK ** -0.5),
        'gmlp_b_s': gain((N_ODD, GMLP_GROUPS, CHUNK)),
        'gmlp_w_out': nrm((N_ODD, D_C, D_MODEL), D_C ** -0.5),
        'moe_w_grp': nrm((DEPTH, D_MODEL, N_EXP_GROUPS), D_MODEL ** -0.5),
        'moe_b_grp': nrm((DEPTH, N_EXP_GROUPS), 0.01),
        'moe_w_exp': nrm((DEPTH, D_MODEL, N_EXPERTS), D_MODEL ** -0.5),
        'moe_b_exp': nrm((DEPTH, N_EXPERTS), 0.01),
        'moe_w1': nrm((DEPTH, N_EXPERTS, D_MODEL, D_EXPERT), D_MODEL ** -0.5),
        'moe_w3': nrm((DEPTH, N_EXPERTS, D_MODEL, D_EXPERT), D_MODEL ** -0.5),
        'moe_w2': nrm((DEPTH, N_EXPERTS, D_EXPERT, D_MODEL), D_EXPERT ** -0.5),
    }


def reference(x_prompt, x_sample, cache_k, cache_v, page_table, state_wkv, state_shift,
              norm_mix, norm_ffn, norm_final, even_w_in, even_w_out,
              rwkv_mu, rwkv_w0, rwkv_w2, rwkv_a0, rwkv_a2, rwkv_g2, rwkv_k_k, rwkv_k_a, rwkv_r_k,
              rwkv_lnx_g, rwkv_lnx_b, diff_lam_q1, diff_lam_k1, diff_lam_q2, diff_lam_k2, diff_subln_g,
              gmlp_w_in, gmlp_ln_g, gmlp_ln_b, gmlp_w_s, gmlp_b_s, gmlp_w_out,
              moe_w_grp, moe_b_grp, moe_w_exp, moe_b_exp, moe_w1, moe_w3, moe_w2):
    Bp, Tp, _ = x_prompt.shape
    Bs, Ts, _ = x_sample.shape
    pos_p = jnp.arange(Tp)
    pos_s = PAST_LEN + jnp.arange(Ts)
    n_past = page_table.shape[1] * cache_k.shape[2]
    hp, hs = x_prompt, x_sample
    k_p, v_p, wkv_p, sh_p = [], [], [], []
    k_s, v_s, wkv_s, sh_s = [], [], [], []
    gv_s = []
    for layer in range(DEPTH):
        j = layer // 2
        xn_p = _rmsnorm(hp, norm_mix[layer])
        xn_s = _rmsnorm(hs, norm_mix[layer])
        if layer % 2 == 0:
            ev = (even_w_in[j], even_w_out[j], rwkv_mu[j], rwkv_w0[j], rwkv_w2[j], rwkv_a0[j], rwkv_a2[j],
                  rwkv_g2[j], rwkv_k_k[j], rwkv_k_a[j], rwkv_r_k[j], rwkv_lnx_g[j], rwkv_lnx_b[j],
                  diff_lam_q1[j], diff_lam_k1[j], diff_lam_q2[j], diff_lam_k2[j], diff_subln_g[j])
            zero_shift = jnp.zeros((Bp, D_A_PROJ), x_prompt.dtype)
            zero_wkv = jnp.zeros((Bp, RWKV_HEADS, RWKV_HEAD_DIM, RWKV_HEAD_DIM), state_wkv.dtype)
            y_p, s1, w1_, kp, vp = _even_mixer(xn_p, pos_p, zero_shift, zero_wkv, None, None, layer, *ev)
            k_past = cache_k[j, page_table].reshape(Bs, n_past, 2 * DIFF_HEADS, DIFF_QK_DIM)
            v_past = cache_v[j, page_table].reshape(Bs, n_past, DIFF_HEADS, DIFF_V_DIM)
            y_s, s2, w2_, ks_, vs_ = _even_mixer(xn_s, pos_s, state_shift[j], state_wkv[j],
                                                 k_past, v_past, layer, *ev)
            k_p.append(kp); v_p.append(vp); wkv_p.append(w1_); sh_p.append(s1)
            k_s.append(ks_); v_s.append(vs_); wkv_s.append(w2_); sh_s.append(s2)
        else:
            od = (gmlp_w_in[j], gmlp_ln_g[j], gmlp_ln_b[j], gmlp_w_s[j], gmlp_b_s[j], gmlp_w_out[j])
            y_p, _ = _odd_mixer(xn_p, *od)
            y_s, v_rows = _odd_mixer(xn_s, *od)
            gv_s.append(v_rows)
        hp = hp + y_p
        hs = hs + y_s
        moe = (moe_w_grp[layer], moe_b_grp[layer], moe_w_exp[layer], moe_b_exp[layer],
               moe_w1[layer], moe_w3[layer], moe_w2[layer])
        hp = hp + _hier_moe(_rmsnorm(hp, norm_ffn[layer]), *moe)
        hs = hs + _hier_moe(_rmsnorm(hs, norm_ffn[layer]), *moe)
    y_prompt = _rmsnorm(hp, norm_final)
    y_sample = _rmsnorm(hs, norm_final)
    return (y_prompt, y_sample,
            jnp.stack(k_p), jnp.stack(v_p), jnp.stack(wkv_p), jnp.stack(sh_p),
            jnp.stack(k_s), jnp.stack(v_s), jnp.stack(wkv_s), jnp.stack(sh_s),
            jnp.stack(gv_s))
```

```python
import functools
import math

import jax
import jax.numpy as jnp
from jax import lax
from jax.experimental import pallas as pl
from jax.experimental.pallas import tpu as pltpu

F32 = jnp.float32
BF16 = jnp.bfloat16
I32 = jnp.int32

RWKV_HEAD_DIM = 64
RWKV_GN_EPS = 64e-5
DIFF_QK_DIM = 64
DIFF_V_DIM = 128
ROT_DIM = 16
ROPE_THETA = 500000.0
ATTN_SCALE = DIFF_QK_DIM ** -0.5
NEG_INF = -1e30
NORM_EPS = 1e-6
SUBLN_EPS = 1e-5
GMLP_LN_EPS = 1e-5
GMLP_GROUPS = 8
CHUNK = 128
N_EXP_GROUPS = 4
EXP_PER_GROUP = 8
LORA_W, LORA_A, LORA_G = 64, 64, 128

LANES = 128
MXU_TILE = 256
VMEM_LIMIT = 56 * 1024 * 1024
MOE_TILE = 256
HIGHEST = lax.Precision.HIGHEST


def _cparams(sem):
    return pltpu.CompilerParams(dimension_semantics=sem, vmem_limit_bytes=VMEM_LIMIT)


def _rmsnorm_body(x_ref, g_ref, o_ref, *, eps):
    x = x_ref[...]
    inv = lax.rsqrt(jnp.mean(x * x, axis=-1, keepdims=True) + eps)
    o_ref[...] = ((x * inv) * g_ref[...]).astype(o_ref.dtype)


def _rmsnorm(x, g, eps, out_dtype, tm):
    m, d = x.shape
    return pl.pallas_call(
        functools.partial(_rmsnorm_body, eps=eps),
        grid=(m // tm,),
        in_specs=[pl.BlockSpec((tm, d), lambda i: (i, 0)),
                  pl.BlockSpec((1, d), lambda i: (0, 0))],
        out_specs=pl.BlockSpec((tm, d), lambda i: (i, 0)),
        out_shape=jax.ShapeDtypeStruct((m, d), out_dtype),
        compiler_params=_cparams(("arbitrary",)),
        name="rmsnorm",
    )(x, g.reshape(1, d))


def _gelu_exact(x):
    return 0.5 * x * (1.0 + lax.erf(x * (1.0 / math.sqrt(2.0))))


def _mm_body(*refs, n_x, act, has_res):
    xs = refs[:n_x]
    w_ref = refs[n_x]
    res_ref = refs[n_x + 1] if has_res else None
    o_ref = refs[n_x + 1 + has_res]
    wb_ref = refs[n_x + 2 + has_res]

    @pl.when(pl.program_id(1) == 0)
    def _():
        wb_ref[...] = w_ref[...].astype(BF16)

    kx = w_ref.shape[0] // n_x
    acc = None
    for i, x_ref in enumerate(xs):
        part = jnp.dot(x_ref[...], wb_ref[i * kx:(i + 1) * kx, :], preferred_element_type=F32)
        acc = part if acc is None else acc + part
    if act == "gelu":
        acc = _gelu_exact(acc)
    if has_res:
        acc = acc + res_ref[...]
    o_ref[...] = acc.astype(o_ref.dtype)


def _matmul(xs, w, *, bm, bn, act=None, res=None, out_dtype=F32, name="matmul"):
    m = xs[0].shape[0]
    k, n = w.shape
    kx = k // len(xs)
    in_specs = [pl.BlockSpec((bm, kx), lambda j, i: (i, 0)) for _ in xs]
    in_specs.append(pl.BlockSpec((k, bn), lambda j, i: (0, j)))
    args = list(xs) + [w]
    if res is not None:
        in_specs.append(pl.BlockSpec((bm, bn), lambda j, i: (i, j)))
        args.append(res)
    return pl.pallas_call(
        functools.partial(_mm_body, n_x=len(xs), act=act, has_res=res is not None),
        grid=(n // bn, m // bm),
        in_specs=in_specs,
        out_specs=pl.BlockSpec((bm, bn), lambda j, i: (i, j)),
        out_shape=jax.ShapeDtypeStruct((m, n), out_dtype),
        scratch_shapes=[pltpu.VMEM((k, bn), BF16)],
        compiler_params=_cparams(("arbitrary", "arbitrary")),
        name=name,
    )(*args)


def _block_ones(n, seg):
    r = lax.broadcasted_iota(I32, (n, n), 0) // seg
    c = lax.broadcasted_iota(I32, (n, n), 1) // seg
    return jnp.where(r == c, 1.0, 0.0).astype(BF16)


def _seg_sum(x, ones_bd):
    hi = x.astype(BF16)
    lo = (x - hi.astype(F32)).astype(BF16)
    outs = []
    for c in range(x.shape[1] // MXU_TILE):
        sl = slice(c * MXU_TILE, (c + 1) * MXU_TILE)
        outs.append(jnp.dot(hi[:, sl], ones_bd, preferred_element_type=F32)
                    + jnp.dot(lo[:, sl], ones_bd, preferred_element_type=F32))
    return jnp.concatenate(outs, axis=1)


def _sigmoid(x):
    return 1.0 / (1.0 + jnp.exp(-x))


def _rwkv_pre_body(pa_ref, repl_ref, mu_ref, w0_ref, w2_ref, a0_ref, a2_ref, g2_ref,
                   kk_ref, ka_ref, rk_ref,
                   r_o, dec_o, k_o, v_o, kkn_o, b_o, g_o, bonus_o, *, period, d_a):
    pa = pa_ref[...]
    tm = pa.shape[0]
    row = lax.broadcasted_iota(I32, pa.shape, 0)
    prev = pltpu.roll(pa, 1, 0)
    repl = repl_ref[...]
    if repl.shape[0] != tm:
        repl = jnp.broadcast_to(repl[0:1, :], pa.shape)
    prev = jnp.where(row % period == 0, repl, prev)
    xm = pa + (prev - pa) * mu_ref[...]
    r = xm[:, 0:d_a]
    k = xm[:, d_a:2 * d_a]
    v = xm[:, 2 * d_a:3 * d_a]
    o = 3 * d_a
    wd = xm[:, o:o + LORA_W]
    ad = xm[:, o + LORA_W:o + LORA_W + LORA_A]
    gd = xm[:, o + LORA_W + LORA_A:o + LORA_W + LORA_A + LORA_G]

    z = w0_ref[...] + jnp.dot(jnp.tanh(wd), w2_ref[...], precision=HIGHEST, preferred_element_type=F32)
    w = jnp.minimum(z, 0.0) - jnp.log1p(jnp.exp(-jnp.abs(z))) - 0.5
    dec = jnp.exp(-jnp.exp(w))
    a = _sigmoid(a0_ref[...] + jnp.dot(ad, a2_ref[...], precision=HIGHEST, preferred_element_type=F32))
    g = jnp.dot(_sigmoid(gd).astype(BF16), g2_ref[...].astype(BF16), preferred_element_type=F32)

    ones_bd = _block_ones(MXU_TILE, RWKV_HEAD_DIM)
    kk = k * kk_ref[...]
    nrm = jnp.sqrt(_seg_sum(kk * kk, ones_bd))
    kkn = kk / jnp.maximum(nrm, 1e-12)
    kh = k * (1.0 + (a - 1.0) * ka_ref[...])
    bonus = _seg_sum(r * kh * rk_ref[...], ones_bd) * v

    r_o[...] = r
    dec_o[...] = dec
    k_o[...] = kh
    v_o[...] = v
    kkn_o[...] = kkn
    b_o[...] = kkn * a
    g_o[...] = g
    bonus_o[...] = bonus


def _rwkv_pre(proj, repl, period, row0, rows, tm, prm):
    d_a = prm["w0"].shape[1]
    d_ap = prm["mu"].shape[1]
    nt = rows // tm
    t0 = row0 // tm
    vec = lambda n: pl.BlockSpec((1, n), lambda i: (0, 0))
    full = lambda a: pl.BlockSpec(a.shape, lambda i: (0, 0))
    out_spec = pl.BlockSpec((tm, d_a), lambda i: (i, 0))
    outs = pl.pallas_call(
        functools.partial(_rwkv_pre_body, period=period, d_a=d_a),
        grid=(nt,),
        in_specs=[pl.BlockSpec((tm, d_ap), lambda i: (t0 + i, 0)),
                  pl.BlockSpec((repl.shape[0] // nt, d_ap), lambda i: (i, 0)),
                  vec(d_ap), vec(d_a), full(prm["w2"]), vec(d_a), full(prm["a2"]), full(prm["g2"]),
                  vec(d_a), vec(d_a), vec(d_a)],
        out_specs=[out_spec] * 8,
        out_shape=[jax.ShapeDtypeStruct((rows, d_a), F32)] * 8,
        compiler_params=_cparams(("arbitrary",)),
        name="rwkv_pre",
    )(proj, repl, prm["mu"], prm["w0"], prm["w2"], prm["a0"], prm["a2"], prm["g2"],
      prm["k_k"], prm["k_a"], prm["r_k"])
    return outs


def _rwkv_scan_body(*refs, tc, tp, nheads, has_state):
    if has_state:
        s0_ref = refs[0]
        refs = refs[1:]
    (r_ref, w_ref, k_ref, v_ref, kk_ref, b_ref, g_ref, bonus_ref, lg_ref, lb_ref,
     o_ref, sout_ref, s_scr, vt_scr, ot_scr) = refs
    n = RWKV_HEAD_DIM
    c = pl.program_id(1)

    @pl.when(c == 0)
    def _():
        if has_state:
            zero = jnp.zeros((n, n), F32)
            for h in range(nheads):
                s0 = s0_ref[0, h]
                s_scr[h] = jnp.concatenate([s0, zero] if h % 2 == 0 else [zero, s0], axis=1)
        else:
            s_scr[...] = jnp.zeros_like(s_scr)

    for h in range(nheads):
        vh = v_ref[:, h * n:(h + 1) * n]
        if tp != tc:
            vh = jnp.concatenate([vh, jnp.zeros((tp - tc, n), F32)], axis=0)
        vt_scr[h] = vh.T
    ot_scr[...] = jnp.zeros_like(ot_scr)

    lane = lax.broadcasted_iota(I32, (n, tp), 1)
    low_half = lax.broadcasted_iota(I32, (1, 2 * n), 1) < n

    sub = 8

    def group(tg, carry):
        base = pl.multiple_of(tg * sub, sub)
        for hp in range(nheads // 2):
            sl = slice(hp * 2 * n, (hp + 1) * 2 * n)
            kk8 = kk_ref[pl.ds(base, sub), sl]
            w8 = w_ref[pl.ds(base, sub), sl]
            b8 = b_ref[pl.ds(base, sub), sl]
            k8 = k_ref[pl.ds(base, sub), sl]
            r8 = r_ref[pl.ds(base, sub), sl]
            for h in (2 * hp, 2 * hp + 1):
                own = low_half if h % 2 == 0 else jnp.logical_not(low_half)
                kk8m = jnp.where(own, kk8, 0.0)
                b8m = jnp.where(own, b8, 0.0)
                k8m = jnp.where(own, k8, 0.0)
                s_h = s_scr[h]
                vt_h = vt_scr[h]
                ot_h = ot_scr[h]
                for s in range(sub):
                    sel = lane == base + s
                    sa = -jnp.sum(s_h * kk8m[s:s + 1], axis=-1, keepdims=True)
                    v_col = jnp.sum(jnp.where(sel, vt_h, 0.0), axis=-1, keepdims=True)
                    s_h = s_h * w8[s:s + 1] + sa * b8m[s:s + 1] + v_col * k8m[s:s + 1]
                    o_col = jnp.sum(s_h * r8[s:s + 1], axis=-1, keepdims=True)
                    ot_h = jnp.where(sel, o_col, ot_h)
                s_scr[h] = s_h
                ot_scr[h] = ot_h
        return carry

    lax.fori_loop(0, tc // sub, group, 0)

    for hp in range(nheads // 2):
        parts = []
        for h in (2 * hp, 2 * hp + 1):
            oc = ot_scr[h]
            mean = jnp.mean(oc, axis=0, keepdims=True)
            d = oc - mean
            var = jnp.mean(d * d, axis=0, keepdims=True)
            on = d * lax.rsqrt(var + RWKV_GN_EPS)
            parts.append(on.T[:tc])
        sl = slice(hp * 2 * n, (hp + 1) * 2 * n)
        on2 = jnp.concatenate(parts, axis=1)
        o = (on2 * lg_ref[:, sl] + lb_ref[:, sl] + bonus_ref[:, sl]) * g_ref[:, sl]
        o_ref[:, sl] = o.astype(o_ref.dtype)

    @pl.when(c == pl.num_programs(1) - 1)
    def _():
        for h in range(nheads):
            s_h = s_scr[h]
            sout_ref[0, h] = s_h[:, 0:n] if h % 2 == 0 else s_h[:, n:2 * n]


def _rwkv_scan(pre, lnx_g, lnx_b, state, nseq, tseq, tc):
    r, dec, kh, v, kkn, b, g, bonus = pre
    rows, d_a = r.shape
    nheads = d_a // RWKV_HEAD_DIM
    nch = tseq // tc
    tp = max(tc, LANES)
    n = RWKV_HEAD_DIM
    tok = pl.BlockSpec((tc, d_a), lambda s, c: (s * nch + c, 0))
    vec = pl.BlockSpec((1, d_a), lambda s, c: (0, 0))
    st = pl.BlockSpec((1, nheads, n, n), lambda s, c: (s, 0, 0, 0))
    has_state = state is not None
    in_specs = ([st] if has_state else []) + [tok] * 8 + [vec, vec]
    args = ([state] if has_state else []) + [r, dec, kh, v, kkn, b, g, bonus,
                                             lnx_g.reshape(1, d_a), lnx_b.reshape(1, d_a)]
    return pl.pallas_call(
        functools.partial(_rwkv_scan_body, tc=tc, tp=tp, nheads=nheads, has_state=has_state),
        grid=(nseq, nch),
        in_specs=in_specs,
        out_specs=[tok, st],
        out_shape=[jax.ShapeDtypeStruct((rows, d_a), BF16 if tc % 16 == 0 else F32),
                   jax.ShapeDtypeStruct((nseq, nheads, n, n), F32)],
        scratch_shapes=[pltpu.VMEM((nheads, n, 2 * n), F32),
                        pltpu.VMEM((nheads, n, tp), F32),
                        pltpu.VMEM((nheads, n, tp), F32)],
        compiler_params=_cparams(("arbitrary", "arbitrary")),
        name="rwkv_scan",
    )(*args)


def _rope_body(q_ref, k_ref, v_ref, c_ref, s1_ref, s2_ref, qs_o, qf_o, kf_o, kb_o, vb_o):
    c, s1, s2 = c_ref[...], s1_ref[...], s2_ref[...]
    half = ROT_DIM // 2
    w = q_ref.shape[1]

    def rot(x):
        return x * c + pltpu.roll(x, w - half, 1) * s1 + pltpu.roll(x, half, 1) * s2

    qr = rot(q_ref[...]) * ATTN_SCALE
    qf_o[...] = qr
    qs_o[...] = qr.astype(BF16)
    kr = rot(k_ref[...])
    kf_o[...] = kr
    kb_o[...] = kr.astype(BF16)
    vb_o[...] = v_ref[...].astype(BF16)


def _rope(proj, pos, d_ap, d_q, tm):
    n_tok = proj.shape[0]
    w = MXU_TILE
    half = ROT_DIM // 2
    inv_freq = ROPE_THETA ** (-jnp.arange(half, dtype=F32) / half)
    ang = pos.astype(F32)[:, None] * inv_freq[None, :]
    cos, sin = jnp.cos(ang), jnp.sin(ang)
    pad = DIFF_QK_DIM - ROT_DIM
    ones = jnp.ones((n_tok, pad), F32)
    zeros = jnp.zeros((n_tok, pad), F32)
    zh = jnp.zeros((n_tok, half), F32)
    reps = w // DIFF_QK_DIM
    c_t = jnp.tile(jnp.concatenate([cos, cos, ones], axis=1), (1, reps))
    s1_t = jnp.tile(jnp.concatenate([-sin, zh, zeros], axis=1), (1, reps))
    s2_t = jnp.tile(jnp.concatenate([zh, sin, zeros], axis=1), (1, reps))
    nq = d_q // w
    qb, kb, vb = d_ap // w, d_ap // w + nq, d_ap // w + 2 * nq
    tab = pl.BlockSpec((tm, w), lambda i, j: (i, 0))
    out = pl.BlockSpec((tm, w), lambda i, j: (i, j))
    return pl.pallas_call(
        _rope_body,
        grid=(n_tok // tm, nq),
        in_specs=[pl.BlockSpec((tm, w), lambda i, j: (i, qb + j)),
                  pl.BlockSpec((tm, w), lambda i, j: (i, kb + j)),
                  pl.BlockSpec((tm, w), lambda i, j: (i, vb + j)),
                  tab, tab, tab],
        out_specs=[out, out, out, out, out],
        out_shape=[jax.ShapeDtypeStruct((n_tok, d_q), BF16),
                   jax.ShapeDtypeStruct((n_tok, d_q), F32),
                   jax.ShapeDtypeStruct((n_tok, d_q), F32),
                   jax.ShapeDtypeStruct((n_tok, d_q), BF16),
                   jax.ShapeDtypeStruct((n_tok, d_q), BF16)],
        compiler_params=_cparams(("arbitrary", "arbitrary")),
        name="rope",
    )(proj, proj, proj, c_t, s1_t, s2_t)


def _lambda(lam_ref, lam_init):
    lp = lam_ref[...]
    l1 = jnp.sum(lp[0:1] * lp[1:2], axis=-1, keepdims=True)
    l2 = jnp.sum(lp[2:3] * lp[3:4], axis=-1, keepdims=True)
    return jnp.exp(l1) - jnp.exp(l2) + lam_init


def _subln(o, g_ref, lam_init):
    inv = lax.rsqrt(jnp.mean(o * o, axis=-1, keepdims=True) + SUBLN_EPS)
    return (o * inv) * g_ref[...] * (1.0 - lam_init)


def _online_softmax_update(s, v, m_scr, l_scr, acc_scr):
    m_prev = m_scr[...]
    m_new = jnp.maximum(m_prev, jnp.max(s, axis=-1, keepdims=True))
    alpha = jnp.exp(m_prev - m_new)
    p = jnp.exp(s - m_new[:, 0:1])
    l_scr[...] = alpha * l_scr[...] + jnp.sum(p, axis=-1, keepdims=True)
    acc_scr[...] = alpha[:, 0:1] * acc_scr[...] + jnp.dot(p.astype(BF16), v, preferred_element_type=F32)
    m_scr[...] = m_new


def _attn_prompt_body(q_ref, k_ref, v_ref, lam_ref, g_ref, o_ref, qq_scr, m_scr, l_scr, acc_scr,
                      *, tq, lam_init):
    qi = pl.program_id(2)
    ki = pl.program_id(3)

    @pl.when(ki == 0)
    def _():
        q = q_ref[...]
        lane = lax.broadcasted_iota(I32, q.shape, 1)
        zero = jnp.zeros_like(q)
        qq_scr[0:tq] = jnp.where(lane < DIFF_QK_DIM, q, zero)
        qq_scr[tq:2 * tq] = jnp.where(lane >= DIFF_QK_DIM, q, zero)
        m_scr[...] = jnp.full_like(m_scr, NEG_INF)
        l_scr[...] = jnp.zeros_like(l_scr)
        acc_scr[...] = jnp.zeros_like(acc_scr)

    def scores():
        return lax.dot_general(qq_scr[...], k_ref[...], (((1,), (1,)), ((), ())),
                               preferred_element_type=F32)

    @pl.when(ki < qi)
    def _():
        _online_softmax_update(scores(), v_ref[...], m_scr, l_scr, acc_scr)

    @pl.when(ki == qi)
    def _():
        s = scores()
        row = lax.broadcasted_iota(I32, s.shape, 0) % tq
        col = lax.broadcasted_iota(I32, s.shape, 1)
        s = jnp.where(row >= col, s, NEG_INF)
        _online_softmax_update(s, v_ref[...], m_scr, l_scr, acc_scr)
        lam = _lambda(lam_ref, lam_init)
        o1 = acc_scr[0:tq] / l_scr[0:tq, 0:1]
        o2 = acc_scr[tq:2 * tq] / l_scr[tq:2 * tq, 0:1]
        o_ref[...] = _subln(o1 - lam * o2, g_ref, lam_init).astype(o_ref.dtype)


def _attn_prompt(qs, kb, vb, lam_p, subln_g, nseq, tseq, tq, lam_init):
    rows, d_q = qs.shape
    nh = d_q // DIFF_V_DIM
    nq = tseq // tq
    w = DIFF_V_DIM
    return pl.pallas_call(
        functools.partial(_attn_prompt_body, tq=tq, lam_init=lam_init),
        grid=(nseq, nh, nq, nq),
        in_specs=[pl.BlockSpec((tq, w), lambda b, h, i, j: (b * nq + i, h)),
                  pl.BlockSpec((tq, w), lambda b, h, i, j: (b * nq + jnp.minimum(i, j), h)),
                  pl.BlockSpec((tq, w), lambda b, h, i, j: (b * nq + jnp.minimum(i, j), h)),
                  pl.BlockSpec(lam_p.shape, lambda b, h, i, j: (0, 0)),
                  pl.BlockSpec((1, w), lambda b, h, i, j: (0, 0))],
        out_specs=pl.BlockSpec((tq, w), lambda b, h, i, j: (b * nq + i, h)),
        out_shape=jax.ShapeDtypeStruct((nseq * tseq, d_q), BF16),
        scratch_shapes=[pltpu.VMEM((2 * tq, w), BF16),
                        pltpu.VMEM((2 * tq, LANES), F32),
                        pltpu.VMEM((2 * tq, LANES), F32),
                        pltpu.VMEM((2 * tq, w), F32)],
        compiler_params=_cparams(("arbitrary", "arbitrary", "arbitrary", "arbitrary")),
        name="attn_prompt",
    )(qs, kb, vb, lam_p, subln_g.reshape(1, w))


def _attn_sample_body(pt_ref, q_ref, kc_ref, vc_ref, kn_ref, vn_ref, lam_ref, g_ref, o_ref,
                      qbd_scr, m_scr, l_scr, acc_scr, *, ts, lam_init):
    p = pl.program_id(1)
    nheads_qk = q_ref.shape[1] // DIFF_QK_DIM
    nrow = nheads_qk * ts

    @pl.when(p == 0)
    def _():
        q = q_ref[...]
        head = lax.broadcasted_iota(I32, q.shape, 1) // DIFF_QK_DIM
        zero = jnp.zeros_like(q)
        rows = [jnp.where(head == m, q, zero) for m in range(nheads_qk)]
        qbd_scr[...] = jnp.concatenate(rows, axis=0).astype(BF16)
        m_scr[...] = jnp.full_like(m_scr, NEG_INF)
        l_scr[...] = jnp.zeros_like(l_scr)
        acc_scr[...] = jnp.zeros_like(acc_scr)

    def scores(kmat):
        return lax.dot_general(qbd_scr[...], kmat, (((1,), (1,)), ((), ())), preferred_element_type=F32)

    _online_softmax_update(scores(kc_ref[0].astype(BF16)), vc_ref[0].astype(BF16), m_scr, l_scr, acc_scr)

    @pl.when(p == pl.num_programs(1) - 1)
    def _():
        padk = jnp.zeros((LANES - ts, kn_ref.shape[1]), F32)
        kn = jnp.concatenate([kn_ref[...], padk], axis=0).astype(BF16)
        vn = jnp.concatenate([vn_ref[...], padk], axis=0).astype(BF16)
        s = scores(kn)
        row = lax.broadcasted_iota(I32, s.shape, 0) % ts
        col = lax.broadcasted_iota(I32, s.shape, 1)
        s = jnp.where(row >= col, s, NEG_INF)
        _online_softmax_update(s, vn, m_scr, l_scr, acc_scr)
        lam = _lambda(lam_ref, lam_init)
        w = DIFF_V_DIM
        for h in range(nheads_qk // 2):
            r1 = slice(2 * h * ts, (2 * h + 1) * ts)
            r2 = slice((2 * h + 1) * ts, (2 * h + 2) * ts)
            cs = slice(h * w, (h + 1) * w)
            o1 = acc_scr[r1, cs] / l_scr[r1, 0:1]
            o2 = acc_scr[r2, cs] / l_scr[r2, 0:1]
            o_ref[:, cs] = _subln(o1 - lam * o2, g_ref, lam_init).astype(o_ref.dtype)


def _attn_sample(qs, kb, vb, cache_k2, cache_v2, page_flat, lam_p, subln_g, nb, ts, npages, lam_init):
    d_q = qs.shape[1]
    nrow = (d_q // DIFF_QK_DIM) * ts
    psz = cache_k2.shape[1]
    tok = pl.BlockSpec((ts, d_q), lambda b, p, pt: (b, 0))
    page = pl.BlockSpec((1, psz, d_q), lambda b, p, pt: (pt[b * npages + p], 0, 0))
    return pl.pallas_call(
        functools.partial(_attn_sample_body, ts=ts, lam_init=lam_init),
        grid_spec=pltpu.PrefetchScalarGridSpec(
            num_scalar_prefetch=1,
            grid=(nb, npages),
            in_specs=[tok, page, page, tok, tok,
                      pl.BlockSpec(lam_p.shape, lambda b, p, pt: (0, 0)),
                      pl.BlockSpec((1, DIFF_V_DIM), lambda b, p, pt: (0, 0))],
            out_specs=tok,
            scratch_shapes=[pltpu.VMEM((nrow, d_q), BF16),
                            pltpu.VMEM((nrow, LANES), F32),
                            pltpu.VMEM((nrow, LANES), F32),
                            pltpu.VMEM((nrow, d_q), F32)]),
        out_shape=jax.ShapeDtypeStruct((nb * ts, d_q), F32),
        compiler_params=_cparams(("arbitrary", "arbitrary")),
        name="attn_sample",
    )(page_flat, qs, cache_k2, cache_v2, kb, vb, lam_p, subln_g.reshape(1, DIFF_V_DIM))


def _gmlp_body(z_ref, lg_ref, lb_ref, wm_ref, bm_ref, y_ref, *maybe_v, d_c):
    u = z_ref[:, 0:d_c]
    v = z_ref[:, d_c:2 * d_c]
    mean = jnp.mean(v, axis=-1, keepdims=True)
    d = v - mean
    var = jnp.mean(d * d, axis=-1, keepdims=True)
    vn = (d * lax.rsqrt(var + GMLP_LN_EPS)) * lg_ref[...] + lb_ref[...]
    if maybe_v:
        maybe_v[0][...] = vn
    gd = d_c // GMLP_GROUPS
    vb = vn.astype(BF16)
    for g in range(GMLP_GROUPS):
        sl = slice(g * gd, (g + 1) * gd)
        mixed = jnp.dot(wm_ref[g].astype(BF16), vb[:, sl], preferred_element_type=F32) + bm_ref[g]
        y_ref[:, sl] = (u[:, sl] * mixed).astype(y_ref.dtype)


def _gmlp(z, row0, rows, ln_g, ln_b, wmix, bmix, emit_v):
    d_c = z.shape[1] // 2
    tm = CHUNK
    t0 = row0 // tm
    gd = d_c // GMLP_GROUPS
    out_specs = [pl.BlockSpec((tm, d_c), lambda i: (i, 0))]
    out_shape = [jax.ShapeDtypeStruct((rows, d_c), BF16)]
    if emit_v:
        out_specs.append(pl.BlockSpec((tm, d_c), lambda i: (i, 0)))
        out_shape.append(jax.ShapeDtypeStruct((rows, d_c), F32))
    return pl.pallas_call(
        functools.partial(_gmlp_body, d_c=d_c),
        grid=(rows // tm,),
        in_specs=[pl.BlockSpec((tm, 2 * d_c), lambda i: (t0 + i, 0)),
                  pl.BlockSpec((1, d_c), lambda i: (0, 0)),
                  pl.BlockSpec((1, d_c), lambda i: (0, 0)),
                  pl.BlockSpec((GMLP_GROUPS, tm, tm), lambda i: (0, 0, 0)),
                  pl.BlockSpec((GMLP_GROUPS, tm, gd), lambda i: (0, 0, 0))],
        out_specs=out_specs,
        out_shape=out_shape,
        compiler_params=_cparams(("arbitrary",)),
        name="gmlp_gate",
    )(z, ln_g.reshape(1, d_c), ln_b.reshape(1, d_c), wmix, bmix)


def _store_token_rows(ref3, x2d):
    for c in range(ref3.shape[1]):
        ref3[:, c, :] = x2d[:, c * LANES:(c + 1) * LANES]


def _load_token_rows(ref3_or_val):
    return jnp.concatenate([ref3_or_val[:, c, :] for c in range(ref3_or_val.shape[1])], axis=1)


def _router_body(h_ref, g_ref, wr_ref, br_ref, xn_o, idx_o, cw_o):
    x = h_ref[...]
    inv = lax.rsqrt(jnp.mean(x * x, axis=-1, keepdims=True) + NORM_EPS)
    xn = (x * inv) * g_ref[...]
    _store_token_rows(xn_o, xn)
    logits = jnp.dot(xn, wr_ref[...], precision=HIGHEST, preferred_element_type=F32) + br_ref[...]
    lane = lax.broadcasted_iota(I32, logits.shape, 1)
    big = jnp.int32(1 << 20)
    ninf = jnp.float32(-jnp.inf)

    def first_max(vals):
        mx = jnp.max(vals, axis=-1, keepdims=True)
        ix = jnp.min(jnp.where(vals == mx, lane, big), axis=-1, keepdims=True)
        return mx, ix

    gl = jnp.where(lane < N_EXP_GROUPS, logits, ninf)
    gmax, gtop = first_max(gl)
    gate = 1.0 / jnp.sum(jnp.exp(gl - gmax), axis=-1, keepdims=True)
    lo = N_EXP_GROUPS + EXP_PER_GROUP * gtop
    el = jnp.where((lane >= lo) & (lane < lo + EXP_PER_GROUP), logits, ninf)
    m1, i1 = first_max(el)
    m2, i2 = first_max(jnp.where(lane == i1, ninf, el))
    p2 = jnp.exp(m2 - m1)
    c1 = gate / (1.0 + p2)
    c2 = gate * p2 / (1.0 + p2)
    idx_o[...] = jnp.where(lane == 0, i1 - N_EXP_GROUPS, jnp.where(lane == 1, i2 - N_EXP_GROUPS, 0))
    cw_o[...] = jnp.where(lane == 0, c1, jnp.where(lane == 1, c2, 0.0))


def _router(h, g, w_grp, b_grp, w_exp, b_exp, tm):
    n_tok, d = h.shape
    ncol = w_grp.shape[1] + w_exp.shape[1]
    wr = jnp.concatenate([w_grp, w_exp, jnp.zeros((d, LANES - ncol), F32)], axis=1)
    br = jnp.concatenate([b_grp, b_exp, jnp.zeros((LANES - ncol,), F32)]).reshape(1, LANES)
    tok = lambda w: pl.BlockSpec((tm, w), lambda i: (i, 0))
    return pl.pallas_call(
        _router_body,
        grid=(n_tok // tm,),
        in_specs=[tok(d), pl.BlockSpec((1, d), lambda i: (0, 0)),
                  pl.BlockSpec((d, LANES), lambda i: (0, 0)),
                  pl.BlockSpec((1, LANES), lambda i: (0, 0))],
        out_specs=[pl.BlockSpec((tm, d // LANES, LANES), lambda i: (i, 0, 0)), tok(LANES), tok(LANES)],
        out_shape=[jax.ShapeDtypeStruct((n_tok, d // LANES, LANES), F32),
                   jax.ShapeDtypeStruct((n_tok, LANES), I32),
                   jax.ShapeDtypeStruct((n_tok, LANES), F32)],
        compiler_params=_cparams(("arbitrary",)),
        name="moe_router",
    )(h, g.reshape(1, d), wr, br)


def _gather_copy(src_hbm, dst_hbm, sem, tok, row):
    return pltpu.make_async_copy(src_hbm.at[pl.ds(tok, 1)], dst_hbm.at[pl.ds(row, 1)], sem)


def _moe_gather_body(tok_ref, nvalid_ref, x_hbm, o_hbm, sem, *, tm):
    t = pl.program_id(0)
    nt = pl.num_programs(0)

    def issue(tile):
        def body(i, c):
            row = tile * tm + i
            _gather_copy(x_hbm, o_hbm, sem, tok_ref[row], row).start()
            return c
        lax.fori_loop(0, tm, body, 0)

    def drain():
        def body(i, c):
            _gather_copy(x_hbm, o_hbm, sem, 0, 0).wait()
            return c
        lax.fori_loop(0, tm, body, 0)

    @pl.when(t < nvalid_ref[0])
    def _():
        issue(t)

    @pl.when((t > 0) & (t - 1 < nvalid_ref[0]))
    def _():
        drain()

    @pl.when((t == nt - 1) & (t < nvalid_ref[0]))
    def _():
        drain()


def _moe_gather(xn, row_tok, nvalid_tiles, n_tiles, tm):
    return pl.pallas_call(
        functools.partial(_moe_gather_body, tm=tm),
        grid_spec=pltpu.PrefetchScalarGridSpec(
            num_scalar_prefetch=2,
            grid=(n_tiles,),
            in_specs=[pl.BlockSpec(memory_space=pl.ANY)],
            out_specs=pl.BlockSpec(memory_space=pl.ANY),
            scratch_shapes=[pltpu.SemaphoreType.DMA(())]),
        out_shape=jax.ShapeDtypeStruct((n_tiles * tm,) + xn.shape[1:], xn.dtype),
        compiler_params=_cparams(("arbitrary",)),
        name="moe_gather",
    )(row_tok, nvalid_tiles, xn)


def _moe_ffn_body(te_ref, nvalid_ref, x_ref, rw_ref, w1_ref, w3_ref, w2_ref, y_ref, w1b, w3b, w2b):
    t = pl.program_id(0)
    valid = t < nvalid_ref[0]

    @pl.when(valid)
    def _():
        prev = te_ref[jnp.maximum(t - 1, 0)]

        @pl.when((t == 0) | (te_ref[t] != prev))
        def _():
            w1b[...] = w1_ref[0].astype(BF16)
            w3b[...] = w3_ref[0].astype(BF16)
            w2b[...] = w2_ref[0].astype(BF16)

        x = _load_token_rows(x_ref).astype(BF16)
        h1 = jnp.dot(x, w1b[...], preferred_element_type=F32)
        h3 = jnp.dot(x, w3b[...], preferred_element_type=F32)
        hh = (h1 * _sigmoid(h1)) * h3 * rw_ref[...]
        _store_token_rows(y_ref, jnp.dot(hh.astype(BF16), w2b[...], preferred_element_type=F32))

    @pl.when(jnp.logical_not(valid))
    def _():
        y_ref[...] = jnp.zeros_like(y_ref)


def _moe_ffn(xs, row_w, tile_exp, nvalid_tiles, w1, w3, w2, tm):
    rows, dc, _ = xs.shape
    d = dc * LANES
    n_tiles = rows // tm
    f = w1.shape[2]
    def last_valid(t, nv):
        return jnp.minimum(t, jnp.maximum(nv[0] - 1, 0))
    return pl.pallas_call(
        _moe_ffn_body,
        grid_spec=pltpu.PrefetchScalarGridSpec(
            num_scalar_prefetch=2,
            grid=(n_tiles,),
            in_specs=[pl.BlockSpec((tm, dc, LANES), lambda t, te, nv: (last_valid(t, nv), 0, 0)),
                      pl.BlockSpec((tm, 1), lambda t, te, nv: (last_valid(t, nv), 0)),
                      pl.BlockSpec((1, d, f), lambda t, te, nv: (te[t], 0, 0)),
                      pl.BlockSpec((1, d, f), lambda t, te, nv: (te[t], 0, 0)),
                      pl.BlockSpec((1, f, d), lambda t, te, nv: (te[t], 0, 0))],
            out_specs=pl.BlockSpec((tm, dc, LANES), lambda t, te, nv: (t, 0, 0)),
            scratch_shapes=[pltpu.VMEM((d, f), BF16), pltpu.VMEM((d, f), BF16), pltpu.VMEM((f, d), BF16)]),
        out_shape=jax.ShapeDtypeStruct((rows, dc, LANES), F32),
        compiler_params=_cparams(("arbitrary",)),
        name="moe_ffn",
    )(tile_exp, nvalid_tiles, xs, row_w, w1, w3, w2)


def _combine_copy(y_hbm, buf, sem, slot, k, i, row):
    return pltpu.make_async_copy(y_hbm.at[pl.ds(row, 1)], buf.at[slot, k, pl.ds(i, 1)], sem.at[slot])


def _moe_combine_body(pos_ref, h_ref, y_hbm, *rest, tm, final_eps):
    if final_eps is None:
        o_ref, buf, sem = rest
    else:
        gf_ref, o_ref, on_ref, buf, sem = rest
    t = pl.program_id(0)
    nt = pl.num_programs(0)

    def issue(tile, slot):
        def body(i, c):
            a = 2 * (tile * tm + i)
            _combine_copy(y_hbm, buf, sem, slot, 0, i, pos_ref[a]).start()
            _combine_copy(y_hbm, buf, sem, slot, 1, i, pos_ref[a + 1]).start()
            return c
        lax.fori_loop(0, tm, body, 0)

    @pl.when(t == 0)
    def _():
        issue(0, 0)

    @pl.when(t + 1 < nt)
    def _():
        issue(t + 1, (t + 1) % 2)

    slot = t % 2

    def drain(i, c):
        _combine_copy(y_hbm, buf, sem, slot, 0, 0, 0).wait()
        _combine_copy(y_hbm, buf, sem, slot, 1, 0, 0).wait()
        return c
    lax.fori_loop(0, tm, drain, 0)

    out = h_ref[...] + _load_token_rows(buf.at[slot, 0]) + _load_token_rows(buf.at[slot, 1])
    o_ref[...] = out
    if final_eps is not None:
        inv = lax.rsqrt(jnp.mean(out * out, axis=-1, keepdims=True) + final_eps)
        on_ref[...] = (out * inv) * gf_ref[...]


def _moe_combine(h, ys, pos, tm, final_g=None):
    n_tok, d = h.shape
    tok = pl.BlockSpec((tm, d), lambda i, p: (i, 0))
    in_specs = [tok, pl.BlockSpec(memory_space=pl.ANY)]
    args = [pos, h, ys]
    out_specs = [tok]
    out_shape = [jax.ShapeDtypeStruct((n_tok, d), F32)]
    if final_g is not None:
        in_specs.append(pl.BlockSpec((1, d), lambda i, p: (0, 0)))
        args.append(final_g.reshape(1, d))
        out_specs.append(tok)
        out_shape.append(jax.ShapeDtypeStruct((n_tok, d), F32))
    return pl.pallas_call(
        functools.partial(_moe_combine_body, tm=tm, final_eps=None if final_g is None else NORM_EPS),
        grid_spec=pltpu.PrefetchScalarGridSpec(
            num_scalar_prefetch=1,
            grid=(n_tok // tm,),
            in_specs=in_specs,
            out_specs=out_specs,
            scratch_shapes=[pltpu.VMEM((2, 2, tm) + ys.shape[1:], F32), pltpu.SemaphoreType.DMA((2,))]),
        out_shape=out_shape,
        compiler_params=_cparams(("arbitrary",)),
        name="moe_combine",
    )(*args)


def _hier_moe(h, g, w_grp, b_grp, w_exp, b_exp, w1, w3, w2, final_g=None):
    n_tok, d = h.shape
    n_exp = w1.shape[0]
    tm = MOE_TILE
    xn, idx, cw = _router(h, g, w_grp, b_grp, w_exp, b_exp, tm=256)

    n_asg = 2 * n_tok
    flat_e = idx[:, :2].reshape(n_asg)
    flat_c = cw[:, :2].reshape(n_asg)
    onehot = (flat_e[:, None] == jnp.arange(n_exp, dtype=I32)[None, :]).astype(I32)
    csum = jnp.cumsum(onehot, axis=0)
    counts = csum[-1]
    rank = jnp.take_along_axis(csum - onehot, flat_e[:, None], axis=1)[:, 0]
    padded = ((counts + tm - 1) // tm) * tm
    ends = jnp.cumsum(padded)
    pos = (ends - padded)[flat_e] + rank
    n_tiles = n_asg // tm + n_exp
    rows = n_tiles * tm
    row_tok = jnp.zeros((rows,), I32).at[pos].set(jnp.arange(n_asg, dtype=I32) // 2)
    row_w = jnp.zeros((rows,), F32).at[pos].set(flat_c).reshape(rows, 1)
    nvalid = (ends[-1] // tm).astype(I32).reshape(1)
    tile_ids = jnp.minimum(jnp.arange(n_tiles, dtype=I32), nvalid - 1)
    tile_exp = jnp.searchsorted(ends, tile_ids * tm, side="right").astype(I32)

    xs = _moe_gather(xn, row_tok, nvalid, n_tiles, tm)
    ys = _moe_ffn(xs, row_w, tile_exp, nvalid, w1, w3, w2, tm)
    return _moe_combine(h, ys, pos.astype(I32), 128, final_g)


def kernel(x_prompt, x_sample, cache_k, cache_v, page_table, state_wkv, state_shift, norm_mix, norm_ffn, norm_final, even_w_in, even_w_out, rwkv_mu, rwkv_w0, rwkv_w2, rwkv_a0, rwkv_a2, rwkv_g2, rwkv_k_k, rwkv_k_a, rwkv_r_k, rwkv_lnx_g, rwkv_lnx_b, diff_lam_q1, diff_lam_k1, diff_lam_q2, diff_lam_k2, diff_subln_g, gmlp_w_in, gmlp_ln_g, gmlp_ln_b, gmlp_w_s, gmlp_b_s, gmlp_w_out, moe_w_grp, moe_b_grp, moe_w_exp, moe_b_exp, moe_w1, moe_w3, moe_w2):
    bp, tp, d = x_prompt.shape
    bs, ts, _ = x_sample.shape
    n_p, n_s = bp * tp, bs * ts
    n_tok = n_p + n_s
    depth = norm_mix.shape[0]
    npages, psz = page_table.shape[1], cache_k.shape[2]
    past_len = npages * psz
    d_a = rwkv_w0.shape[1]
    d_ap = rwkv_mu.shape[1]
    d_q = cache_k.shape[3] * cache_k.shape[4]
    n_pool = cache_k.shape[1]
    nheads = d_a // RWKV_HEAD_DIM
    tm = 512 if n_p % 512 == 0 and n_s % 512 == 0 else 128
    tq = 512 if tp % 512 == 0 else 128
    tc = min(tp, CHUNK)

    h = jnp.concatenate([x_prompt.reshape(n_p, d), x_sample.reshape(n_s, d)], axis=0)
    pos = jnp.concatenate([jnp.tile(jnp.arange(tp), bp), jnp.tile(past_len + jnp.arange(ts), bs)])
    moe = lambda l, hh, fg=None: _hier_moe(hh, norm_ffn[l], moe_w_grp[l], moe_b_grp[l], moe_w_exp[l],
                                          moe_b_exp[l], moe_w1[l], moe_w3[l], moe_w2[l], fg)

    outs = {k: [] for k in ("k_p", "v_p", "wkv_p", "sh_p", "k_s", "v_s", "wkv_s", "sh_s", "gv_s")}
    y_norm = None
    for layer in range(depth):
        j = layer // 2
        xn = _rmsnorm(h, norm_mix[layer], NORM_EPS, BF16, tm)
        last = layer == depth - 1
        if layer % 2 == 0:
            proj = _matmul([xn], even_w_in[j], bm=tm, bn=even_w_in.shape[2] // 5, name="even_in_proj")
            prm = dict(mu=rwkv_mu[j].reshape(1, d_ap), w0=rwkv_w0[j].reshape(1, d_a), w2=rwkv_w2[j],
                       a0=rwkv_a0[j].reshape(1, d_a), a2=rwkv_a2[j], g2=rwkv_g2[j],
                       k_k=rwkv_k_k[j].reshape(1, d_a), k_a=rwkv_k_a[j].reshape(1, d_a),
                       r_k=rwkv_r_k[j].reshape(1, d_a))
            pa = proj[:, :d_ap]
            tmr = 256 if tp % 256 == 0 else 128
            pa_p = pa[:n_p].reshape(bp, tp, d_ap)
            bnd = jnp.concatenate([jnp.zeros((bp, 1, d_ap), F32), pa_p[:, tmr - 1:tp - 1:tmr]], axis=1)
            repl_p = jnp.zeros((n_p // tmr, 8, d_ap), F32).at[:, 0].set(bnd.reshape(n_p // tmr, d_ap))
            repl_p = repl_p.reshape(n_p // tmr * 8, d_ap)
            repl_s = jnp.repeat(state_shift[j], ts, axis=0)
            pre_p = _rwkv_pre(proj, repl_p, tmr, 0, n_p, tmr, prm)
            pre_s = _rwkv_pre(proj, repl_s, ts, n_p, n_s, tmr, prm)
            oa_p, wkv_p = _rwkv_scan(pre_p, rwkv_lnx_g[j], rwkv_lnx_b[j], None, bp, tp, tc)
            oa_s, wkv_s = _rwkv_scan(pre_s, rwkv_lnx_g[j], rwkv_lnx_b[j], state_wkv[j], bs, ts, ts)

            qs, qf, kf, kb, vb = _rope(proj, pos, d_ap, d_q, tm)
            vf = proj[:, d_ap + 2 * d_q:]
            lam_init = 0.8 - 0.6 * math.exp(-0.3 * layer)
            lam_p = jnp.stack([diff_lam_q1[j], diff_lam_k1[j], diff_lam_q2[j], diff_lam_k2[j]])
            ob_p = _attn_prompt(qs[:n_p], kb[:n_p], vb[:n_p], lam_p, diff_subln_g[j], bp, tp, tq, lam_init)
            ck = cache_k[j].reshape(n_pool, psz, d_q)
            cv = cache_v[j].reshape(n_pool, psz, d_q)
            ob_s = _attn_sample(qf[n_p:], kf[n_p:], vf[n_p:], ck, cv, page_table.reshape(-1).astype(I32),
                                lam_p, diff_subln_g[j], bs, ts, npages, lam_init)
            oa = jnp.concatenate([oa_p, oa_s.astype(BF16)], axis=0)
            ob = jnp.concatenate([ob_p, ob_s.astype(BF16)], axis=0)
            h = _matmul([oa, ob], even_w_out[j], bm=tm, bn=1024, res=h, name="even_out_proj")

            outs["k_p"].append(kf[:n_p].reshape(bp, tp, -1, DIFF_QK_DIM))
            outs["v_p"].append(vf[:n_p].reshape(bp, tp, -1, DIFF_V_DIM))
            outs["wkv_p"].append(wkv_p)
            outs["sh_p"].append(pa_p[:, -1])
            outs["k_s"].append(kf[n_p:].reshape(bs, ts, -1, DIFF_QK_DIM))
            outs["v_s"].append(vf[n_p:].reshape(bs, ts, -1, DIFF_V_DIM))
            outs["wkv_s"].append(wkv_s)
            outs["sh_s"].append(pa[n_p:].reshape(bs, ts, d_ap)[:, -1])
        else:
            d_c = gmlp_ln_g.shape[1]
            z = _matmul([xn], gmlp_w_in[j], bm=tm, bn=1024, act="gelu", name="gmlp_in_proj")
            ws = gmlp_w_s[j]
            bsb = gmlp_b_s[j]
            gd = d_c // GMLP_GROUPS
            lp = min(tp, CHUNK)
            wm_p = jnp.tril(ws[:, :lp, :lp])
            bm_p = jnp.broadcast_to(bsb[:, :lp, None], (GMLP_GROUPS, lp, gd))
            rep = CHUNK // ts
            eye = jnp.eye(rep, dtype=F32)
            wm_s = jnp.einsum("ab,gts->gatbs", eye, jnp.tril(ws[:, :ts, :ts])).reshape(GMLP_GROUPS, CHUNK, CHUNK)
            bm_s = jnp.broadcast_to(jnp.tile(bsb[:, :ts], (1, rep))[:, :, None], (GMLP_GROUPS, CHUNK, gd))
            (y_p,) = _gmlp(z, 0, n_p, gmlp_ln_g[j], gmlp_ln_b[j], wm_p, bm_p, False)
            y_s, v_rows = _gmlp(z, n_p, n_s, gmlp_ln_g[j], gmlp_ln_b[j], wm_s, bm_s, True)
            yin = jnp.concatenate([y_p, y_s], axis=0)
            h = _matmul([yin], gmlp_w_out[j], bm=tm, bn=1024, res=h, name="gmlp_out_proj")
            outs["gv_s"].append(v_rows.reshape(bs, ts, d_c))
        res = moe(layer, h, norm_final if last else None)
        h = res[0]
        if last:
            y_norm = res[1]

    y_prompt = y_norm[:n_p].reshape(bp, tp, d)
    y_sample = y_norm[n_p:].reshape(bs, ts, d)
    st = lambda k: jnp.stack(outs[k])
    return (y_prompt, y_sample, st("k_p"), st("v_p"), st("wkv_p"), st("sh_p"),
            st("k_s"), st("v_s"), st("wkv_s"), st("sh_s"), st("gv_s"))
```

```python
import functools
import math

import jax
import jax.numpy as jnp
from jax import lax
from jax.experimental import pallas as pl
from jax.experimental.pallas import tpu as pltpu

F32 = jnp.float32
BF16 = jnp.bfloat16
I32 = jnp.int32

RWKV_HEAD_DIM = 64
RWKV_GN_EPS = 64e-5
DIFF_QK_DIM = 64
DIFF_V_DIM = 128
ROT_DIM = 16
ROPE_THETA = 500000.0
ATTN_SCALE = DIFF_QK_DIM ** -0.5
NEG_INF = -1e30
NORM_EPS = 1e-6
SUBLN_EPS = 1e-5
GMLP_LN_EPS = 1e-5
GMLP_GROUPS = 8
CHUNK = 128
N_EXP_GROUPS = 4
EXP_PER_GROUP = 8
LORA_W, LORA_A, LORA_G = 64, 64, 128

LANES = 128
MXU_TILE = 256
VMEM_LIMIT = 56 * 1024 * 1024
MOE_TILE = 256
HIGHEST = lax.Precision.HIGHEST


def _cparams(sem):
    return pltpu.CompilerParams(dimension_semantics=sem, vmem_limit_bytes=VMEM_LIMIT)


def _rmsnorm_body(x_ref, g_ref, o_ref, *, eps):
    x = x_ref[...]
    inv = lax.rsqrt(jnp.mean(x * x, axis=-1, keepdims=True) + eps)
    o_ref[...] = ((x * inv) * g_ref[...]).astype(o_ref.dtype)


def _rmsnorm(x, g, eps, out_dtype, tm):
    m, d = x.shape
    return pl.pallas_call(
        functools.partial(_rmsnorm_body, eps=eps),
        grid=(m // tm,),
        in_specs=[pl.BlockSpec((tm, d), lambda i: (i, 0)),
                  pl.BlockSpec((1, d), lambda i: (0, 0))],
        out_specs=pl.BlockSpec((tm, d), lambda i: (i, 0)),
        out_shape=jax.ShapeDtypeStruct((m, d), out_dtype),
        compiler_params=_cparams(("arbitrary",)),
        name="rmsnorm",
    )(x, g.reshape(1, d))


def _gelu_exact(x):
    return 0.5 * x * (1.0 + lax.erf(x * (1.0 / math.sqrt(2.0))))


def _mm_body(*refs, n_x, act, has_res):
    xs = refs[:n_x]
    w_ref = refs[n_x]
    res_ref = refs[n_x + 1] if has_res else None
    o_ref = refs[n_x + 1 + has_res]
    wb_ref = refs[n_x + 2 + has_res]

    @pl.when(pl.program_id(1) == 0)
    def _():
        wb_ref[...] = w_ref[...].astype(BF16)

    kx = w_ref.shape[0] // n_x
    acc = None
    for i, x_ref in enumerate(xs):
        part = jnp.dot(x_ref[...], wb_ref[i * kx:(i + 1) * kx, :], preferred_element_type=F32)
        acc = part if acc is None else acc + part
    if act == "gelu":
        acc = _gelu_exact(acc)
    if has_res:
        acc = acc + res_ref[...]
    o_ref[...] = acc.astype(o_ref.dtype)


def _matmul(xs, w, *, bm, bn, act=None, res=None, out_dtype=F32, name="matmul"):
    m = xs[0].shape[0]
    k, n = w.shape
    kx = k // len(xs)
    in_specs = [pl.BlockSpec((bm, kx), lambda j, i: (i, 0)) for _ in xs]
    in_specs.append(pl.BlockSpec((k, bn), lambda j, i: (0, j)))
    args = list(xs) + [w]
    if res is not None:
        in_specs.append(pl.BlockSpec((bm, bn), lambda j, i: (i, j)))
        args.append(res)
    return pl.pallas_call(
        functools.partial(_mm_body, n_x=len(xs), act=act, has_res=res is not None),
        grid=(n // bn, m // bm),
        in_specs=in_specs,
        out_specs=pl.BlockSpec((bm, bn), lambda j, i: (i, j)),
        out_shape=jax.ShapeDtypeStruct((m, n), out_dtype),
        scratch_shapes=[pltpu.VMEM((k, bn), BF16)],
        compiler_params=_cparams(("arbitrary", "arbitrary")),
        name=name,
    )(*args)


def _block_ones(n, seg):
    r = lax.broadcasted_iota(I32, (n, n), 0) // seg
    c = lax.broadcasted_iota(I32, (n, n), 1) // seg
    return jnp.where(r == c, 1.0, 0.0).astype(BF16)


def _seg_sum(x, ones_bd):
    hi = x.astype(BF16)
    lo = (x - hi.astype(F32)).astype(BF16)
    outs = []
    for c in range(x.shape[1] // MXU_TILE):
        sl = slice(c * MXU_TILE, (c + 1) * MXU_TILE)
        outs.append(jnp.dot(hi[:, sl], ones_bd, preferred_element_type=F32)
                    + jnp.dot(lo[:, sl], ones_bd, preferred_element_type=F32))
    return jnp.concatenate(outs, axis=1)


def _sigmoid(x):
    return 1.0 / (1.0 + jnp.exp(-x))


def _rwkv_pre_body(pa_ref, repl_ref, mu_ref, w0_ref, w2_ref, a0_ref, a2_ref, g2_ref,
                   kk_ref, ka_ref, rk_ref,
                   r_o, dec_o, k_o, v_o, kkn_o, b_o, g_o, bonus_o, *, period, d_a):
    pa = pa_ref[...]
    tm = pa.shape[0]
    row = lax.broadcasted_iota(I32, pa.shape, 0)
    prev = pltpu.roll(pa, 1, 0)
    repl = repl_ref[...]
    if repl.shape[0] != tm:
        repl = jnp.broadcast_to(repl[0:1, :], pa.shape)
    prev = jnp.where(row % period == 0, repl, prev)
    xm = pa + (prev - pa) * mu_ref[...]
    r = xm[:, 0:d_a]
    k = xm[:, d_a:2 * d_a]
    v = xm[:, 2 * d_a:3 * d_a]
    o = 3 * d_a
    wd = xm[:, o:o + LORA_W]
    ad = xm[:, o + LORA_W:o + LORA_W + LORA_A]
    gd = xm[:, o + LORA_W + LORA_A:o + LORA_W + LORA_A + LORA_G]

    z = w0_ref[...] + jnp.dot(jnp.tanh(wd), w2_ref[...], precision=HIGHEST, preferred_element_type=F32)
    w = jnp.minimum(z, 0.0) - jnp.log1p(jnp.exp(-jnp.abs(z))) - 0.5
    dec = jnp.exp(-jnp.exp(w))
    a = _sigmoid(a0_ref[...] + jnp.dot(ad, a2_ref[...], precision=HIGHEST, preferred_element_type=F32))
    g = jnp.dot(_sigmoid(gd).astype(BF16), g2_ref[...].astype(BF16), preferred_element_type=F32)

    ones_bd = _block_ones(MXU_TILE, RWKV_HEAD_DIM)
    kk = k * kk_ref[...]
    nrm = jnp.sqrt(_seg_sum(kk * kk, ones_bd))
    kkn = kk / jnp.maximum(nrm, 1e-12)
    kh = k * (1.0 + (a - 1.0) * ka_ref[...])
    bonus = _seg_sum(r * kh * rk_ref[...], ones_bd) * v

    r_o[...] = r
    dec_o[...] = dec
    k_o[...] = kh
    v_o[...] = v
    kkn_o[...] = kkn
    b_o[...] = kkn * a
    g_o[...] = g
    bonus_o[...] = bonus


def _rwkv_pre(proj, repl, period, row0, rows, tm, prm):
    d_a = prm["w0"].shape[1]
    d_ap = prm["mu"].shape[1]
    nt = rows // tm
    t0 = row0 // tm
    vec = lambda n: pl.BlockSpec((1, n), lambda i: (0, 0))
    full = lambda a: pl.BlockSpec(a.shape, lambda i: (0, 0))
    out_spec = pl.BlockSpec((tm, d_a), lambda i: (i, 0))
    outs = pl.pallas_call(
        functools.partial(_rwkv_pre_body, period=period, d_a=d_a),
        grid=(nt,),
        in_specs=[pl.BlockSpec((tm, d_ap), lambda i: (t0 + i, 0)),
                  pl.BlockSpec((repl.shape[0] // nt, d_ap), lambda i: (i, 0)),
                  vec(d_ap), vec(d_a), full(prm["w2"]), vec(d_a), full(prm["a2"]), full(prm["g2"]),
                  vec(d_a), vec(d_a), vec(d_a)],
        out_specs=[out_spec] * 8,
        out_shape=[jax.ShapeDtypeStruct((rows, d_a), F32)] * 8,
        compiler_params=_cparams(("arbitrary",)),
        name="rwkv_pre",
    )(proj, repl, prm["mu"], prm["w0"], prm["w2"], prm["a0"], prm["a2"], prm["g2"],
      prm["k_k"], prm["k_a"], prm["r_k"])
    return outs


def _rwkv_scan_body(*refs, tc, tp, nheads, nb, has_state):
    if has_state:
        s0_ref = refs[0]
        refs = refs[1:]
    (r_ref, w_ref, k_ref, v_ref, kk_ref, b_ref, g_ref, bonus_ref, lg_ref, lb_ref,
     o_ref, sout_ref, s_scr, vth_scr, vtl_scr, ot_scr) = refs
    n = RWKV_HEAD_DIM
    c = pl.program_id(1)
    assert tp == 2 * n
    nh = nb * nheads
    heads = [(b, h) for b in range(nb) for h in range(nheads)]

    @pl.when(c == 0)
    def _():
        if has_state:
            zero = jnp.zeros((n, n), F32)
            for i, (b, h) in enumerate(heads):
                s0 = s0_ref[b, h]
                s_scr[i] = jnp.concatenate([s0, zero] if h % 2 == 0 else [zero, s0], axis=1)
        else:
            s_scr[...] = jnp.zeros_like(s_scr)

    for i, (b, h) in enumerate(heads):
        vh = v_ref[b, :, h * n:(h + 1) * n]
        if tp != tc:
            vh = jnp.concatenate([vh, jnp.zeros((tp - tc, n), F32)], axis=0)
        vt = vh.T
        hi = vt.astype(BF16)
        vth_scr[i * n:(i + 1) * n, :] = hi
        vtl_scr[i * n:(i + 1) * n, :] = (vt - hi.astype(F32)).astype(BF16)
    ot_scr[...] = jnp.zeros_like(ot_scr)

    lane = lax.broadcasted_iota(I32, (1, n, tp), 2)
    trow = lax.broadcasted_iota(I32, (tp, 2 * n), 0)
    ones_w = jnp.ones((2 * n, 2 * n), BF16)
    head_par = lax.broadcasted_iota(I32, (nh, 1, 2 * n), 0) % 2
    lane_par = lax.broadcasted_iota(I32, (nh, 1, 2 * n), 2) // n
    own = head_par == lane_par

    sub = 8

    def per_head(ref, base, masked):
        xs = [ref[b, pl.ds(base, sub), :] for b in range(nb)]
        x = jnp.stack([xs[b][:, (h // 2) * 2 * n:(h // 2 + 1) * 2 * n] for b, h in heads], axis=0)
        return jnp.where(own, x, 0.0) if masked else x

    def group(tg, carry):
        base = pl.multiple_of(tg * sub, sub)
        kk8 = per_head(kk_ref, base, True)
        b8 = per_head(b_ref, base, True)
        k8 = per_head(k_ref, base, True)
        w8 = per_head(w_ref, base, False)
        r8 = per_head(r_ref, base, False)
        st = s_scr[...]
        vth = vth_scr[...]
        vtl = vtl_scr[...]
        ot = ot_scr[...]
        for s in range(sub):
            t = base + s
            pick = jnp.where(trow == t, 1.0, 0.0).astype(BF16)
            v_col = (jnp.dot(vth, pick, preferred_element_type=F32)
                     + jnp.dot(vtl, pick, preferred_element_type=F32)).reshape(nh, n, 2 * n)
            sa = -jnp.sum(st * kk8[:, s:s + 1], axis=-1, keepdims=True)
            st = st * w8[:, s:s + 1] + sa * b8[:, s:s + 1] + v_col * k8[:, s:s + 1]
            q = (st * r8[:, s:s + 1]).reshape(nh * n, 2 * n).astype(BF16)
            o_col = jnp.dot(q, ones_w, preferred_element_type=F32).reshape(nh, n, 2 * n)
            ot = jnp.where(lane == t, o_col, ot)
        s_scr[...] = st
        ot_scr[...] = ot
        return carry

    lax.fori_loop(0, tc // sub, group, 0)

    for b in range(nb):
        for hp in range(nheads // 2):
            parts = []
            for h in (2 * hp, 2 * hp + 1):
                oc = ot_scr[b * nheads + h]
                mean = jnp.mean(oc, axis=0, keepdims=True)
                d = oc - mean
                var = jnp.mean(d * d, axis=0, keepdims=True)
                on = d * lax.rsqrt(var + RWKV_GN_EPS)
                parts.append(on.T[:tc])
            sl = slice(hp * 2 * n, (hp + 1) * 2 * n)
            on2 = jnp.concatenate(parts, axis=1)
            o = (on2 * lg_ref[:, sl] + lb_ref[:, sl] + bonus_ref[b, :, sl]) * g_ref[b, :, sl]
            o_ref[b, :, sl] = o.astype(o_ref.dtype)

    @pl.when(c == pl.num_programs(1) - 1)
    def _():
        for i, (b, h) in enumerate(heads):
            s_h = s_scr[i]
            sout_ref[b, h] = s_h[:, 0:n] if h % 2 == 0 else s_h[:, n:2 * n]


def _rwkv_scan(pre, lnx_g, lnx_b, state, nseq, tseq, tc, nb):
    rows, d_a = pre[0].shape
    nheads = d_a // RWKV_HEAD_DIM
    nch = tseq // tc
    tp = max(tc, LANES)
    n = RWKV_HEAD_DIM
    nh = nb * nheads
    tok = pl.BlockSpec((nb, tc, d_a), lambda s, c: (s, c, 0))
    vec = pl.BlockSpec((1, d_a), lambda s, c: (0, 0))
    st = pl.BlockSpec((nb, nheads, n, n), lambda s, c: (s, 0, 0, 0))
    has_state = state is not None
    in_specs = ([st] if has_state else []) + [tok] * 8 + [vec, vec]
    args = ([state] if has_state else []) + [a.reshape(nseq, tseq, d_a) for a in pre] + [
        lnx_g.reshape(1, d_a), lnx_b.reshape(1, d_a)]
    oa, s_fin = pl.pallas_call(
        functools.partial(_rwkv_scan_body, tc=tc, tp=tp, nheads=nheads, nb=nb, has_state=has_state),
        grid=(nseq // nb, nch),
        in_specs=in_specs,
        out_specs=[tok, st],
        out_shape=[jax.ShapeDtypeStruct((nseq, tseq, d_a), BF16 if tc % 16 == 0 else F32),
                   jax.ShapeDtypeStruct((nseq, nheads, n, n), F32)],
        scratch_shapes=[pltpu.VMEM((nh, n, 2 * n), F32),
                        pltpu.VMEM((nh * n, tp), BF16),
                        pltpu.VMEM((nh * n, tp), BF16),
                        pltpu.VMEM((nh, n, tp), F32)],
        compiler_params=_cparams(("arbitrary", "arbitrary")),
        name="rwkv_scan",
    )(*args)
    return oa.reshape(rows, d_a), s_fin


def _rope_body(q_ref, k_ref, v_ref, c_ref, s1_ref, s2_ref, qs_o, qf_o, kf_o, kb_o, vb_o):
    c, s1, s2 = c_ref[...], s1_ref[...], s2_ref[...]
    half = ROT_DIM // 2
    w = q_ref.shape[1]

    def rot(x):
        return x * c + pltpu.roll(x, w - half, 1) * s1 + pltpu.roll(x, half, 1) * s2

    qr = rot(q_ref[...]) * ATTN_SCALE
    qf_o[...] = qr
    qs_o[...] = qr.astype(BF16)
    kr = rot(k_ref[...])
    kf_o[...] = kr
    kb_o[...] = kr.astype(BF16)
    vb_o[...] = v_ref[...].astype(BF16)


def _rope(proj, pos, d_ap, d_q, tm):
    n_tok = proj.shape[0]
    w = MXU_TILE
    half = ROT_DIM // 2
    inv_freq = ROPE_THETA ** (-jnp.arange(half, dtype=F32) / half)
    ang = pos.astype(F32)[:, None] * inv_freq[None, :]
    cos, sin = jnp.cos(ang), jnp.sin(ang)
    pad = DIFF_QK_DIM - ROT_DIM
    ones = jnp.ones((n_tok, pad), F32)
    zeros = jnp.zeros((n_tok, pad), F32)
    zh = jnp.zeros((n_tok, half), F32)
    reps = w // DIFF_QK_DIM
    c_t = jnp.tile(jnp.concatenate([cos, cos, ones], axis=1), (1, reps))
    s1_t = jnp.tile(jnp.concatenate([-sin, zh, zeros], axis=1), (1, reps))
    s2_t = jnp.tile(jnp.concatenate([zh, sin, zeros], axis=1), (1, reps))
    nq = d_q // w
    qb, kb, vb = d_ap // w, d_ap // w + nq, d_ap // w + 2 * nq
    tab = pl.BlockSpec((tm, w), lambda i, j: (i, 0))
    out = pl.BlockSpec((tm, w), lambda i, j: (i, j))
    return pl.pallas_call(
        _rope_body,
        grid=(n_tok // tm, nq),
        in_specs=[pl.BlockSpec((tm, w), lambda i, j: (i, qb + j)),
                  pl.BlockSpec((tm, w), lambda i, j: (i, kb + j)),
                  pl.BlockSpec((tm, w), lambda i, j: (i, vb + j)),
                  tab, tab, tab],
        out_specs=[out, out, out, out, out],
        out_shape=[jax.ShapeDtypeStruct((n_tok, d_q), BF16),
                   jax.ShapeDtypeStruct((n_tok, d_q), F32),
                   jax.ShapeDtypeStruct((n_tok, d_q), F32),
                   jax.ShapeDtypeStruct((n_tok, d_q), BF16),
                   jax.ShapeDtypeStruct((n_tok, d_q), BF16)],
        compiler_params=_cparams(("arbitrary", "arbitrary")),
        name="rope",
    )(proj, proj, proj, c_t, s1_t, s2_t)


def _lambda(lam_ref, lam_init):
    lp = lam_ref[...]
    l1 = jnp.sum(lp[0:1] * lp[1:2], axis=-1, keepdims=True)
    l2 = jnp.sum(lp[2:3] * lp[3:4], axis=-1, keepdims=True)
    return jnp.exp(l1) - jnp.exp(l2) + lam_init


def _subln(o, g_ref, lam_init):
    inv = lax.rsqrt(jnp.mean(o * o, axis=-1, keepdims=True) + SUBLN_EPS)
    return (o * inv) * g_ref[...] * (1.0 - lam_init)


def _online_softmax_update(s, v, m_scr, l_scr, acc_scr):
    m_prev = m_scr[...]
    m_new = jnp.maximum(m_prev, jnp.max(s, axis=-1, keepdims=True))
    alpha = jnp.exp(m_prev - m_new)
    p = jnp.exp(s - m_new[:, 0:1])
    l_scr[...] = alpha * l_scr[...] + jnp.sum(p, axis=-1, keepdims=True)
    acc_scr[...] = alpha[:, 0:1] * acc_scr[...] + jnp.dot(p.astype(BF16), v, preferred_element_type=F32)
    m_scr[...] = m_new


def _attn_prompt_body(q_ref, k_ref, v_ref, lam_ref, g_ref, o_ref, qq_scr, m_scr, l_scr, acc_scr,
                      *, tq, lam_init):
    qi = pl.program_id(2)
    ki = pl.program_id(3)

    @pl.when(ki == 0)
    def _():
        q = q_ref[...]
        lane = lax.broadcasted_iota(I32, q.shape, 1)
        zero = jnp.zeros_like(q)
        qq_scr[0:tq] = jnp.where(lane < DIFF_QK_DIM, q, zero)
        qq_scr[tq:2 * tq] = jnp.where(lane >= DIFF_QK_DIM, q, zero)
        m_scr[...] = jnp.full_like(m_scr, NEG_INF)
        l_scr[...] = jnp.zeros_like(l_scr)
        acc_scr[...] = jnp.zeros_like(acc_scr)

    def scores():
        return lax.dot_general(qq_scr[...], k_ref[...], (((1,), (1,)), ((), ())),
                               preferred_element_type=F32)

    @pl.when(ki < qi)
    def _():
        _online_softmax_update(scores(), v_ref[...], m_scr, l_scr, acc_scr)

    @pl.when(ki == qi)
    def _():
        s = scores()
        row = lax.broadcasted_iota(I32, s.shape, 0) % tq
        col = lax.broadcasted_iota(I32, s.shape, 1)
        s = jnp.where(row >= col, s, NEG_INF)
        _online_softmax_update(s, v_ref[...], m_scr, l_scr, acc_scr)
        lam = _lambda(lam_ref, lam_init)
        o1 = acc_scr[0:tq] / l_scr[0:tq, 0:1]
        o2 = acc_scr[tq:2 * tq] / l_scr[tq:2 * tq, 0:1]
        o_ref[...] = _subln(o1 - lam * o2, g_ref, lam_init).astype(o_ref.dtype)


def _attn_prompt(qs, kb, vb, lam_p, subln_g, nseq, tseq, tq, lam_init):
    rows, d_q = qs.shape
    nh = d_q // DIFF_V_DIM
    nq = tseq // tq
    w = DIFF_V_DIM
    return pl.pallas_call(
        functools.partial(_attn_prompt_body, tq=tq, lam_init=lam_init),
        grid=(nseq, nh, nq, nq),
        in_specs=[pl.BlockSpec((tq, w), lambda b, h, i, j: (b * nq + i, h)),
                  pl.BlockSpec((tq, w), lambda b, h, i, j: (b * nq + jnp.minimum(i, j), h)),
                  pl.BlockSpec((tq, w), lambda b, h, i, j: (b * nq + jnp.minimum(i, j), h)),
                  pl.BlockSpec(lam_p.shape, lambda b, h, i, j: (0, 0)),
                  pl.BlockSpec((1, w), lambda b, h, i, j: (0, 0))],
        out_specs=pl.BlockSpec((tq, w), lambda b, h, i, j: (b * nq + i, h)),
        out_shape=jax.ShapeDtypeStruct((nseq * tseq, d_q), BF16),
        scratch_shapes=[pltpu.VMEM((2 * tq, w), BF16),
                        pltpu.VMEM((2 * tq, LANES), F32),
                        pltpu.VMEM((2 * tq, LANES), F32),
                        pltpu.VMEM((2 * tq, w), F32)],
        compiler_params=_cparams(("arbitrary", "arbitrary", "arbitrary", "arbitrary")),
        name="attn_prompt",
    )(qs, kb, vb, lam_p, subln_g.reshape(1, w))


def _attn_sample_body(pt_ref, q_ref, *refs, ts, lam_init, pages_per_step):
    kc_refs = refs[:pages_per_step]
    vc_refs = refs[pages_per_step:2 * pages_per_step]
    kn_ref, vn_ref, lam_ref, g_ref, o_ref, qbd_scr, m_scr, l_scr, acc_scr = refs[2 * pages_per_step:]
    p = pl.program_id(1)
    nheads_qk = q_ref.shape[1] // DIFF_QK_DIM

    @pl.when(p == 0)
    def _():
        q = q_ref[...]
        head = lax.broadcasted_iota(I32, q.shape, 1) // DIFF_QK_DIM
        zero = jnp.zeros_like(q)
        rows = [jnp.where(head == m, q, zero) for m in range(nheads_qk)]
        qbd_scr[...] = jnp.concatenate(rows, axis=0).astype(BF16)
        m_scr[...] = jnp.full_like(m_scr, NEG_INF)
        l_scr[...] = jnp.zeros_like(l_scr)
        acc_scr[...] = jnp.zeros_like(acc_scr)

    def scores(kmat):
        return lax.dot_general(qbd_scr[...], kmat, (((1,), (1,)), ((), ())), preferred_element_type=F32)

    def page_rows(ref):
        return jnp.concatenate([ref[:, m, :] for m in range(ref.shape[1])], axis=1)

    kcat = jnp.concatenate([page_rows(r) for r in kc_refs], axis=0).astype(BF16)
    vcat = jnp.concatenate([page_rows(r) for r in vc_refs], axis=0).astype(BF16)
    _online_softmax_update(scores(kcat), vcat, m_scr, l_scr, acc_scr)

    @pl.when(p == pl.num_programs(1) - 1)
    def _():
        padk = jnp.zeros((LANES - ts, kn_ref.shape[1]), F32)
        kn = jnp.concatenate([kn_ref[...], padk], axis=0).astype(BF16)
        vn = jnp.concatenate([vn_ref[...], padk], axis=0).astype(BF16)
        s = scores(kn)
        row = lax.broadcasted_iota(I32, s.shape, 0) % ts
        col = lax.broadcasted_iota(I32, s.shape, 1)
        s = jnp.where(row >= col, s, NEG_INF)
        _online_softmax_update(s, vn, m_scr, l_scr, acc_scr)
        lam = _lambda(lam_ref, lam_init)
        w = DIFF_V_DIM
        for h in range(nheads_qk // 2):
            r1 = slice(2 * h * ts, (2 * h + 1) * ts)
            r2 = slice((2 * h + 1) * ts, (2 * h + 2) * ts)
            cs = slice(h * w, (h + 1) * w)
            o1 = acc_scr[r1, cs] / l_scr[r1, 0:1]
            o2 = acc_scr[r2, cs] / l_scr[r2, 0:1]
            o_ref[:, cs] = _subln(o1 - lam * o2, g_ref, lam_init).astype(o_ref.dtype)


def _attn_sample(qs, kb, vb, cache_k, cache_v, layer_j, page_flat, lam_p, subln_g, nb, ts, npages, lam_init):
    d_q = qs.shape[1]
    nrow = (d_q // DIFF_QK_DIM) * ts
    gpp = 4 if npages % 4 == 0 else (2 if npages % 2 == 0 else 1)
    tok = pl.BlockSpec((ts, d_q), lambda b, p, pt: (b, 0))

    def page(cache, g):
        return pl.BlockSpec((None, None) + cache.shape[2:],
                            lambda b, p, pt: (layer_j, pt[b * npages + p * gpp + g], 0, 0, 0))

    return pl.pallas_call(
        functools.partial(_attn_sample_body, ts=ts, lam_init=lam_init, pages_per_step=gpp),
        grid_spec=pltpu.PrefetchScalarGridSpec(
            num_scalar_prefetch=1,
            grid=(nb, npages // gpp),
            in_specs=[tok] + [page(cache_k, g) for g in range(gpp)] + [page(cache_v, g) for g in range(gpp)]
                     + [tok, tok,
                        pl.BlockSpec(lam_p.shape, lambda b, p, pt: (0, 0)),
                        pl.BlockSpec((1, DIFF_V_DIM), lambda b, p, pt: (0, 0))],
            out_specs=tok,
            scratch_shapes=[pltpu.VMEM((nrow, d_q), BF16),
                            pltpu.VMEM((nrow, LANES), F32),
                            pltpu.VMEM((nrow, LANES), F32),
                            pltpu.VMEM((nrow, d_q), F32)]),
        out_shape=jax.ShapeDtypeStruct((nb * ts, d_q), F32),
        compiler_params=_cparams(("arbitrary", "arbitrary")),
        name="attn_sample",
    )(page_flat, qs, *([cache_k] * gpp), *([cache_v] * gpp), kb, vb, lam_p, subln_g.reshape(1, DIFF_V_DIM))


def _gmlp_body(z_ref, lg_ref, lb_ref, wm_ref, bm_ref, y_ref, *maybe_v, d_c):
    u = z_ref[:, 0:d_c]
    v = z_ref[:, d_c:2 * d_c]
    mean = jnp.mean(v, axis=-1, keepdims=True)
    d = v - mean
    var = jnp.mean(d * d, axis=-1, keepdims=True)
    vn = (d * lax.rsqrt(var + GMLP_LN_EPS)) * lg_ref[...] + lb_ref[...]
    if maybe_v:
        maybe_v[0][...] = vn
    gd = d_c // GMLP_GROUPS
    vb = vn.astype(BF16)
    for g in range(GMLP_GROUPS):
        sl = slice(g * gd, (g + 1) * gd)
        mixed = jnp.dot(wm_ref[g].astype(BF16), vb[:, sl], preferred_element_type=F32) + bm_ref[g]
        y_ref[:, sl] = (u[:, sl] * mixed).astype(y_ref.dtype)


def _gmlp(z, row0, rows, ln_g, ln_b, wmix, bmix, emit_v):
    d_c = z.shape[1] // 2
    tm = CHUNK
    t0 = row0 // tm
    gd = d_c // GMLP_GROUPS
    out_specs = [pl.BlockSpec((tm, d_c), lambda i: (i, 0))]
    out_shape = [jax.ShapeDtypeStruct((rows, d_c), BF16)]
    if emit_v:
        out_specs.append(pl.BlockSpec((tm, d_c), lambda i: (i, 0)))
        out_shape.append(jax.ShapeDtypeStruct((rows, d_c), F32))
    return pl.pallas_call(
        functools.partial(_gmlp_body, d_c=d_c),
        grid=(rows // tm,),
        in_specs=[pl.BlockSpec((tm, 2 * d_c), lambda i: (t0 + i, 0)),
                  pl.BlockSpec((1, d_c), lambda i: (0, 0)),
                  pl.BlockSpec((1, d_c), lambda i: (0, 0)),
                  pl.BlockSpec((GMLP_GROUPS, tm, tm), lambda i: (0, 0, 0)),
                  pl.BlockSpec((GMLP_GROUPS, tm, gd), lambda i: (0, 0, 0))],
        out_specs=out_specs,
        out_shape=out_shape,
        compiler_params=_cparams(("arbitrary",)),
        name="gmlp_gate",
    )(z, ln_g.reshape(1, d_c), ln_b.reshape(1, d_c), wmix, bmix)


def _store_token_rows(ref3, x2d):
    for c in range(ref3.shape[1]):
        ref3[:, c, :] = x2d[:, c * LANES:(c + 1) * LANES]


def _load_token_rows(ref3_or_val):
    return jnp.concatenate([ref3_or_val[:, c, :] for c in range(ref3_or_val.shape[1])], axis=1)


def _router_body(h_ref, g_ref, wr_ref, br_ref, xn_o, idx_o, cw_o):
    x = h_ref[...]
    inv = lax.rsqrt(jnp.mean(x * x, axis=-1, keepdims=True) + NORM_EPS)
    xn = (x * inv) * g_ref[...]
    _store_token_rows(xn_o, xn)
    logits = jnp.dot(xn, wr_ref[...], precision=HIGHEST, preferred_element_type=F32) + br_ref[...]
    lane = lax.broadcasted_iota(I32, logits.shape, 1)
    big = jnp.int32(1 << 20)
    ninf = jnp.float32(-jnp.inf)

    def first_max(vals):
        mx = jnp.max(vals, axis=-1, keepdims=True)
        ix = jnp.min(jnp.where(vals == mx, lane, big), axis=-1, keepdims=True)
        return mx, ix

    gl = jnp.where(lane < N_EXP_GROUPS, logits, ninf)
    gmax, gtop = first_max(gl)
    gate = 1.0 / jnp.sum(jnp.exp(gl - gmax), axis=-1, keepdims=True)
    lo = N_EXP_GROUPS + EXP_PER_GROUP * gtop
    el = jnp.where((lane >= lo) & (lane < lo + EXP_PER_GROUP), logits, ninf)
    m1, i1 = first_max(el)
    m2, i2 = first_max(jnp.where(lane == i1, ninf, el))
    p2 = jnp.exp(m2 - m1)
    c1 = gate / (1.0 + p2)
    c2 = gate * p2 / (1.0 + p2)
    idx_o[...] = jnp.where(lane == 0, i1 - N_EXP_GROUPS, jnp.where(lane == 1, i2 - N_EXP_GROUPS, 0))
    cw_o[...] = jnp.where(lane == 0, c1, jnp.where(lane == 1, c2, 0.0))


def _router(h, g, w_grp, b_grp, w_exp, b_exp, tm):
    n_tok, d = h.shape
    ncol = w_grp.shape[1] + w_exp.shape[1]
    wr = jnp.concatenate([w_grp, w_exp, jnp.zeros((d, LANES - ncol), F32)], axis=1)
    br = jnp.concatenate([b_grp, b_exp, jnp.zeros((LANES - ncol,), F32)]).reshape(1, LANES)
    tok = lambda w: pl.BlockSpec((tm, w), lambda i: (i, 0))
    return pl.pallas_call(
        _router_body,
        grid=(n_tok // tm,),
        in_specs=[tok(d), pl.BlockSpec((1, d), lambda i: (0, 0)),
                  pl.BlockSpec((d, LANES), lambda i: (0, 0)),
                  pl.BlockSpec((1, LANES), lambda i: (0, 0))],
        out_specs=[pl.BlockSpec((tm, d // LANES, LANES), lambda i: (i, 0, 0)), tok(LANES), tok(LANES)],
        out_shape=[jax.ShapeDtypeStruct((n_tok, d // LANES, LANES), F32),
                   jax.ShapeDtypeStruct((n_tok, LANES), I32),
                   jax.ShapeDtypeStruct((n_tok, LANES), F32)],
        compiler_params=_cparams(("arbitrary",)),
        name="moe_router",
    )(h, g.reshape(1, d), wr, br)


def _gather_copy(x_hbm, xbuf, sem, slot, i, tok):
    return pltpu.make_async_copy(x_hbm.at[pl.ds(tok, 1)], xbuf.at[slot, pl.ds(i, 1)], sem.at[slot])


def _moe_ffn_body(te_ref, nvalid_ref, tok_ref, x_hbm, rw_ref, w1_ref, w3_ref, w2_ref, y_ref,
                  xbuf, sem, w1b, w3b, w2b, *, tm):
    t = pl.program_id(0)
    nv = nvalid_ref[0]
    valid = t < nv

    def issue(tile, slot):
        def body(i, c):
            _gather_copy(x_hbm, xbuf, sem, slot, i, tok_ref[tile * tm + i]).start()
            return c
        lax.fori_loop(0, tm, body, 0, unroll=8)

    @pl.when((t == 0) & valid)
    def _():
        issue(0, 0)

    @pl.when(t + 1 < nv)
    def _():
        issue(t + 1, (t + 1) % 2)

    @pl.when(valid)
    def _():
        prev = te_ref[jnp.maximum(t - 1, 0)]

        @pl.when((t == 0) | (te_ref[t] != prev))
        def _():
            w1b[...] = w1_ref[0].astype(BF16)
            w3b[...] = w3_ref[0].astype(BF16)
            w2b[...] = w2_ref[0].astype(BF16)

        slot = t % 2

        def drain(i, c):
            _gather_copy(x_hbm, xbuf, sem, slot, 0, 0).wait()
            return c
        lax.fori_loop(0, tm, drain, 0, unroll=8)

        x = _load_token_rows(xbuf.at[slot]).astype(BF16)
        h1 = jnp.dot(x, w1b[...], preferred_element_type=F32)
        h3 = jnp.dot(x, w3b[...], preferred_element_type=F32)
        hh = (h1 * _sigmoid(h1)) * h3 * rw_ref[...]
        _store_token_rows(y_ref, jnp.dot(hh.astype(BF16), w2b[...], preferred_element_type=F32))

    @pl.when(jnp.logical_not(valid))
    def _():
        y_ref[...] = jnp.zeros_like(y_ref)


def _moe_ffn(xn, row_tok, row_w, tile_exp, nvalid_tiles, w1, w3, w2, n_tiles, tm):
    _, dc, _ = xn.shape
    d = dc * LANES
    rows = n_tiles * tm
    f = w1.shape[2]
    def last_valid(t, nv):
        return jnp.minimum(t, jnp.maximum(nv[0] - 1, 0))
    return pl.pallas_call(
        functools.partial(_moe_ffn_body, tm=tm),
        grid_spec=pltpu.PrefetchScalarGridSpec(
            num_scalar_prefetch=3,
            grid=(n_tiles,),
            in_specs=[pl.BlockSpec(memory_space=pl.ANY),
                      pl.BlockSpec((tm, 1), lambda t, te, nv, tk: (last_valid(t, nv), 0)),
                      pl.BlockSpec((1, d, f), lambda t, te, nv, tk: (te[t], 0, 0)),
                      pl.BlockSpec((1, d, f), lambda t, te, nv, tk: (te[t], 0, 0)),
                      pl.BlockSpec((1, f, d), lambda t, te, nv, tk: (te[t], 0, 0))],
            out_specs=pl.BlockSpec((tm, dc, LANES), lambda t, te, nv, tk: (t, 0, 0)),
            scratch_shapes=[pltpu.VMEM((2, tm, dc, LANES), F32), pltpu.SemaphoreType.DMA((2,)),
                            pltpu.VMEM((d, f), BF16), pltpu.VMEM((d, f), BF16), pltpu.VMEM((f, d), BF16)]),
        out_shape=jax.ShapeDtypeStruct((rows, dc, LANES), F32),
        compiler_params=_cparams(("arbitrary",)),
        name="moe_ffn",
    )(tile_exp, nvalid_tiles, row_tok, xn, row_w, w1, w3, w2)


def _combine_copy(y_hbm, buf, sem, slot, k, i, row):
    return pltpu.make_async_copy(y_hbm.at[pl.ds(row, 1)], buf.at[slot, k, pl.ds(i, 1)], sem.at[slot])


def _moe_combine_body(pos_ref, h_ref, y_hbm, *rest, tm, final_eps):
    if final_eps is None:
        o_ref, buf, sem = rest
    else:
        gf_ref, o_ref, on_ref, buf, sem = rest
    t = pl.program_id(0)
    nt = pl.num_programs(0)

    def issue(tile, slot):
        def body(i, c):
            a = 2 * (tile * tm + i)
            _combine_copy(y_hbm, buf, sem, slot, 0, i, pos_ref[a]).start()
            _combine_copy(y_hbm, buf, sem, slot, 1, i, pos_ref[a + 1]).start()
            return c
        lax.fori_loop(0, tm, body, 0)

    @pl.when(t == 0)
    def _():
        issue(0, 0)

    @pl.when(t + 1 < nt)
    def _():
        issue(t + 1, (t + 1) % 2)

    slot = t % 2

    def drain(i, c):
        _combine_copy(y_hbm, buf, sem, slot, 0, 0, 0).wait()
        _combine_copy(y_hbm, buf, sem, slot, 1, 0, 0).wait()
        return c
    lax.fori_loop(0, tm, drain, 0)

    out = h_ref[...] + _load_token_rows(buf.at[slot, 0]) + _load_token_rows(buf.at[slot, 1])
    o_ref[...] = out
    if final_eps is not None:
        inv = lax.rsqrt(jnp.mean(out * out, axis=-1, keepdims=True) + final_eps)
        on_ref[...] = (out * inv) * gf_ref[...]


def _moe_combine(h, ys, pos, tm, final_g=None):
    n_tok, d = h.shape
    tok = pl.BlockSpec((tm, d), lambda i, p: (i, 0))
    in_specs = [tok, pl.BlockSpec(memory_space=pl.ANY)]
    args = [pos, h, ys]
    out_specs = [tok]
    out_shape = [jax.ShapeDtypeStruct((n_tok, d), F32)]
    if final_g is not None:
        in_specs.append(pl.BlockSpec((1, d), lambda i, p: (0, 0)))
        args.append(final_g.reshape(1, d))
        out_specs.append(tok)
        out_shape.append(jax.ShapeDtypeStruct((n_tok, d), F32))
    return pl.pallas_call(
        functools.partial(_moe_combine_body, tm=tm, final_eps=None if final_g is None else NORM_EPS),
        grid_spec=pltpu.PrefetchScalarGridSpec(
            num_scalar_prefetch=1,
            grid=(n_tok // tm,),
            in_specs=in_specs,
            out_specs=out_specs,
            scratch_shapes=[pltpu.VMEM((2, 2, tm) + ys.shape[1:], F32), pltpu.SemaphoreType.DMA((2,))]),
        out_shape=out_shape,
        compiler_params=_cparams(("arbitrary",)),
        name="moe_combine",
    )(*args)


def _hier_moe(h, g, w_grp, b_grp, w_exp, b_exp, w1, w3, w2, final_g=None):
    n_tok, d = h.shape
    n_exp = w1.shape[0]
    tm = MOE_TILE
    xn, idx, cw = _router(h, g, w_grp, b_grp, w_exp, b_exp, tm=256)

    n_asg = 2 * n_tok
    flat_e = idx[:, :2].reshape(n_asg)
    flat_c = cw[:, :2].reshape(n_asg)
    onehot = (flat_e[:, None] == jnp.arange(n_exp, dtype=I32)[None, :]).astype(I32)
    csum = jnp.cumsum(onehot, axis=0)
    counts = csum[-1]
    rank = jnp.take_along_axis(csum - onehot, flat_e[:, None], axis=1)[:, 0]
    padded = ((counts + tm - 1) // tm) * tm
    ends = jnp.cumsum(padded)
    pos = (ends - padded)[flat_e] + rank
    n_tiles = n_asg // tm + n_exp
    rows = n_tiles * tm
    row_tok = jnp.zeros((rows,), I32).at[pos].set(jnp.arange(n_asg, dtype=I32) // 2)
    row_w = jnp.zeros((rows,), F32).at[pos].set(flat_c).reshape(rows, 1)
    nvalid = (ends[-1] // tm).astype(I32).reshape(1)
    tile_ids = jnp.minimum(jnp.arange(n_tiles, dtype=I32), nvalid - 1)
    tile_exp = jnp.searchsorted(ends, tile_ids * tm, side="right").astype(I32)

    ys = _moe_ffn(xn, row_tok, row_w, tile_exp, nvalid, w1, w3, w2, n_tiles, tm)
    return _moe_combine(h, ys, pos.astype(I32), 128, final_g)


def kernel(x_prompt, x_sample, cache_k, cache_v, page_table, state_wkv, state_shift, norm_mix, norm_ffn, norm_final, even_w_in, even_w_out, rwkv_mu, rwkv_w0, rwkv_w2, rwkv_a0, rwkv_a2, rwkv_g2, rwkv_k_k, rwkv_k_a, rwkv_r_k, rwkv_lnx_g, rwkv_lnx_b, diff_lam_q1, diff_lam_k1, diff_lam_q2, diff_lam_k2, diff_subln_g, gmlp_w_in, gmlp_ln_g, gmlp_ln_b, gmlp_w_s, gmlp_b_s, gmlp_w_out, moe_w_grp, moe_b_grp, moe_w_exp, moe_b_exp, moe_w1, moe_w3, moe_w2):
    bp, tp, d = x_prompt.shape
    bs, ts, _ = x_sample.shape
    n_p, n_s = bp * tp, bs * ts
    n_tok = n_p + n_s
    depth = norm_mix.shape[0]
    npages, psz = page_table.shape[1], cache_k.shape[2]
    past_len = npages * psz
    d_a = rwkv_w0.shape[1]
    d_ap = rwkv_mu.shape[1]
    d_q = cache_k.shape[3] * cache_k.shape[4]
    n_pool = cache_k.shape[1]
    nheads = d_a // RWKV_HEAD_DIM
    tm = 512 if n_p % 512 == 0 and n_s % 512 == 0 else 128
    tq = 512 if tp % 512 == 0 else 128
    tc = min(tp, CHUNK)

    h = jnp.concatenate([x_prompt.reshape(n_p, d), x_sample.reshape(n_s, d)], axis=0)
    pos = jnp.concatenate([jnp.tile(jnp.arange(tp), bp), jnp.tile(past_len + jnp.arange(ts), bs)])
    moe = lambda l, hh, fg=None: _hier_moe(hh, norm_ffn[l], moe_w_grp[l], moe_b_grp[l], moe_w_exp[l],
                                          moe_b_exp[l], moe_w1[l], moe_w3[l], moe_w2[l], fg)

    outs = {k: [] for k in ("k_p", "v_p", "wkv_p", "sh_p", "k_s", "v_s", "wkv_s", "sh_s", "gv_s")}
    y_norm = None
    for layer in range(depth):
        j = layer // 2
        xn = _rmsnorm(h, norm_mix[layer], NORM_EPS, BF16, tm)
        last = layer == depth - 1
        if layer % 2 == 0:
            proj = _matmul([xn], even_w_in[j], bm=tm, bn=even_w_in.shape[2] // 5, name="even_in_proj")
            prm = dict(mu=rwkv_mu[j].reshape(1, d_ap), w0=rwkv_w0[j].reshape(1, d_a), w2=rwkv_w2[j],
                       a0=rwkv_a0[j].reshape(1, d_a), a2=rwkv_a2[j], g2=rwkv_g2[j],
                       k_k=rwkv_k_k[j].reshape(1, d_a), k_a=rwkv_k_a[j].reshape(1, d_a),
                       r_k=rwkv_r_k[j].reshape(1, d_a))
            pa = proj[:, :d_ap]
            tmr = 256 if tp % 256 == 0 else 128
            pa_p = pa[:n_p].reshape(bp, tp, d_ap)
            bnd = jnp.concatenate([jnp.zeros((bp, 1, d_ap), F32), pa_p[:, tmr - 1:tp - 1:tmr]], axis=1)
            repl_p = jnp.zeros((n_p // tmr, 8, d_ap), F32).at[:, 0].set(bnd.reshape(n_p // tmr, d_ap))
            repl_p = repl_p.reshape(n_p // tmr * 8, d_ap)
            repl_s = jnp.repeat(state_shift[j], ts, axis=0)
            pre_p = _rwkv_pre(proj, repl_p, tmr, 0, n_p, tmr, prm)
            pre_s = _rwkv_pre(proj, repl_s, ts, n_p, n_s, tmr, prm)
            oa_p, wkv_p = _rwkv_scan(pre_p, rwkv_lnx_g[j], rwkv_lnx_b[j], None, bp, tp, tc,
                                     2 if bp % 2 == 0 else 1)
            oa_s, wkv_s = _rwkv_scan(pre_s, rwkv_lnx_g[j], rwkv_lnx_b[j], state_wkv[j], bs, ts, ts,
                                     2 if bs % 2 == 0 else 1)

            qs, qf, kf, kb, vb = _rope(proj, pos, d_ap, d_q, tm)
            vf = proj[:, d_ap + 2 * d_q:]
            lam_init = 0.8 - 0.6 * math.exp(-0.3 * layer)
            lam_p = jnp.stack([diff_lam_q1[j], diff_lam_k1[j], diff_lam_q2[j], diff_lam_k2[j]])
            ob_p = _attn_prompt(qs[:n_p], kb[:n_p], vb[:n_p], lam_p, diff_subln_g[j], bp, tp, tq, lam_init)
            ob_s = _attn_sample(qf[n_p:], kf[n_p:], vf[n_p:], cache_k, cache_v, j,
                                page_table.reshape(-1).astype(I32),
                                lam_p, diff_subln_g[j], bs, ts, npages, lam_init)
            oa = jnp.concatenate([oa_p, oa_s.astype(BF16)], axis=0)
            ob = jnp.concatenate([ob_p, ob_s.astype(BF16)], axis=0)
            h = _matmul([oa, ob], even_w_out[j], bm=tm, bn=1024, res=h, name="even_out_proj")

            outs["k_p"].append(kf[:n_p].reshape(bp, tp, -1, DIFF_QK_DIM))
            outs["v_p"].append(vf[:n_p].reshape(bp, tp, -1, DIFF_V_DIM))
            outs["wkv_p"].append(wkv_p)
            outs["sh_p"].append(pa_p[:, -1])
            outs["k_s"].append(kf[n_p:].reshape(bs, ts, -1, DIFF_QK_DIM))
            outs["v_s"].append(vf[n_p:].reshape(bs, ts, -1, DIFF_V_DIM))
            outs["wkv_s"].append(wkv_s)
            outs["sh_s"].append(pa[n_p:].reshape(bs, ts, d_ap)[:, -1])
        else:
            d_c = gmlp_ln_g.shape[1]
            z = _matmul([xn], gmlp_w_in[j], bm=tm, bn=1024, act="gelu", name="gmlp_in_proj")
            ws = gmlp_w_s[j]
            bsb = gmlp_b_s[j]
            gd = d_c // GMLP_GROUPS
            lp = min(tp, CHUNK)
            wm_p = jnp.tril(ws[:, :lp, :lp])
            bm_p = jnp.broadcast_to(bsb[:, :lp, None], (GMLP_GROUPS, lp, gd))
            rep = CHUNK // ts
            eye = jnp.eye(rep, dtype=F32)
            wm_s = jnp.einsum("ab,gts->gatbs", eye, jnp.tril(ws[:, :ts, :ts])).reshape(GMLP_GROUPS, CHUNK, CHUNK)
            bm_s = jnp.broadcast_to(jnp.tile(bsb[:, :ts], (1, rep))[:, :, None], (GMLP_GROUPS, CHUNK, gd))
            (y_p,) = _gmlp(z, 0, n_p, gmlp_ln_g[j], gmlp_ln_b[j], wm_p, bm_p, False)
            y_s, v_rows = _gmlp(z, n_p, n_s, gmlp_ln_g[j], gmlp_ln_b[j], wm_s, bm_s, True)
            yin = jnp.concatenate([y_p, y_s], axis=0)
            h = _matmul([yin], gmlp_w_out[j], bm=tm, bn=1024, res=h, name="gmlp_out_proj")
            outs["gv_s"].append(v_rows.reshape(bs, ts, d_c))
        res = moe(layer, h, norm_final if last else None)
        h = res[0]
        if last:
            y_norm = res[1]

    y_prompt = y_norm[:n_p].reshape(bp, tp, d)
    y_sample = y_norm[n_p:].reshape(bs, ts, d)
    st = lambda k: jnp.stack(outs[k])
    return (y_prompt, y_sample, st("k_p"), st("v_p"), st("wkv_p"), st("sh_p"),
            st("k_s"), st("v_s"), st("wkv_s"), st("sh_s"), st("gv_s"))
```

```python
import functools
import math

import jax
import jax.numpy as jnp
from jax import lax
from jax.experimental import pallas as pl
from jax.experimental.pallas import tpu as pltpu

F32 = jnp.float32
BF16 = jnp.bfloat16
I32 = jnp.int32

RWKV_HEAD_DIM = 64
RWKV_GN_EPS = 64e-5
DIFF_QK_DIM = 64
DIFF_V_DIM = 128
ROT_DIM = 16
ROPE_THETA = 500000.0
ATTN_SCALE = DIFF_QK_DIM ** -0.5
NEG_INF = -1e30
NORM_EPS = 1e-6
SUBLN_EPS = 1e-5
GMLP_LN_EPS = 1e-5
GMLP_GROUPS = 8
CHUNK = 128
N_EXP_GROUPS = 4
EXP_PER_GROUP = 8
LORA_W, LORA_A, LORA_G = 64, 64, 128

LANES = 128
MXU_TILE = 256
VMEM_LIMIT = 56 * 1024 * 1024
MOE_TILE = 256
HIGHEST = lax.Precision.HIGHEST


def _cparams(sem):
    return pltpu.CompilerParams(dimension_semantics=sem, vmem_limit_bytes=VMEM_LIMIT)


def _rmsnorm_body(x_ref, g_ref, o_ref, *, eps):
    x = x_ref[...]
    inv = lax.rsqrt(jnp.mean(x * x, axis=-1, keepdims=True) + eps)
    o_ref[...] = ((x * inv) * g_ref[...]).astype(o_ref.dtype)


def _rmsnorm(x, g, eps, out_dtype, tm):
    m, d = x.shape
    return pl.pallas_call(
        functools.partial(_rmsnorm_body, eps=eps),
        grid=(m // tm,),
        in_specs=[pl.BlockSpec((tm, d), lambda i: (i, 0)),
                  pl.BlockSpec((1, d), lambda i: (0, 0))],
        out_specs=pl.BlockSpec((tm, d), lambda i: (i, 0)),
        out_shape=jax.ShapeDtypeStruct((m, d), out_dtype),
        compiler_params=_cparams(("arbitrary",)),
        name="rmsnorm",
    )(x, g.reshape(1, d))


def _gelu_exact(x):
    return 0.5 * x * (1.0 + lax.erf(x * (1.0 / math.sqrt(2.0))))


def _mm_body(*refs, n_x, act, has_res):
    xs = refs[:n_x]
    w_ref = refs[n_x]
    res_ref = refs[n_x + 1] if has_res else None
    o_ref = refs[n_x + 1 + has_res]
    wb_ref = refs[n_x + 2 + has_res]

    @pl.when(pl.program_id(1) == 0)
    def _():
        wb_ref[...] = w_ref[...].astype(BF16)

    kx = w_ref.shape[0] // n_x
    acc = None
    for i, x_ref in enumerate(xs):
        part = jnp.dot(x_ref[...], wb_ref[i * kx:(i + 1) * kx, :], preferred_element_type=F32)
        acc = part if acc is None else acc + part
    if act == "gelu":
        acc = _gelu_exact(acc)
    if has_res:
        acc = acc + res_ref[...]
    o_ref[...] = acc.astype(o_ref.dtype)


def _matmul(xs, w, *, bm, bn, act=None, res=None, out_dtype=F32, name="matmul"):
    m = xs[0].shape[0]
    k, n = w.shape
    kx = k // len(xs)
    in_specs = [pl.BlockSpec((bm, kx), lambda j, i: (i, 0)) for _ in xs]
    in_specs.append(pl.BlockSpec((k, bn), lambda j, i: (0, j)))
    args = list(xs) + [w]
    if res is not None:
        in_specs.append(pl.BlockSpec((bm, bn), lambda j, i: (i, j)))
        args.append(res)
    return pl.pallas_call(
        functools.partial(_mm_body, n_x=len(xs), act=act, has_res=res is not None),
        grid=(n // bn, m // bm),
        in_specs=in_specs,
        out_specs=pl.BlockSpec((bm, bn), lambda j, i: (i, j)),
        out_shape=jax.ShapeDtypeStruct((m, n), out_dtype),
        scratch_shapes=[pltpu.VMEM((k, bn), BF16)],
        compiler_params=_cparams(("arbitrary", "arbitrary")),
        name=name,
    )(*args)


def _block_ones(n, seg):
    r = lax.broadcasted_iota(I32, (n, n), 0) // seg
    c = lax.broadcasted_iota(I32, (n, n), 1) // seg
    return jnp.where(r == c, 1.0, 0.0).astype(BF16)


def _seg_sum(x, ones_bd):
    hi = x.astype(BF16)
    lo = (x - hi.astype(F32)).astype(BF16)
    outs = []
    for c in range(x.shape[1] // MXU_TILE):
        sl = slice(c * MXU_TILE, (c + 1) * MXU_TILE)
        outs.append(jnp.dot(hi[:, sl], ones_bd, preferred_element_type=F32)
                    + jnp.dot(lo[:, sl], ones_bd, preferred_element_type=F32))
    return jnp.concatenate(outs, axis=1)


def _sigmoid(x):
    return 1.0 / (1.0 + jnp.exp(-x))


def _rwkv_pre_body(pa_ref, repl_ref, mu_ref, w0_ref, w2_ref, a0_ref, a2_ref, g2_ref,
                   kk_ref, ka_ref, rk_ref,
                   r_o, dec_o, k_o, v_o, kkn_o, b_o, g_o, bonus_o, *, period, d_a):
    pa = pa_ref[...]
    tm = pa.shape[0]
    row = lax.broadcasted_iota(I32, pa.shape, 0)
    prev = pltpu.roll(pa, 1, 0)
    repl = repl_ref[...]
    if repl.shape[0] != tm:
        repl = jnp.broadcast_to(repl[0:1, :], pa.shape)
    prev = jnp.where(row % period == 0, repl, prev)
    xm = pa + (prev - pa) * mu_ref[...]
    r = xm[:, 0:d_a]
    k = xm[:, d_a:2 * d_a]
    v = xm[:, 2 * d_a:3 * d_a]
    o = 3 * d_a
    wd = xm[:, o:o + LORA_W]
    ad = xm[:, o + LORA_W:o + LORA_W + LORA_A]
    gd = xm[:, o + LORA_W + LORA_A:o + LORA_W + LORA_A + LORA_G]

    z = w0_ref[...] + jnp.dot(jnp.tanh(wd), w2_ref[...], precision=HIGHEST, preferred_element_type=F32)
    w = jnp.minimum(z, 0.0) - jnp.log1p(jnp.exp(-jnp.abs(z))) - 0.5
    dec = jnp.exp(-jnp.exp(w))
    a = _sigmoid(a0_ref[...] + jnp.dot(ad, a2_ref[...], precision=HIGHEST, preferred_element_type=F32))
    g = jnp.dot(_sigmoid(gd).astype(BF16), g2_ref[...].astype(BF16), preferred_element_type=F32)

    ones_bd = _block_ones(MXU_TILE, RWKV_HEAD_DIM)
    kk = k * kk_ref[...]
    nrm = jnp.sqrt(_seg_sum(kk * kk, ones_bd))
    kkn = kk / jnp.maximum(nrm, 1e-12)
    kh = k * (1.0 + (a - 1.0) * ka_ref[...])
    bonus = _seg_sum(r * kh * rk_ref[...], ones_bd) * v

    r_o[...] = r
    dec_o[...] = dec
    k_o[...] = kh
    v_o[...] = v
    kkn_o[...] = kkn
    b_o[...] = kkn * a
    g_o[...] = g
    bonus_o[...] = bonus


def _rwkv_pre(proj, repl, period, row0, rows, tm, prm):
    d_a = prm["w0"].shape[1]
    d_ap = prm["mu"].shape[1]
    nt = rows // tm
    t0 = row0 // tm
    vec = lambda n: pl.BlockSpec((1, n), lambda i: (0, 0))
    full = lambda a: pl.BlockSpec(a.shape, lambda i: (0, 0))
    out_spec = pl.BlockSpec((tm, d_a), lambda i: (i, 0))
    outs = pl.pallas_call(
        functools.partial(_rwkv_pre_body, period=period, d_a=d_a),
        grid=(nt,),
        in_specs=[pl.BlockSpec((tm, d_ap), lambda i: (t0 + i, 0)),
                  pl.BlockSpec((repl.shape[0] // nt, d_ap), lambda i: (i, 0)),
                  vec(d_ap), vec(d_a), full(prm["w2"]), vec(d_a), full(prm["a2"]), full(prm["g2"]),
                  vec(d_a), vec(d_a), vec(d_a)],
        out_specs=[out_spec] * 8,
        out_shape=[jax.ShapeDtypeStruct((rows, d_a), F32)] * 8,
        compiler_params=_cparams(("arbitrary",)),
        name="rwkv_pre",
    )(proj, repl, prm["mu"], prm["w0"], prm["w2"], prm["a0"], prm["a2"], prm["g2"],
      prm["k_k"], prm["k_a"], prm["r_k"])
    return outs


def _rwkv_scan_body(*refs, tc, tp, nheads, nb, has_state):
    if has_state:
        s0_ref = refs[0]
        refs = refs[1:]
    (r_ref, w_ref, k_ref, v_ref, kk_ref, b_ref, g_ref, bonus_ref, lg_ref, lb_ref,
     o_ref, sout_ref, s_scr, vth_scr, vtl_scr, ot_scr) = refs
    n = RWKV_HEAD_DIM
    c = pl.program_id(1)
    assert tp == 2 * n
    nh = nb * nheads
    heads = [(b, h) for b in range(nb) for h in range(nheads)]

    @pl.when(c == 0)
    def _():
        if has_state:
            zero = jnp.zeros((n, n), F32)
            for i, (b, h) in enumerate(heads):
                s0 = s0_ref[b, h]
                s_scr[i] = jnp.concatenate([s0, zero] if h % 2 == 0 else [zero, s0], axis=1)
        else:
            s_scr[...] = jnp.zeros_like(s_scr)

    for i, (b, h) in enumerate(heads):
        vh = v_ref[b, :, h * n:(h + 1) * n]
        if tp != tc:
            vh = jnp.concatenate([vh, jnp.zeros((tp - tc, n), F32)], axis=0)
        vt = vh.T
        hi = vt.astype(BF16)
        vth_scr[i * n:(i + 1) * n, :] = hi
        vtl_scr[i * n:(i + 1) * n, :] = (vt - hi.astype(F32)).astype(BF16)
    ot_scr[...] = jnp.zeros_like(ot_scr)

    lane = lax.broadcasted_iota(I32, (1, n, tp), 2)
    trow = lax.broadcasted_iota(I32, (tp, 2 * n), 0)
    ones_w = jnp.ones((2 * n, 2 * n), BF16)
    head_par = lax.broadcasted_iota(I32, (nh, 1, 2 * n), 0) % 2
    lane_par = lax.broadcasted_iota(I32, (nh, 1, 2 * n), 2) // n
    own = head_par == lane_par

    sub = 8

    def per_head(ref, base, masked):
        xs = [ref[b, pl.ds(base, sub), :] for b in range(nb)]
        x = jnp.stack([xs[b][:, (h // 2) * 2 * n:(h // 2 + 1) * 2 * n] for b, h in heads], axis=0)
        return jnp.where(own, x, 0.0) if masked else x

    def group(tg, carry):
        base = pl.multiple_of(tg * sub, sub)
        kk8 = per_head(kk_ref, base, True)
        b8 = per_head(b_ref, base, True)
        k8 = per_head(k_ref, base, True)
        w8 = per_head(w_ref, base, False)
        r8 = per_head(r_ref, base, False)
        st = s_scr[...]
        vth = vth_scr[...]
        vtl = vtl_scr[...]
        ot = ot_scr[...]
        for s in range(sub):
            t = base + s
            pick = jnp.where(trow == t, 1.0, 0.0).astype(BF16)
            v_col = (jnp.dot(vth, pick, preferred_element_type=F32)
                     + jnp.dot(vtl, pick, preferred_element_type=F32)).reshape(nh, n, 2 * n)
            sa = -jnp.sum(st * kk8[:, s:s + 1], axis=-1, keepdims=True)
            st = st * w8[:, s:s + 1] + sa * b8[:, s:s + 1] + v_col * k8[:, s:s + 1]
            q = (st * r8[:, s:s + 1]).reshape(nh * n, 2 * n).astype(BF16)
            o_col = jnp.dot(q, ones_w, preferred_element_type=F32).reshape(nh, n, 2 * n)
            ot = jnp.where(lane == t, o_col, ot)
        s_scr[...] = st
        ot_scr[...] = ot
        return carry

    lax.fori_loop(0, tc // sub, group, 0)

    for b in range(nb):
        for hp in range(nheads // 2):
            parts = []
            for h in (2 * hp, 2 * hp + 1):
                oc = ot_scr[b * nheads + h]
                mean = jnp.mean(oc, axis=0, keepdims=True)
                d = oc - mean
                var = jnp.mean(d * d, axis=0, keepdims=True)
                on = d * lax.rsqrt(var + RWKV_GN_EPS)
                parts.append(on.T[:tc])
            sl = slice(hp * 2 * n, (hp + 1) * 2 * n)
            on2 = jnp.concatenate(parts, axis=1)
            o = (on2 * lg_ref[:, sl] + lb_ref[:, sl] + bonus_ref[b, :, sl]) * g_ref[b, :, sl]
            o_ref[b, :, sl] = o.astype(o_ref.dtype)

    @pl.when(c == pl.num_programs(1) - 1)
    def _():
        for i, (b, h) in enumerate(heads):
            s_h = s_scr[i]
            sout_ref[b, h] = s_h[:, 0:n] if h % 2 == 0 else s_h[:, n:2 * n]


def _rwkv_scan(pre, lnx_g, lnx_b, state, nseq, tseq, tc, nb):
    rows, d_a = pre[0].shape
    nheads = d_a // RWKV_HEAD_DIM
    nch = tseq // tc
    tp = max(tc, LANES)
    n = RWKV_HEAD_DIM
    nh = nb * nheads
    tok = pl.BlockSpec((nb, tc, d_a), lambda s, c: (s, c, 0))
    vec = pl.BlockSpec((1, d_a), lambda s, c: (0, 0))
    st = pl.BlockSpec((nb, nheads, n, n), lambda s, c: (s, 0, 0, 0))
    has_state = state is not None
    in_specs = ([st] if has_state else []) + [tok] * 8 + [vec, vec]
    args = ([state] if has_state else []) + [a.reshape(nseq, tseq, d_a) for a in pre] + [
        lnx_g.reshape(1, d_a), lnx_b.reshape(1, d_a)]
    oa, s_fin = pl.pallas_call(
        functools.partial(_rwkv_scan_body, tc=tc, tp=tp, nheads=nheads, nb=nb, has_state=has_state),
        grid=(nseq // nb, nch),
        in_specs=in_specs,
        out_specs=[tok, st],
        out_shape=[jax.ShapeDtypeStruct((nseq, tseq, d_a), BF16 if tc % 16 == 0 else F32),
                   jax.ShapeDtypeStruct((nseq, nheads, n, n), F32)],
        scratch_shapes=[pltpu.VMEM((nh, n, 2 * n), F32),
                        pltpu.VMEM((nh * n, tp), BF16),
                        pltpu.VMEM((nh * n, tp), BF16),
                        pltpu.VMEM((nh, n, tp), F32)],
        compiler_params=_cparams(("arbitrary", "arbitrary")),
        name="rwkv_scan",
    )(*args)
    return oa.reshape(rows, d_a), s_fin


def _rope_body(q_ref, k_ref, v_ref, c_ref, s1_ref, s2_ref, qs_o, qf_o, kf_o, kb_o, vb_o):
    c, s1, s2 = c_ref[...], s1_ref[...], s2_ref[...]
    half = ROT_DIM // 2
    w = q_ref.shape[1]

    def rot(x):
        return x * c + pltpu.roll(x, w - half, 1) * s1 + pltpu.roll(x, half, 1) * s2

    qr = rot(q_ref[...]) * ATTN_SCALE
    qf_o[...] = qr
    qs_o[...] = qr.astype(BF16)
    kr = rot(k_ref[...])
    kf_o[...] = kr
    kb_o[...] = kr.astype(BF16)
    vb_o[...] = v_ref[...].astype(BF16)


def _rope(proj, pos, d_ap, d_q, tm):
    n_tok = proj.shape[0]
    w = MXU_TILE
    half = ROT_DIM // 2
    inv_freq = ROPE_THETA ** (-jnp.arange(half, dtype=F32) / half)
    ang = pos.astype(F32)[:, None] * inv_freq[None, :]
    cos, sin = jnp.cos(ang), jnp.sin(ang)
    pad = DIFF_QK_DIM - ROT_DIM
    ones = jnp.ones((n_tok, pad), F32)
    zeros = jnp.zeros((n_tok, pad), F32)
    zh = jnp.zeros((n_tok, half), F32)
    reps = w // DIFF_QK_DIM
    c_t = jnp.tile(jnp.concatenate([cos, cos, ones], axis=1), (1, reps))
    s1_t = jnp.tile(jnp.concatenate([-sin, zh, zeros], axis=1), (1, reps))
    s2_t = jnp.tile(jnp.concatenate([zh, sin, zeros], axis=1), (1, reps))
    nq = d_q // w
    qb, kb, vb = d_ap // w, d_ap // w + nq, d_ap // w + 2 * nq
    tab = pl.BlockSpec((tm, w), lambda i, j: (i, 0))
    out = pl.BlockSpec((tm, w), lambda i, j: (i, j))
    return pl.pallas_call(
        _rope_body,
        grid=(n_tok // tm, nq),
        in_specs=[pl.BlockSpec((tm, w), lambda i, j: (i, qb + j)),
                  pl.BlockSpec((tm, w), lambda i, j: (i, kb + j)),
                  pl.BlockSpec((tm, w), lambda i, j: (i, vb + j)),
                  tab, tab, tab],
        out_specs=[out, out, out, out, out],
        out_shape=[jax.ShapeDtypeStruct((n_tok, d_q), BF16),
                   jax.ShapeDtypeStruct((n_tok, d_q), F32),
                   jax.ShapeDtypeStruct((n_tok, d_q), F32),
                   jax.ShapeDtypeStruct((n_tok, d_q), BF16),
                   jax.ShapeDtypeStruct((n_tok, d_q), BF16)],
        compiler_params=_cparams(("arbitrary", "arbitrary")),
        name="rope",
    )(proj, proj, proj, c_t, s1_t, s2_t)


def _lambda(lam_ref, lam_init):
    lp = lam_ref[...]
    l1 = jnp.sum(lp[0:1] * lp[1:2], axis=-1, keepdims=True)
    l2 = jnp.sum(lp[2:3] * lp[3:4], axis=-1, keepdims=True)
    return jnp.exp(l1) - jnp.exp(l2) + lam_init


def _subln(o, g_ref, lam_init):
    inv = lax.rsqrt(jnp.mean(o * o, axis=-1, keepdims=True) + SUBLN_EPS)
    return (o * inv) * g_ref[...] * (1.0 - lam_init)


def _online_softmax_update(s, v, m_scr, l_scr, acc_scr):
    m_prev = m_scr[...]
    m_new = jnp.maximum(m_prev, jnp.max(s, axis=-1, keepdims=True))
    alpha = jnp.exp(m_prev - m_new)
    p = jnp.exp(s - m_new[:, 0:1])
    l_scr[...] = alpha * l_scr[...] + jnp.sum(p, axis=-1, keepdims=True)
    acc_scr[...] = alpha[:, 0:1] * acc_scr[...] + jnp.dot(p.astype(BF16), v, preferred_element_type=F32)
    m_scr[...] = m_new


def _attn_prompt_body(q_ref, k_ref, v_ref, lam_ref, g_ref, o_ref, qq_scr, m_scr, l_scr, acc_scr,
                      *, tq, lam_init):
    qi = pl.program_id(2)
    ki = pl.program_id(3)

    @pl.when(ki == 0)
    def _():
        q = q_ref[...]
        lane = lax.broadcasted_iota(I32, q.shape, 1)
        zero = jnp.zeros_like(q)
        qq_scr[0:tq] = jnp.where(lane < DIFF_QK_DIM, q, zero)
        qq_scr[tq:2 * tq] = jnp.where(lane >= DIFF_QK_DIM, q, zero)
        m_scr[...] = jnp.full_like(m_scr, NEG_INF)
        l_scr[...] = jnp.zeros_like(l_scr)
        acc_scr[...] = jnp.zeros_like(acc_scr)

    def scores():
        return lax.dot_general(qq_scr[...], k_ref[...], (((1,), (1,)), ((), ())),
                               preferred_element_type=F32)

    @pl.when(ki < qi)
    def _():
        _online_softmax_update(scores(), v_ref[...], m_scr, l_scr, acc_scr)

    @pl.when(ki == qi)
    def _():
        s = scores()
        row = lax.broadcasted_iota(I32, s.shape, 0) % tq
        col = lax.broadcasted_iota(I32, s.shape, 1)
        s = jnp.where(row >= col, s, NEG_INF)
        _online_softmax_update(s, v_ref[...], m_scr, l_scr, acc_scr)
        lam = _lambda(lam_ref, lam_init)
        o1 = acc_scr[0:tq] / l_scr[0:tq, 0:1]
        o2 = acc_scr[tq:2 * tq] / l_scr[tq:2 * tq, 0:1]
        o_ref[...] = _subln(o1 - lam * o2, g_ref, lam_init).astype(o_ref.dtype)


def _attn_prompt(qs, kb, vb, lam_p, subln_g, nseq, tseq, tq, lam_init):
    rows, d_q = qs.shape
    nh = d_q // DIFF_V_DIM
    nq = tseq // tq
    w = DIFF_V_DIM
    return pl.pallas_call(
        functools.partial(_attn_prompt_body, tq=tq, lam_init=lam_init),
        grid=(nseq, nh, nq, nq),
        in_specs=[pl.BlockSpec((tq, w), lambda b, h, i, j: (b * nq + i, h)),
                  pl.BlockSpec((tq, w), lambda b, h, i, j: (b * nq + jnp.minimum(i, j), h)),
                  pl.BlockSpec((tq, w), lambda b, h, i, j: (b * nq + jnp.minimum(i, j), h)),
                  pl.BlockSpec(lam_p.shape, lambda b, h, i, j: (0, 0)),
                  pl.BlockSpec((1, w), lambda b, h, i, j: (0, 0))],
        out_specs=pl.BlockSpec((tq, w), lambda b, h, i, j: (b * nq + i, h)),
        out_shape=jax.ShapeDtypeStruct((nseq * tseq, d_q), BF16),
        scratch_shapes=[pltpu.VMEM((2 * tq, w), BF16),
                        pltpu.VMEM((2 * tq, LANES), F32),
                        pltpu.VMEM((2 * tq, LANES), F32),
                        pltpu.VMEM((2 * tq, w), F32)],
        compiler_params=_cparams(("arbitrary", "arbitrary", "arbitrary", "arbitrary")),
        name="attn_prompt",
    )(qs, kb, vb, lam_p, subln_g.reshape(1, w))


def _attn_sample_body(pt_ref, q_ref, *refs, ts, lam_init, pages_per_step):
    kc_refs = refs[:pages_per_step]
    vc_refs = refs[pages_per_step:2 * pages_per_step]
    kn_ref, vn_ref, lam_ref, g_ref, o_ref, qbd_scr, m_scr, l_scr, acc_scr = refs[2 * pages_per_step:]
    p = pl.program_id(1)
    nheads_qk = q_ref.shape[1] // DIFF_QK_DIM

    @pl.when(p == 0)
    def _():
        q = q_ref[...]
        head = lax.broadcasted_iota(I32, q.shape, 1) // DIFF_QK_DIM
        zero = jnp.zeros_like(q)
        rows = [jnp.where(head == m, q, zero) for m in range(nheads_qk)]
        qbd_scr[...] = jnp.concatenate(rows, axis=0).astype(BF16)
        m_scr[...] = jnp.full_like(m_scr, NEG_INF)
        l_scr[...] = jnp.zeros_like(l_scr)
        acc_scr[...] = jnp.zeros_like(acc_scr)

    nh_v = nheads_qk // 2
    w = DIFF_V_DIM

    def update(s, v_heads):
        m_prev = m_scr[...]
        m_new = jnp.maximum(m_prev, jnp.max(s, axis=-1, keepdims=True))
        alpha = jnp.exp(m_prev - m_new)
        pr = jnp.exp(s - m_new[:, 0:1])
        l_scr[...] = alpha * l_scr[...] + jnp.sum(pr, axis=-1, keepdims=True)
        m_scr[...] = m_new
        pb = pr.astype(BF16)
        for h in range(nh_v):
            rows = slice(2 * h * ts, (2 * h + 2) * ts)
            acc_scr[rows] = alpha[rows] * acc_scr[rows] + jnp.dot(pb[rows], v_heads[h],
                                                                  preferred_element_type=F32)

    kt = jnp.concatenate([r[...].reshape(nheads_qk * DIFF_QK_DIM, r.shape[2]) for r in kc_refs],
                         axis=1).astype(BF16)
    s_past = jnp.dot(qbd_scr[...], kt, preferred_element_type=F32)
    npos = vc_refs[0].shape[0] // nh_v
    v_past = [jnp.concatenate([r[pl.ds(h, npos, stride=nh_v), :] for r in vc_refs], axis=0).astype(BF16)
              for h in range(nh_v)]
    update(s_past, v_past)

    @pl.when(p == pl.num_programs(1) - 1)
    def _():
        padk = jnp.zeros((LANES - ts, kn_ref.shape[1]), F32)
        kn = jnp.concatenate([kn_ref[...], padk], axis=0).astype(BF16)
        vn = jnp.concatenate([vn_ref[...], padk], axis=0).astype(BF16)
        s = lax.dot_general(qbd_scr[...], kn, (((1,), (1,)), ((), ())), preferred_element_type=F32)
        row = lax.broadcasted_iota(I32, s.shape, 0) % ts
        col = lax.broadcasted_iota(I32, s.shape, 1)
        s = jnp.where(row >= col, s, NEG_INF)
        update(s, [vn[:, h * w:(h + 1) * w] for h in range(nh_v)])
        lam = _lambda(lam_ref, lam_init)
        for h in range(nh_v):
            r1 = slice(2 * h * ts, (2 * h + 1) * ts)
            r2 = slice((2 * h + 1) * ts, (2 * h + 2) * ts)
            o1 = acc_scr[r1] / l_scr[r1, 0:1]
            o2 = acc_scr[r2] / l_scr[r2, 0:1]
            o_ref[:, h * w:(h + 1) * w] = _subln(o1 - lam * o2, g_ref, lam_init).astype(o_ref.dtype)


def _attn_sample(qs, kb, vb, cache_k, cache_v, layer_j, page_flat, lam_p, subln_g, nb, ts, npages, lam_init):
    d_q = qs.shape[1]
    nrow = (d_q // DIFF_QK_DIM) * ts
    gpp = 4 if npages % 4 == 0 else (2 if npages % 2 == 0 else 1)
    tok = pl.BlockSpec((ts, d_q), lambda b, p, pt: (b, 0))
    cache_k = jnp.transpose(cache_k, (0, 1, 3, 4, 2))
    cache_v = cache_v.reshape(cache_v.shape[:2] + (-1, cache_v.shape[-1]))

    def page(cache, g):
        return pl.BlockSpec((None, None) + cache.shape[2:],
                            lambda b, p, pt: (layer_j, pt[b * npages + p * gpp + g]) + (0,) * (cache.ndim - 2))

    return pl.pallas_call(
        functools.partial(_attn_sample_body, ts=ts, lam_init=lam_init, pages_per_step=gpp),
        grid_spec=pltpu.PrefetchScalarGridSpec(
            num_scalar_prefetch=1,
            grid=(nb, npages // gpp),
            in_specs=[tok] + [page(cache_k, g) for g in range(gpp)] + [page(cache_v, g) for g in range(gpp)]
                     + [tok, tok,
                        pl.BlockSpec(lam_p.shape, lambda b, p, pt: (0, 0)),
                        pl.BlockSpec((1, DIFF_V_DIM), lambda b, p, pt: (0, 0))],
            out_specs=tok,
            scratch_shapes=[pltpu.VMEM((nrow, d_q), BF16),
                            pltpu.VMEM((nrow, LANES), F32),
                            pltpu.VMEM((nrow, LANES), F32),
                            pltpu.VMEM((nrow, DIFF_V_DIM), F32)]),
        out_shape=jax.ShapeDtypeStruct((nb * ts, d_q), F32),
        compiler_params=_cparams(("arbitrary", "arbitrary")),
        name="attn_sample",
    )(page_flat, qs, *([cache_k] * gpp), *([cache_v] * gpp), kb, vb, lam_p, subln_g.reshape(1, DIFF_V_DIM))


def _gmlp_body(z_ref, lg_ref, lb_ref, wm_ref, bm_ref, y_ref, *maybe_v, d_c):
    u = z_ref[:, 0:d_c]
    v = z_ref[:, d_c:2 * d_c]
    mean = jnp.mean(v, axis=-1, keepdims=True)
    d = v - mean
    var = jnp.mean(d * d, axis=-1, keepdims=True)
    vn = (d * lax.rsqrt(var + GMLP_LN_EPS)) * lg_ref[...] + lb_ref[...]
    if maybe_v:
        maybe_v[0][...] = vn
    gd = d_c // GMLP_GROUPS
    vb = vn.astype(BF16)
    for g in range(GMLP_GROUPS):
        sl = slice(g * gd, (g + 1) * gd)
        mixed = jnp.dot(wm_ref[g].astype(BF16), vb[:, sl], preferred_element_type=F32) + bm_ref[g]
        y_ref[:, sl] = (u[:, sl] * mixed).astype(y_ref.dtype)


def _gmlp(z, row0, rows, ln_g, ln_b, wmix, bmix, emit_v):
    d_c = z.shape[1] // 2
    tm = CHUNK
    t0 = row0 // tm
    gd = d_c // GMLP_GROUPS
    out_specs = [pl.BlockSpec((tm, d_c), lambda i: (i, 0))]
    out_shape = [jax.ShapeDtypeStruct((rows, d_c), BF16)]
    if emit_v:
        out_specs.append(pl.BlockSpec((tm, d_c), lambda i: (i, 0)))
        out_shape.append(jax.ShapeDtypeStruct((rows, d_c), F32))
    return pl.pallas_call(
        functools.partial(_gmlp_body, d_c=d_c),
        grid=(rows // tm,),
        in_specs=[pl.BlockSpec((tm, 2 * d_c), lambda i: (t0 + i, 0)),
                  pl.BlockSpec((1, d_c), lambda i: (0, 0)),
                  pl.BlockSpec((1, d_c), lambda i: (0, 0)),
                  pl.BlockSpec((GMLP_GROUPS, tm, tm), lambda i: (0, 0, 0)),
                  pl.BlockSpec((GMLP_GROUPS, tm, gd), lambda i: (0, 0, 0))],
        out_specs=out_specs,
        out_shape=out_shape,
        compiler_params=_cparams(("arbitrary",)),
        name="gmlp_gate",
    )(z, ln_g.reshape(1, d_c), ln_b.reshape(1, d_c), wmix, bmix)


def _store_token_rows(ref2, x2d):
    dc = x2d.shape[1] // LANES
    for c in range(dc):
        ref2[pl.ds(c, x2d.shape[0], stride=dc), :] = x2d[:, c * LANES:(c + 1) * LANES]


def _load_token_rows(ref2, dc):
    n = ref2.shape[0] // dc
    return jnp.concatenate([ref2[pl.ds(c, n, stride=dc), :] for c in range(dc)], axis=1)


def _router_body(h_ref, g_ref, wr_ref, br_ref, xn_o, idx_o, cw_o):
    x = h_ref[...]
    inv = lax.rsqrt(jnp.mean(x * x, axis=-1, keepdims=True) + NORM_EPS)
    xn = (x * inv) * g_ref[...]
    _store_token_rows(xn_o, xn)
    logits = jnp.dot(xn, wr_ref[...], precision=HIGHEST, preferred_element_type=F32) + br_ref[...]
    lane = lax.broadcasted_iota(I32, logits.shape, 1)
    big = jnp.int32(1 << 20)
    ninf = jnp.float32(-jnp.inf)

    def first_max(vals):
        mx = jnp.max(vals, axis=-1, keepdims=True)
        ix = jnp.min(jnp.where(vals == mx, lane, big), axis=-1, keepdims=True)
        return mx, ix

    gl = jnp.where(lane < N_EXP_GROUPS, logits, ninf)
    gmax, gtop = first_max(gl)
    gate = 1.0 / jnp.sum(jnp.exp(gl - gmax), axis=-1, keepdims=True)
    lo = N_EXP_GROUPS + EXP_PER_GROUP * gtop
    el = jnp.where((lane >= lo) & (lane < lo + EXP_PER_GROUP), logits, ninf)
    m1, i1 = first_max(el)
    m2, i2 = first_max(jnp.where(lane == i1, ninf, el))
    p2 = jnp.exp(m2 - m1)
    c1 = gate / (1.0 + p2)
    c2 = gate * p2 / (1.0 + p2)
    idx_o[...] = jnp.where(lane == 0, i1 - N_EXP_GROUPS, jnp.where(lane == 1, i2 - N_EXP_GROUPS, 0))
    cw_o[...] = jnp.where(lane == 0, c1, jnp.where(lane == 1, c2, 0.0))


def _router(h, g, w_grp, b_grp, w_exp, b_exp, tm):
    n_tok, d = h.shape
    ncol = w_grp.shape[1] + w_exp.shape[1]
    wr = jnp.concatenate([w_grp, w_exp, jnp.zeros((d, LANES - ncol), F32)], axis=1)
    br = jnp.concatenate([b_grp, b_exp, jnp.zeros((LANES - ncol,), F32)]).reshape(1, LANES)
    tok = lambda w: pl.BlockSpec((tm, w), lambda i: (i, 0))
    return pl.pallas_call(
        _router_body,
        grid=(n_tok // tm,),
        in_specs=[tok(d), pl.BlockSpec((1, d), lambda i: (0, 0)),
                  pl.BlockSpec((d, LANES), lambda i: (0, 0)),
                  pl.BlockSpec((1, LANES), lambda i: (0, 0))],
        out_specs=[pl.BlockSpec((tm * (d // LANES), LANES), lambda i: (i, 0)), tok(LANES), tok(LANES)],
        out_shape=[jax.ShapeDtypeStruct((n_tok * (d // LANES), LANES), F32),
                   jax.ShapeDtypeStruct((n_tok, LANES), I32),
                   jax.ShapeDtypeStruct((n_tok, LANES), F32)],
        compiler_params=_cparams(("arbitrary",)),
        name="moe_router",
    )(h, g.reshape(1, d), wr, br)


def _gather_copy(x_hbm, xbuf, sem, slot, i, tok, dc):
    src = x_hbm.at[pl.ds(pl.multiple_of(tok * dc, dc), dc)]
    dst = xbuf.at[slot, pl.ds(pl.multiple_of(i * dc, dc), dc)]
    return pltpu.make_async_copy(src, dst, sem.at[slot])


def _moe_ffn_body(te_ref, nvalid_ref, tok_ref, x_hbm, rw_ref, w1_ref, w3_ref, w2_ref, y_ref,
                  xbuf, sem, w1b, w3b, w2b, *, tm):
    t = pl.program_id(0)
    nv = nvalid_ref[0]
    valid = t < nv
    dc = w1_ref.shape[1] // LANES

    def issue(tile, slot):
        def body(i, c):
            _gather_copy(x_hbm, xbuf, sem, slot, i, tok_ref[tile * tm + i], dc).start()
            return c
        lax.fori_loop(0, tm, body, 0, unroll=8)

    @pl.when((t == 0) & valid)
    def _():
        issue(0, 0)

    @pl.when(t + 1 < nv)
    def _():
        issue(t + 1, (t + 1) % 2)

    @pl.when(valid)
    def _():
        prev = te_ref[jnp.maximum(t - 1, 0)]

        @pl.when((t == 0) | (te_ref[t] != prev))
        def _():
            w1b[...] = w1_ref[0].astype(BF16)
            w3b[...] = w3_ref[0].astype(BF16)
            w2b[...] = w2_ref[0].astype(BF16)

        slot = t % 2

        def drain(i, c):
            _gather_copy(x_hbm, xbuf, sem, slot, 0, 0, dc).wait()
            return c
        lax.fori_loop(0, tm, drain, 0, unroll=8)

        x = _load_token_rows(xbuf.at[slot], dc).astype(BF16)
        h1 = jnp.dot(x, w1b[...], preferred_element_type=F32)
        h3 = jnp.dot(x, w3b[...], preferred_element_type=F32)
        hh = (h1 * _sigmoid(h1)) * h3 * rw_ref[...]
        _store_token_rows(y_ref, jnp.dot(hh.astype(BF16), w2b[...], preferred_element_type=F32))

    @pl.when(jnp.logical_not(valid))
    def _():
        y_ref[...] = jnp.zeros_like(y_ref)


def _moe_ffn(xn, row_tok, row_w, tile_exp, nvalid_tiles, w1, w3, w2, n_tiles, tm):
    d, f = w1.shape[1], w1.shape[2]
    dc = d // LANES
    rows = n_tiles * tm
    def last_valid(t, nv):
        return jnp.minimum(t, jnp.maximum(nv[0] - 1, 0))
    return pl.pallas_call(
        functools.partial(_moe_ffn_body, tm=tm),
        grid_spec=pltpu.PrefetchScalarGridSpec(
            num_scalar_prefetch=3,
            grid=(n_tiles,),
            in_specs=[pl.BlockSpec(memory_space=pl.ANY),
                      pl.BlockSpec((tm, 1), lambda t, te, nv, tk: (last_valid(t, nv), 0)),
                      pl.BlockSpec((1, d, f), lambda t, te, nv, tk: (te[t], 0, 0)),
                      pl.BlockSpec((1, d, f), lambda t, te, nv, tk: (te[t], 0, 0)),
                      pl.BlockSpec((1, f, d), lambda t, te, nv, tk: (te[t], 0, 0))],
            out_specs=pl.BlockSpec((tm * dc, LANES), lambda t, te, nv, tk: (t, 0)),
            scratch_shapes=[pltpu.VMEM((2, tm * dc, LANES), F32), pltpu.SemaphoreType.DMA((2,)),
                            pltpu.VMEM((d, f), BF16), pltpu.VMEM((d, f), BF16), pltpu.VMEM((f, d), BF16)]),
        out_shape=jax.ShapeDtypeStruct((rows * dc, LANES), F32),
        compiler_params=_cparams(("arbitrary",)),
        name="moe_ffn",
    )(tile_exp, nvalid_tiles, row_tok, xn, row_w, w1, w3, w2)


def _combine_copy(y_hbm, buf, sem, slot, k, i, row, dc):
    src = y_hbm.at[pl.ds(pl.multiple_of(row * dc, dc), dc)]
    dst = buf.at[slot, k, pl.ds(pl.multiple_of(i * dc, dc), dc)]
    return pltpu.make_async_copy(src, dst, sem.at[slot])


def _moe_combine_body(pos_ref, h_ref, y_hbm, *rest, tm, final_eps):
    if final_eps is None:
        o_ref, buf, sem = rest
    else:
        gf_ref, o_ref, on_ref, buf, sem = rest
    t = pl.program_id(0)
    nt = pl.num_programs(0)
    dc = h_ref.shape[1] // LANES

    def issue(tile, slot):
        def body(i, c):
            a = 2 * (tile * tm + i)
            _combine_copy(y_hbm, buf, sem, slot, 0, i, pos_ref[a], dc).start()
            _combine_copy(y_hbm, buf, sem, slot, 1, i, pos_ref[a + 1], dc).start()
            return c
        lax.fori_loop(0, tm, body, 0, unroll=8)

    @pl.when(t == 0)
    def _():
        issue(0, 0)

    @pl.when(t + 1 < nt)
    def _():
        issue(t + 1, (t + 1) % 2)

    slot = t % 2

    def drain(i, c):
        _combine_copy(y_hbm, buf, sem, slot, 0, 0, 0, dc).wait()
        _combine_copy(y_hbm, buf, sem, slot, 1, 0, 0, dc).wait()
        return c
    lax.fori_loop(0, tm, drain, 0, unroll=8)

    out = h_ref[...] + _load_token_rows(buf.at[slot, 0], dc) + _load_token_rows(buf.at[slot, 1], dc)
    o_ref[...] = out
    if final_eps is not None:
        inv = lax.rsqrt(jnp.mean(out * out, axis=-1, keepdims=True) + final_eps)
        on_ref[...] = (out * inv) * gf_ref[...]


def _moe_combine(h, ys, pos, tm, final_g=None):
    n_tok, d = h.shape
    tok = pl.BlockSpec((tm, d), lambda i, p: (i, 0))
    in_specs = [tok, pl.BlockSpec(memory_space=pl.ANY)]
    args = [pos, h, ys]
    out_specs = [tok]
    out_shape = [jax.ShapeDtypeStruct((n_tok, d), F32)]
    if final_g is not None:
        in_specs.append(pl.BlockSpec((1, d), lambda i, p: (0, 0)))
        args.append(final_g.reshape(1, d))
        out_specs.append(tok)
        out_shape.append(jax.ShapeDtypeStruct((n_tok, d), F32))
    return pl.pallas_call(
        functools.partial(_moe_combine_body, tm=tm, final_eps=None if final_g is None else NORM_EPS),
        grid_spec=pltpu.PrefetchScalarGridSpec(
            num_scalar_prefetch=1,
            grid=(n_tok // tm,),
            in_specs=in_specs,
            out_specs=out_specs,
            scratch_shapes=[pltpu.VMEM((2, 2, tm * (d // LANES), LANES), F32), pltpu.SemaphoreType.DMA((2,))]),
        out_shape=out_shape,
        compiler_params=_cparams(("arbitrary",)),
        name="moe_combine",
    )(*args)


def _hier_moe(h, g, w_grp, b_grp, w_exp, b_exp, w1, w3, w2, final_g=None):
    n_tok, d = h.shape
    n_exp = w1.shape[0]
    tm = MOE_TILE
    xn, idx, cw = _router(h, g, w_grp, b_grp, w_exp, b_exp, tm=256)

    n_asg = 2 * n_tok
    flat_e = idx[:, :2].reshape(n_asg)
    flat_c = cw[:, :2].reshape(n_asg)
    onehot = (flat_e[:, None] == jnp.arange(n_exp, dtype=I32)[None, :]).astype(I32)
    csum = jnp.cumsum(onehot, axis=0)
    counts = csum[-1]
    rank = jnp.take_along_axis(csum - onehot, flat_e[:, None], axis=1)[:, 0]
    padded = ((counts + tm - 1) // tm) * tm
    ends = jnp.cumsum(padded)
    pos = (ends - padded)[flat_e] + rank
    n_tiles = n_asg // tm + n_exp
    rows = n_tiles * tm
    row_tok = jnp.zeros((rows,), I32).at[pos].set(jnp.arange(n_asg, dtype=I32) // 2)
    row_w = jnp.zeros((rows,), F32).at[pos].set(flat_c).reshape(rows, 1)
    nvalid = (ends[-1] // tm).astype(I32).reshape(1)
    tile_ids = jnp.minimum(jnp.arange(n_tiles, dtype=I32), nvalid - 1)
    tile_exp = jnp.searchsorted(ends, tile_ids * tm, side="right").astype(I32)

    ys = _moe_ffn(xn, row_tok, row_w, tile_exp, nvalid, w1, w3, w2, n_tiles, tm)
    return _moe_combine(h, ys, pos.astype(I32), 128, final_g)


def kernel(x_prompt, x_sample, cache_k, cache_v, page_table, state_wkv, state_shift, norm_mix, norm_ffn, norm_final, even_w_in, even_w_out, rwkv_mu, rwkv_w0, rwkv_w2, rwkv_a0, rwkv_a2, rwkv_g2, rwkv_k_k, rwkv_k_a, rwkv_r_k, rwkv_lnx_g, rwkv_lnx_b, diff_lam_q1, diff_lam_k1, diff_lam_q2, diff_lam_k2, diff_subln_g, gmlp_w_in, gmlp_ln_g, gmlp_ln_b, gmlp_w_s, gmlp_b_s, gmlp_w_out, moe_w_grp, moe_b_grp, moe_w_exp, moe_b_exp, moe_w1, moe_w3, moe_w2):
    bp, tp, d = x_prompt.shape
    bs, ts, _ = x_sample.shape
    n_p, n_s = bp * tp, bs * ts
    n_tok = n_p + n_s
    depth = norm_mix.shape[0]
    npages, psz = page_table.shape[1], cache_k.shape[2]
    past_len = npages * psz
    d_a = rwkv_w0.shape[1]
    d_ap = rwkv_mu.shape[1]
    d_q = cache_k.shape[3] * cache_k.shape[4]
    n_pool = cache_k.shape[1]
    nheads = d_a // RWKV_HEAD_DIM
    tm = 512 if n_p % 512 == 0 and n_s % 512 == 0 else 128
    tq = 512 if tp % 512 == 0 else 128
    tc = min(tp, CHUNK)

    h = jnp.concatenate([x_prompt.reshape(n_p, d), x_sample.reshape(n_s, d)], axis=0)
    pos = jnp.concatenate([jnp.tile(jnp.arange(tp), bp), jnp.tile(past_len + jnp.arange(ts), bs)])
    moe = lambda l, hh, fg=None: _hier_moe(hh, norm_ffn[l], moe_w_grp[l], moe_b_grp[l], moe_w_exp[l],
                                          moe_b_exp[l], moe_w1[l], moe_w3[l], moe_w2[l], fg)

    outs = {k: [] for k in ("k_p", "v_p", "wkv_p", "sh_p", "k_s", "v_s", "wkv_s", "sh_s", "gv_s")}
    y_norm = None
    for layer in range(depth):
        j = layer // 2
        xn = _rmsnorm(h, norm_mix[layer], NORM_EPS, BF16, tm)
        last = layer == depth - 1
        if layer % 2 == 0:
            proj = _matmul([xn], even_w_in[j], bm=tm, bn=even_w_in.shape[2] // 5, name="even_in_proj")
            prm = dict(mu=rwkv_mu[j].reshape(1, d_ap), w0=rwkv_w0[j].reshape(1, d_a), w2=rwkv_w2[j],
                       a0=rwkv_a0[j].reshape(1, d_a), a2=rwkv_a2[j], g2=rwkv_g2[j],
                       k_k=rwkv_k_k[j].reshape(1, d_a), k_a=rwkv_k_a[j].reshape(1, d_a),
                       r_k=rwkv_r_k[j].reshape(1, d_a))
            pa = proj[:, :d_ap]
            tmr = 256 if tp % 256 == 0 else 128
            pa_p = pa[:n_p].reshape(bp, tp, d_ap)
            bnd = jnp.concatenate([jnp.zeros((bp, 1, d_ap), F32), pa_p[:, tmr - 1:tp - 1:tmr]], axis=1)
            repl_p = jnp.zeros((n_p // tmr, 8, d_ap), F32).at[:, 0].set(bnd.reshape(n_p // tmr, d_ap))
            repl_p = repl_p.reshape(n_p // tmr * 8, d_ap)
            repl_s = jnp.repeat(state_shift[j], ts, axis=0)
            pre_p = _rwkv_pre(proj, repl_p, tmr, 0, n_p, tmr, prm)
            pre_s = _rwkv_pre(proj, repl_s, ts, n_p, n_s, tmr, prm)
            oa_p, wkv_p = _rwkv_scan(pre_p, rwkv_lnx_g[j], rwkv_lnx_b[j], None, bp, tp, tc,
                                     2 if bp % 2 == 0 else 1)
            oa_s, wkv_s = _rwkv_scan(pre_s, rwkv_lnx_g[j], rwkv_lnx_b[j], state_wkv[j], bs, ts, ts,
                                     2 if bs % 2 == 0 else 1)

            qs, qf, kf, kb, vb = _rope(proj, pos, d_ap, d_q, tm)
            vf = proj[:, d_ap + 2 * d_q:]
            lam_init = 0.8 - 0.6 * math.exp(-0.3 * layer)
            lam_p = jnp.stack([diff_lam_q1[j], diff_lam_k1[j], diff_lam_q2[j], diff_lam_k2[j]])
            ob_p = _attn_prompt(qs[:n_p], kb[:n_p], vb[:n_p], lam_p, diff_subln_g[j], bp, tp, tq, lam_init)
            ob_s = _attn_sample(qf[n_p:], kf[n_p:], vf[n_p:], cache_k, cache_v, j,
                                page_table.reshape(-1).astype(I32),
                                lam_p, diff_subln_g[j], bs, ts, npages, lam_init)
            oa = jnp.concatenate([oa_p, oa_s.astype(BF16)], axis=0)
            ob = jnp.concatenate([ob_p, ob_s.astype(BF16)], axis=0)
            h = _matmul([oa, ob], even_w_out[j], bm=tm, bn=1024, res=h, name="even_out_proj")

            outs["k_p"].append(kf[:n_p].reshape(bp, tp, -1, DIFF_QK_DIM))
            outs["v_p"].append(vf[:n_p].reshape(bp, tp, -1, DIFF_V_DIM))
            outs["wkv_p"].append(wkv_p)
            outs["sh_p"].append(pa_p[:, -1])
            outs["k_s"].append(kf[n_p:].reshape(bs, ts, -1, DIFF_QK_DIM))
            outs["v_s"].append(vf[n_p:].reshape(bs, ts, -1, DIFF_V_DIM))
            outs["wkv_s"].append(wkv_s)
            outs["sh_s"].append(pa[n_p:].reshape(bs, ts, d_ap)[:, -1])
        else:
            d_c = gmlp_ln_g.shape[1]
            z = _matmul([xn], gmlp_w_in[j], bm=tm, bn=1024, act="gelu", name="gmlp_in_proj")
            ws = gmlp_w_s[j]
            bsb = gmlp_b_s[j]
            gd = d_c // GMLP_GROUPS
            lp = min(tp, CHUNK)
            wm_p = jnp.tril(ws[:, :lp, :lp])
            bm_p = jnp.broadcast_to(bsb[:, :lp, None], (GMLP_GROUPS, lp, gd))
            rep = CHUNK // ts
            eye = jnp.eye(rep, dtype=F32)
            wm_s = jnp.einsum("ab,gts->gatbs", eye, jnp.tril(ws[:, :ts, :ts])).reshape(GMLP_GROUPS, CHUNK, CHUNK)
            bm_s = jnp.broadcast_to(jnp.tile(bsb[:, :ts], (1, rep))[:, :, None], (GMLP_GROUPS, CHUNK, gd))
            (y_p,) = _gmlp(z, 0, n_p, gmlp_ln_g[j], gmlp_ln_b[j], wm_p, bm_p, False)
            y_s, v_rows = _gmlp(z, n_p, n_s, gmlp_ln_g[j], gmlp_ln_b[j], wm_s, bm_s, True)
            yin = jnp.concatenate([y_p, y_s], axis=0)
            h = _matmul([yin], gmlp_w_out[j], bm=tm, bn=1024, res=h, name="gmlp_out_proj")
            outs["gv_s"].append(v_rows.reshape(bs, ts, d_c))
        res = moe(layer, h, norm_final if last else None)
        h = res[0]
        if last:
            y_norm = res[1]

    y_prompt = y_norm[:n_p].reshape(bp, tp, d)
    y_sample = y_norm[n_p:].reshape(bs, ts, d)
    st = lambda k: jnp.stack(outs[k])
    return (y_prompt, y_sample, st("k_p"), st("v_p"), st("wkv_p"), st("sh_p"),
            st("k_s"), st("v_s"), st("wkv_s"), st("sh_s"), st("gv_s"))
```

```python
import functools
import math

import jax
import jax.numpy as jnp
from jax import lax
from jax.experimental import pallas as pl
from jax.experimental.pallas import tpu as pltpu

F32 = jnp.float32
BF16 = jnp.bfloat16
I32 = jnp.int32

RWKV_HEAD_DIM = 64
RWKV_GN_EPS = 64e-5
DIFF_QK_DIM = 64
DIFF_V_DIM = 128
ROT_DIM = 16
ROPE_THETA = 500000.0
ATTN_SCALE = DIFF_QK_DIM ** -0.5
NEG_INF = -1e30
NORM_EPS = 1e-6
SUBLN_EPS = 1e-5
GMLP_LN_EPS = 1e-5
GMLP_GROUPS = 8
CHUNK = 128
N_EXP_GROUPS = 4
EXP_PER_GROUP = 8
LORA_W, LORA_A, LORA_G = 64, 64, 128

LANES = 128
MXU_TILE = 256
VMEM_LIMIT = 56 * 1024 * 1024
MOE_TILE = 256
HIGHEST = lax.Precision.HIGHEST


def _cparams(sem):
    return pltpu.CompilerParams(dimension_semantics=sem, vmem_limit_bytes=VMEM_LIMIT)


def _rmsnorm_body(x_ref, g_ref, o_ref, *, eps):
    x = x_ref[...]
    inv = lax.rsqrt(jnp.mean(x * x, axis=-1, keepdims=True) + eps)
    o_ref[...] = ((x * inv) * g_ref[...]).astype(o_ref.dtype)


def _rmsnorm(x, g, eps, out_dtype, tm):
    m, d = x.shape
    return pl.pallas_call(
        functools.partial(_rmsnorm_body, eps=eps),
        grid=(m // tm,),
        in_specs=[pl.BlockSpec((tm, d), lambda i: (i, 0)),
                  pl.BlockSpec((1, d), lambda i: (0, 0))],
        out_specs=pl.BlockSpec((tm, d), lambda i: (i, 0)),
        out_shape=jax.ShapeDtypeStruct((m, d), out_dtype),
        compiler_params=_cparams(("arbitrary",)),
        name="rmsnorm",
    )(x, g.reshape(1, d))


def _gelu_exact(x):
    return 0.5 * x * (1.0 + lax.erf(x * (1.0 / math.sqrt(2.0))))


def _mm_body(*refs, n_x, act, has_res):
    xs = refs[:n_x]
    w_ref = refs[n_x]
    res_ref = refs[n_x + 1] if has_res else None
    o_ref = refs[n_x + 1 + has_res]
    wb_ref = refs[n_x + 2 + has_res]

    @pl.when(pl.program_id(1) == 0)
    def _():
        wb_ref[...] = w_ref[...].astype(BF16)

    kx = w_ref.shape[0] // n_x
    acc = None
    for i, x_ref in enumerate(xs):
        part = jnp.dot(x_ref[...], wb_ref[i * kx:(i + 1) * kx, :], preferred_element_type=F32)
        acc = part if acc is None else acc + part
    if act == "gelu":
        acc = _gelu_exact(acc)
    if has_res:
        acc = acc + res_ref[...]
    o_ref[...] = acc.astype(o_ref.dtype)


def _matmul(xs, w, *, bm, bn, act=None, res=None, out_dtype=F32, name="matmul"):
    m = xs[0].shape[0]
    k, n = w.shape
    kx = k // len(xs)
    in_specs = [pl.BlockSpec((bm, kx), lambda j, i: (i, 0)) for _ in xs]
    in_specs.append(pl.BlockSpec((k, bn), lambda j, i: (0, j)))
    args = list(xs) + [w]
    if res is not None:
        in_specs.append(pl.BlockSpec((bm, bn), lambda j, i: (i, j)))
        args.append(res)
    return pl.pallas_call(
        functools.partial(_mm_body, n_x=len(xs), act=act, has_res=res is not None),
        grid=(n // bn, m // bm),
        in_specs=in_specs,
        out_specs=pl.BlockSpec((bm, bn), lambda j, i: (i, j)),
        out_shape=jax.ShapeDtypeStruct((m, n), out_dtype),
        scratch_shapes=[pltpu.VMEM((k, bn), BF16)],
        compiler_params=_cparams(("arbitrary", "arbitrary")),
        name=name,
    )(*args)


def _block_ones(n, seg):
    r = lax.broadcasted_iota(I32, (n, n), 0) // seg
    c = lax.broadcasted_iota(I32, (n, n), 1) // seg
    return jnp.where(r == c, 1.0, 0.0).astype(BF16)


def _seg_sum(x, ones_bd):
    hi = x.astype(BF16)
    lo = (x - hi.astype(F32)).astype(BF16)
    outs = []
    for c in range(x.shape[1] // MXU_TILE):
        sl = slice(c * MXU_TILE, (c + 1) * MXU_TILE)
        outs.append(jnp.dot(hi[:, sl], ones_bd, preferred_element_type=F32)
                    + jnp.dot(lo[:, sl], ones_bd, preferred_element_type=F32))
    return jnp.concatenate(outs, axis=1)


def _sigmoid(x):
    return 1.0 / (1.0 + jnp.exp(-x))


def _rwkv_pre_body(pa_ref, repl_ref, mu_ref, w0_ref, w2_ref, a0_ref, a2_ref, g2_ref,
                   kk_ref, ka_ref, rk_ref,
                   r_o, dec_o, k_o, v_o, kkn_o, b_o, g_o, bonus_o, *, period, d_a):
    pa = pa_ref[...]
    tm = pa.shape[0]
    row = lax.broadcasted_iota(I32, pa.shape, 0)
    prev = pltpu.roll(pa, 1, 0)
    repl = repl_ref[...]
    if repl.shape[0] != tm:
        repl = jnp.broadcast_to(repl[0:1, :], pa.shape)
    prev = jnp.where(row % period == 0, repl, prev)
    xm = pa + (prev - pa) * mu_ref[...]
    r = xm[:, 0:d_a]
    k = xm[:, d_a:2 * d_a]
    v = xm[:, 2 * d_a:3 * d_a]
    o = 3 * d_a
    wd = xm[:, o:o + LORA_W]
    ad = xm[:, o + LORA_W:o + LORA_W + LORA_A]
    gd = xm[:, o + LORA_W + LORA_A:o + LORA_W + LORA_A + LORA_G]

    z = w0_ref[...] + jnp.dot(jnp.tanh(wd), w2_ref[...], precision=HIGHEST, preferred_element_type=F32)
    w = jnp.minimum(z, 0.0) - jnp.log1p(jnp.exp(-jnp.abs(z))) - 0.5
    dec = jnp.exp(-jnp.exp(w))
    a = _sigmoid(a0_ref[...] + jnp.dot(ad, a2_ref[...], precision=HIGHEST, preferred_element_type=F32))
    g = jnp.dot(_sigmoid(gd).astype(BF16), g2_ref[...].astype(BF16), preferred_element_type=F32)

    ones_bd = _block_ones(MXU_TILE, RWKV_HEAD_DIM)
    kk = k * kk_ref[...]
    nrm = jnp.sqrt(_seg_sum(kk * kk, ones_bd))
    kkn = kk / jnp.maximum(nrm, 1e-12)
    kh = k * (1.0 + (a - 1.0) * ka_ref[...])
    bonus = _seg_sum(r * kh * rk_ref[...], ones_bd) * v

    r_o[...] = r
    dec_o[...] = dec
    k_o[...] = kh
    v_o[...] = v
    kkn_o[...] = kkn
    b_o[...] = kkn * a
    g_o[...] = g
    bonus_o[...] = bonus


def _rwkv_pre(proj, repl, period, row0, rows, tm, prm):
    d_a = prm["w0"].shape[1]
    d_ap = prm["mu"].shape[1]
    nt = rows // tm
    t0 = row0 // tm
    vec = lambda n: pl.BlockSpec((1, n), lambda i: (0, 0))
    full = lambda a: pl.BlockSpec(a.shape, lambda i: (0, 0))
    out_spec = pl.BlockSpec((tm, d_a), lambda i: (i, 0))
    outs = pl.pallas_call(
        functools.partial(_rwkv_pre_body, period=period, d_a=d_a),
        grid=(nt,),
        in_specs=[pl.BlockSpec((tm, d_ap), lambda i: (t0 + i, 0)),
                  pl.BlockSpec((repl.shape[0] // nt, d_ap), lambda i: (i, 0)),
                  vec(d_ap), vec(d_a), full(prm["w2"]), vec(d_a), full(prm["a2"]), full(prm["g2"]),
                  vec(d_a), vec(d_a), vec(d_a)],
        out_specs=[out_spec] * 8,
        out_shape=[jax.ShapeDtypeStruct((rows, d_a), F32)] * 8,
        compiler_params=_cparams(("arbitrary",)),
        name="rwkv_pre",
    )(proj, repl, prm["mu"], prm["w0"], prm["w2"], prm["a0"], prm["a2"], prm["g2"],
      prm["k_k"], prm["k_a"], prm["r_k"])
    return outs


def _rwkv_scan_body(*refs, tc, tp, nheads, nb, has_state):
    if has_state:
        s0_ref = refs[0]
        refs = refs[1:]
    (r_ref, w_ref, k_ref, v_ref, kk_ref, b_ref, g_ref, bonus_ref, lg_ref, lb_ref,
     o_ref, sout_ref, s_scr, vth_scr, ot_scr) = refs
    n = RWKV_HEAD_DIM
    c = pl.program_id(1)
    assert tp == 2 * n
    nh = nb * nheads
    heads = [(b, h) for b in range(nb) for h in range(nheads)]

    @pl.when(c == 0)
    def _():
        if has_state:
            zero = jnp.zeros((n, n), F32)
            for i, (b, h) in enumerate(heads):
                s0 = s0_ref[b, h]
                s_scr[i] = jnp.concatenate([s0, zero] if h % 2 == 0 else [zero, s0], axis=1)
        else:
            s_scr[...] = jnp.zeros_like(s_scr)

    for i, (b, h) in enumerate(heads):
        vh = v_ref[b, :, h * n:(h + 1) * n]
        if tp != tc:
            vh = jnp.concatenate([vh, jnp.zeros((tp - tc, n), F32)], axis=0)
        vth_scr[i * n:(i + 1) * n, :] = vh.T.astype(BF16)
    ot_scr[...] = jnp.zeros_like(ot_scr)

    trow = lax.broadcasted_iota(I32, (tp, 2 * n), 0)
    tcol = lax.broadcasted_iota(I32, (2 * n, tp), 1)
    head_par = lax.broadcasted_iota(I32, (nh, 1, 2 * n), 0) % 2
    lane_par = lax.broadcasted_iota(I32, (nh, 1, 2 * n), 2) // n
    own = head_par == lane_par

    sub = 8

    def per_head(ref, base, masked):
        xs = [ref[b, pl.ds(base, sub), :] for b in range(nb)]
        x = jnp.stack([xs[b][:, (h // 2) * 2 * n:(h // 2 + 1) * 2 * n] for b, h in heads], axis=0)
        return jnp.where(own, x, 0.0) if masked else x

    def group(tg, carry):
        base = pl.multiple_of(tg * sub, sub)
        kk8 = per_head(kk_ref, base, True)
        b8 = per_head(b_ref, base, True)
        k8 = per_head(k_ref, base, True)
        w8 = per_head(w_ref, base, False)
        r8 = per_head(r_ref, base, False)
        st = s_scr[...]
        vth = vth_scr[...]
        o8 = None
        for s in range(sub):
            t = base + s
            pick = jnp.where(trow == t, 1.0, 0.0).astype(BF16)
            v_col = jnp.dot(vth, pick, preferred_element_type=F32).reshape(nh, n, 2 * n)
            sa = -jnp.sum(st * kk8[:, s:s + 1], axis=-1, keepdims=True)
            st = st * w8[:, s:s + 1] + sa * b8[:, s:s + 1] + v_col * k8[:, s:s + 1]
            q = (st * r8[:, s:s + 1]).reshape(nh * n, 2 * n).astype(BF16)
            put = jnp.where(tcol == t, 1.0, 0.0).astype(BF16)
            o_t = jnp.dot(q, put, preferred_element_type=F32)
            o8 = o_t if o8 is None else o8 + o_t
        s_scr[...] = st
        ot_scr[...] = ot_scr[...] + o8.reshape(nh, n, tp)
        return carry

    lax.fori_loop(0, tc // sub, group, 0)

    for b in range(nb):
        for hp in range(nheads // 2):
            parts = []
            for h in (2 * hp, 2 * hp + 1):
                oc = ot_scr[b * nheads + h]
                mean = jnp.mean(oc, axis=0, keepdims=True)
                d = oc - mean
                var = jnp.mean(d * d, axis=0, keepdims=True)
                on = d * lax.rsqrt(var + RWKV_GN_EPS)
                parts.append(on.T[:tc])
            sl = slice(hp * 2 * n, (hp + 1) * 2 * n)
            on2 = jnp.concatenate(parts, axis=1)
            o = (on2 * lg_ref[:, sl] + lb_ref[:, sl] + bonus_ref[b, :, sl]) * g_ref[b, :, sl]
            o_ref[b, :, sl] = o.astype(o_ref.dtype)

    @pl.when(c == pl.num_programs(1) - 1)
    def _():
        for i, (b, h) in enumerate(heads):
            s_h = s_scr[i]
            sout_ref[b, h] = s_h[:, 0:n] if h % 2 == 0 else s_h[:, n:2 * n]


def _rwkv_scan(pre, lnx_g, lnx_b, state, nseq, tseq, tc, nb):
    rows, d_a = pre[0].shape
    nheads = d_a // RWKV_HEAD_DIM
    nch = tseq // tc
    tp = max(tc, LANES)
    n = RWKV_HEAD_DIM
    nh = nb * nheads
    tok = pl.BlockSpec((nb, tc, d_a), lambda s, c: (s, c, 0))
    vec = pl.BlockSpec((1, d_a), lambda s, c: (0, 0))
    st = pl.BlockSpec((nb, nheads, n, n), lambda s, c: (s, 0, 0, 0))
    has_state = state is not None
    in_specs = ([st] if has_state else []) + [tok] * 8 + [vec, vec]
    args = ([state] if has_state else []) + [a.reshape(nseq, tseq, d_a) for a in pre] + [
        lnx_g.reshape(1, d_a), lnx_b.reshape(1, d_a)]
    oa, s_fin = pl.pallas_call(
        functools.partial(_rwkv_scan_body, tc=tc, tp=tp, nheads=nheads, nb=nb, has_state=has_state),
        grid=(nseq // nb, nch),
        in_specs=in_specs,
        out_specs=[tok, st],
        out_shape=[jax.ShapeDtypeStruct((nseq, tseq, d_a), BF16 if tc % 16 == 0 else F32),
                   jax.ShapeDtypeStruct((nseq, nheads, n, n), F32)],
        scratch_shapes=[pltpu.VMEM((nh, n, 2 * n), F32),
                        pltpu.VMEM((nh * n, tp), BF16),
                        pltpu.VMEM((nh, n, tp), F32)],
        compiler_params=_cparams(("arbitrary", "arbitrary")),
        name="rwkv_scan",
    )(*args)
    return oa.reshape(rows, d_a), s_fin


def _rope_body(q_ref, k_ref, v_ref, c_ref, s1_ref, s2_ref, qs_o, qf_o, kf_o, kb_o, vb_o):
    c, s1, s2 = c_ref[...], s1_ref[...], s2_ref[...]
    half = ROT_DIM // 2
    w = q_ref.shape[1]

    def rot(x):
        return x * c + pltpu.roll(x, w - half, 1) * s1 + pltpu.roll(x, half, 1) * s2

    qr = rot(q_ref[...]) * ATTN_SCALE
    qf_o[...] = qr
    qs_o[...] = qr.astype(BF16)
    kr = rot(k_ref[...])
    kf_o[...] = kr
    kb_o[...] = kr.astype(BF16)
    vb_o[...] = v_ref[...].astype(BF16)


def _rope(proj, pos, d_ap, d_q, tm):
    n_tok = proj.shape[0]
    w = MXU_TILE
    half = ROT_DIM // 2
    inv_freq = ROPE_THETA ** (-jnp.arange(half, dtype=F32) / half)
    ang = pos.astype(F32)[:, None] * inv_freq[None, :]
    cos, sin = jnp.cos(ang), jnp.sin(ang)
    pad = DIFF_QK_DIM - ROT_DIM
    ones = jnp.ones((n_tok, pad), F32)
    zeros = jnp.zeros((n_tok, pad), F32)
    zh = jnp.zeros((n_tok, half), F32)
    reps = w // DIFF_QK_DIM
    c_t = jnp.tile(jnp.concatenate([cos, cos, ones], axis=1), (1, reps))
    s1_t = jnp.tile(jnp.concatenate([-sin, zh, zeros], axis=1), (1, reps))
    s2_t = jnp.tile(jnp.concatenate([zh, sin, zeros], axis=1), (1, reps))
    nq = d_q // w
    qb, kb, vb = d_ap // w, d_ap // w + nq, d_ap // w + 2 * nq
    tab = pl.BlockSpec((tm, w), lambda i, j: (i, 0))
    out = pl.BlockSpec((tm, w), lambda i, j: (i, j))
    return pl.pallas_call(
        _rope_body,
        grid=(n_tok // tm, nq),
        in_specs=[pl.BlockSpec((tm, w), lambda i, j: (i, qb + j)),
                  pl.BlockSpec((tm, w), lambda i, j: (i, kb + j)),
                  pl.BlockSpec((tm, w), lambda i, j: (i, vb + j)),
                  tab, tab, tab],
        out_specs=[out, out, out, out, out],
        out_shape=[jax.ShapeDtypeStruct((n_tok, d_q), BF16),
                   jax.ShapeDtypeStruct((n_tok, d_q), F32),
                   jax.ShapeDtypeStruct((n_tok, d_q), F32),
                   jax.ShapeDtypeStruct((n_tok, d_q), BF16),
                   jax.ShapeDtypeStruct((n_tok, d_q), BF16)],
        compiler_params=_cparams(("arbitrary", "arbitrary")),
        name="rope",
    )(proj, proj, proj, c_t, s1_t, s2_t)


def _lambda(lam_ref, lam_init):
    lp = lam_ref[...]
    l1 = jnp.sum(lp[0:1] * lp[1:2], axis=-1, keepdims=True)
    l2 = jnp.sum(lp[2:3] * lp[3:4], axis=-1, keepdims=True)
    return jnp.exp(l1) - jnp.exp(l2) + lam_init


def _subln(o, g_ref, lam_init):
    inv = lax.rsqrt(jnp.mean(o * o, axis=-1, keepdims=True) + SUBLN_EPS)
    return (o * inv) * g_ref[...] * (1.0 - lam_init)


def _online_softmax_update(s, v, m_scr, l_scr, acc_scr):
    m_prev = m_scr[...]
    m_new = jnp.maximum(m_prev, jnp.max(s, axis=-1, keepdims=True))
    alpha = jnp.exp(m_prev - m_new)
    p = jnp.exp(s - m_new[:, 0:1])
    l_scr[...] = alpha * l_scr[...] + jnp.sum(p, axis=-1, keepdims=True)
    acc_scr[...] = alpha[:, 0:1] * acc_scr[...] + jnp.dot(p.astype(BF16), v, preferred_element_type=F32)
    m_scr[...] = m_new


def _attn_prompt_body(q_ref, k_ref, v_ref, lam_ref, g_ref, o_ref, qq_scr, m_scr, l_scr, acc_scr,
                      *, tq, lam_init):
    qi = pl.program_id(2)
    ki = pl.program_id(3)

    @pl.when(ki == 0)
    def _():
        q = q_ref[...]
        lane = lax.broadcasted_iota(I32, q.shape, 1)
        zero = jnp.zeros_like(q)
        qq_scr[0:tq] = jnp.where(lane < DIFF_QK_DIM, q, zero)
        qq_scr[tq:2 * tq] = jnp.where(lane >= DIFF_QK_DIM, q, zero)
        m_scr[...] = jnp.full_like(m_scr, NEG_INF)
        l_scr[...] = jnp.zeros_like(l_scr)
        acc_scr[...] = jnp.zeros_like(acc_scr)

    rb = min(tq, 256)

    def tile(diagonal):
        for r0 in range(0, 2 * tq, rb):
            rows = slice(r0, r0 + rb)
            nk = min(tq, (r0 % tq) + rb) if diagonal else tq
            s = lax.dot_general(qq_scr[rows], k_ref[0:nk], (((1,), (1,)), ((), ())),
                                preferred_element_type=F32)
            if diagonal:
                row = (r0 % tq) + lax.broadcasted_iota(I32, s.shape, 0)
                col = lax.broadcasted_iota(I32, s.shape, 1)
                s = jnp.where(row >= col, s, NEG_INF)
            m_prev = m_scr[rows]
            m_new = jnp.maximum(m_prev, jnp.max(s, axis=-1, keepdims=True))
            alpha = jnp.exp(m_prev - m_new)
            p = jnp.exp(s - m_new[:, 0:1])
            l_scr[rows] = alpha * l_scr[rows] + jnp.sum(p, axis=-1, keepdims=True)
            acc_scr[rows] = alpha * acc_scr[rows] + jnp.dot(p.astype(BF16), v_ref[0:nk],
                                                            preferred_element_type=F32)
            m_scr[rows] = m_new

    @pl.when(ki < qi)
    def _():
        tile(False)

    @pl.when(ki == qi)
    def _():
        tile(True)
        lam = _lambda(lam_ref, lam_init)
        o1 = acc_scr[0:tq] / l_scr[0:tq, 0:1]
        o2 = acc_scr[tq:2 * tq] / l_scr[tq:2 * tq, 0:1]
        o_ref[...] = _subln(o1 - lam * o2, g_ref, lam_init).astype(o_ref.dtype)


def _attn_prompt(qs, kb, vb, lam_p, subln_g, nseq, tseq, tq, lam_init):
    rows, d_q = qs.shape
    nh = d_q // DIFF_V_DIM
    nq = tseq // tq
    w = DIFF_V_DIM
    return pl.pallas_call(
        functools.partial(_attn_prompt_body, tq=tq, lam_init=lam_init),
        grid=(nseq, nh, nq, nq),
        in_specs=[pl.BlockSpec((tq, w), lambda b, h, i, j: (b * nq + i, h)),
                  pl.BlockSpec((tq, w), lambda b, h, i, j: (b * nq + jnp.minimum(i, j), h)),
                  pl.BlockSpec((tq, w), lambda b, h, i, j: (b * nq + jnp.minimum(i, j), h)),
                  pl.BlockSpec(lam_p.shape, lambda b, h, i, j: (0, 0)),
                  pl.BlockSpec((1, w), lambda b, h, i, j: (0, 0))],
        out_specs=pl.BlockSpec((tq, w), lambda b, h, i, j: (b * nq + i, h)),
        out_shape=jax.ShapeDtypeStruct((nseq * tseq, d_q), BF16),
        scratch_shapes=[pltpu.VMEM((2 * tq, w), BF16),
                        pltpu.VMEM((2 * tq, LANES), F32),
                        pltpu.VMEM((2 * tq, LANES), F32),
                        pltpu.VMEM((2 * tq, w), F32)],
        compiler_params=_cparams(("arbitrary", "arbitrary", "arbitrary", "arbitrary")),
        name="attn_prompt",
    )(qs, kb, vb, lam_p, subln_g.reshape(1, w))


def _attn_sample_body(pt_ref, q_ref, *refs, ts, lam_init, pages_per_step):
    kc_refs = refs[:pages_per_step]
    vc_refs = refs[pages_per_step:2 * pages_per_step]
    kn_ref, vn_ref, lam_ref, g_ref, o_ref, qbd_scr, m_scr, l_scr, acc_scr = refs[2 * pages_per_step:]
    p = pl.program_id(1)
    nheads_qk = q_ref.shape[1] // DIFF_QK_DIM

    @pl.when(p == 0)
    def _():
        q = q_ref[...]
        head = lax.broadcasted_iota(I32, q.shape, 1) // DIFF_QK_DIM
        zero = jnp.zeros_like(q)
        rows = [jnp.where(head == m, q, zero) for m in range(nheads_qk)]
        qbd_scr[...] = jnp.concatenate(rows, axis=0).astype(BF16)
        m_scr[...] = jnp.full_like(m_scr, NEG_INF)
        l_scr[...] = jnp.zeros_like(l_scr)
        acc_scr[...] = jnp.zeros_like(acc_scr)

    nh_v = nheads_qk // 2
    w = DIFF_V_DIM

    def update(s, v_heads):
        m_prev = m_scr[...]
        m_new = jnp.maximum(m_prev, jnp.max(s, axis=-1, keepdims=True))
        alpha = jnp.exp(m_prev - m_new)
        pr = jnp.exp(s - m_new[:, 0:1])
        l_scr[...] = alpha * l_scr[...] + jnp.sum(pr, axis=-1, keepdims=True)
        m_scr[...] = m_new
        pb = pr.astype(BF16)
        for h in range(nh_v):
            rows = slice(2 * h * ts, (2 * h + 2) * ts)
            acc_scr[rows] = alpha[rows] * acc_scr[rows] + jnp.dot(pb[rows], v_heads[h],
                                                                  preferred_element_type=F32)

    kt = jnp.concatenate([r[...].reshape(nheads_qk * DIFF_QK_DIM, r.shape[2]) for r in kc_refs],
                         axis=1).astype(BF16)
    s_past = jnp.dot(qbd_scr[...], kt, preferred_element_type=F32)
    npos = vc_refs[0].shape[0] // nh_v
    v_past = [jnp.concatenate([r[pl.ds(h, npos, stride=nh_v), :] for r in vc_refs], axis=0).astype(BF16)
              for h in range(nh_v)]
    update(s_past, v_past)

    @pl.when(p == pl.num_programs(1) - 1)
    def _():
        padk = jnp.zeros((LANES - ts, kn_ref.shape[1]), F32)
        kn = jnp.concatenate([kn_ref[...], padk], axis=0).astype(BF16)
        vn = jnp.concatenate([vn_ref[...], padk], axis=0).astype(BF16)
        s = lax.dot_general(qbd_scr[...], kn, (((1,), (1,)), ((), ())), preferred_element_type=F32)
        row = lax.broadcasted_iota(I32, s.shape, 0) % ts
        col = lax.broadcasted_iota(I32, s.shape, 1)
        s = jnp.where(row >= col, s, NEG_INF)
        update(s, [vn[:, h * w:(h + 1) * w] for h in range(nh_v)])
        lam = _lambda(lam_ref, lam_init)
        for h in range(nh_v):
            r1 = slice(2 * h * ts, (2 * h + 1) * ts)
            r2 = slice((2 * h + 1) * ts, (2 * h + 2) * ts)
            o1 = acc_scr[r1] / l_scr[r1, 0:1]
            o2 = acc_scr[r2] / l_scr[r2, 0:1]
            o_ref[:, h * w:(h + 1) * w] = _subln(o1 - lam * o2, g_ref, lam_init).astype(o_ref.dtype)


def _attn_sample(qs, kb, vb, cache_k, cache_v, layer_j, page_flat, lam_p, subln_g, nb, ts, npages, lam_init):
    d_q = qs.shape[1]
    nrow = (d_q // DIFF_QK_DIM) * ts
    gpp = 4 if npages % 4 == 0 else (2 if npages % 2 == 0 else 1)
    tok = pl.BlockSpec((ts, d_q), lambda b, p, pt: (b, 0))
    cache_k = jnp.transpose(cache_k, (0, 1, 3, 4, 2))
    cache_v = cache_v.reshape(cache_v.shape[:2] + (-1, cache_v.shape[-1]))

    def page(cache, g):
        return pl.BlockSpec((None, None) + cache.shape[2:],
                            lambda b, p, pt: (layer_j, pt[b * npages + p * gpp + g]) + (0,) * (cache.ndim - 2))

    return pl.pallas_call(
        functools.partial(_attn_sample_body, ts=ts, lam_init=lam_init, pages_per_step=gpp),
        grid_spec=pltpu.PrefetchScalarGridSpec(
            num_scalar_prefetch=1,
            grid=(nb, npages // gpp),
            in_specs=[tok] + [page(cache_k, g) for g in range(gpp)] + [page(cache_v, g) for g in range(gpp)]
                     + [tok, tok,
                        pl.BlockSpec(lam_p.shape, lambda b, p, pt: (0, 0)),
                        pl.BlockSpec((1, DIFF_V_DIM), lambda b, p, pt: (0, 0))],
            out_specs=tok,
            scratch_shapes=[pltpu.VMEM((nrow, d_q), BF16),
                            pltpu.VMEM((nrow, LANES), F32),
                            pltpu.VMEM((nrow, LANES), F32),
                            pltpu.VMEM((nrow, DIFF_V_DIM), F32)]),
        out_shape=jax.ShapeDtypeStruct((nb * ts, d_q), F32),
        compiler_params=_cparams(("arbitrary", "arbitrary")),
        name="attn_sample",
    )(page_flat, qs, *([cache_k] * gpp), *([cache_v] * gpp), kb, vb, lam_p, subln_g.reshape(1, DIFF_V_DIM))


def _gmlp_body(z_ref, lg_ref, lb_ref, wm_ref, bm_ref, y_ref, *maybe_v, d_c):
    u = z_ref[:, 0:d_c]
    v = z_ref[:, d_c:2 * d_c]
    mean = jnp.mean(v, axis=-1, keepdims=True)
    d = v - mean
    var = jnp.mean(d * d, axis=-1, keepdims=True)
    vn = (d * lax.rsqrt(var + GMLP_LN_EPS)) * lg_ref[...] + lb_ref[...]
    if maybe_v:
        maybe_v[0][...] = vn
    gd = d_c // GMLP_GROUPS
    vb = vn.astype(BF16)
    for g in range(GMLP_GROUPS):
        sl = slice(g * gd, (g + 1) * gd)
        mixed = jnp.dot(wm_ref[g].astype(BF16), vb[:, sl], preferred_element_type=F32) + bm_ref[g]
        y_ref[:, sl] = (u[:, sl] * mixed).astype(y_ref.dtype)


def _gmlp(z, row0, rows, ln_g, ln_b, wmix, bmix, emit_v):
    d_c = z.shape[1] // 2
    tm = CHUNK
    t0 = row0 // tm
    gd = d_c // GMLP_GROUPS
    out_specs = [pl.BlockSpec((tm, d_c), lambda i: (i, 0))]
    out_shape = [jax.ShapeDtypeStruct((rows, d_c), BF16)]
    if emit_v:
        out_specs.append(pl.BlockSpec((tm, d_c), lambda i: (i, 0)))
        out_shape.append(jax.ShapeDtypeStruct((rows, d_c), F32))
    return pl.pallas_call(
        functools.partial(_gmlp_body, d_c=d_c),
        grid=(rows // tm,),
        in_specs=[pl.BlockSpec((tm, 2 * d_c), lambda i: (t0 + i, 0)),
                  pl.BlockSpec((1, d_c), lambda i: (0, 0)),
                  pl.BlockSpec((1, d_c), lambda i: (0, 0)),
                  pl.BlockSpec((GMLP_GROUPS, tm, tm), lambda i: (0, 0, 0)),
                  pl.BlockSpec((GMLP_GROUPS, tm, gd), lambda i: (0, 0, 0))],
        out_specs=out_specs,
        out_shape=out_shape,
        compiler_params=_cparams(("arbitrary",)),
        name="gmlp_gate",
    )(z, ln_g.reshape(1, d_c), ln_b.reshape(1, d_c), wmix, bmix)


def _store_token_rows(ref2, x2d):
    dc = x2d.shape[1] // LANES
    for c in range(dc):
        ref2[pl.ds(c, x2d.shape[0], stride=dc), :] = x2d[:, c * LANES:(c + 1) * LANES]


def _load_token_rows(ref2, dc):
    n = ref2.shape[0] // dc
    return jnp.concatenate([ref2[pl.ds(c, n, stride=dc), :] for c in range(dc)], axis=1)


def _router_body(h_ref, g_ref, wr_ref, br_ref, xn_o, idx_o, cw_o):
    x = h_ref[...]
    inv = lax.rsqrt(jnp.mean(x * x, axis=-1, keepdims=True) + NORM_EPS)
    xn = (x * inv) * g_ref[...]
    _store_token_rows(xn_o, xn)
    logits = jnp.dot(xn, wr_ref[...], precision=HIGHEST, preferred_element_type=F32) + br_ref[...]
    lane = lax.broadcasted_iota(I32, logits.shape, 1)
    big = jnp.int32(1 << 20)
    ninf = jnp.float32(-jnp.inf)

    def first_max(vals):
        mx = jnp.max(vals, axis=-1, keepdims=True)
        ix = jnp.min(jnp.where(vals == mx, lane, big), axis=-1, keepdims=True)
        return mx, ix

    gl = jnp.where(lane < N_EXP_GROUPS, logits, ninf)
    gmax, gtop = first_max(gl)
    gate = 1.0 / jnp.sum(jnp.exp(gl - gmax), axis=-1, keepdims=True)
    lo = N_EXP_GROUPS + EXP_PER_GROUP * gtop
    el = jnp.where((lane >= lo) & (lane < lo + EXP_PER_GROUP), logits, ninf)
    m1, i1 = first_max(el)
    m2, i2 = first_max(jnp.where(lane == i1, ninf, el))
    p2 = jnp.exp(m2 - m1)
    c1 = gate / (1.0 + p2)
    c2 = gate * p2 / (1.0 + p2)
    idx_o[...] = jnp.where(lane == 0, i1 - N_EXP_GROUPS, jnp.where(lane == 1, i2 - N_EXP_GROUPS, 0))
    cw_o[...] = jnp.where(lane == 0, c1, jnp.where(lane == 1, c2, 0.0))


def _router(h, g, w_grp, b_grp, w_exp, b_exp, tm):
    n_tok, d = h.shape
    ncol = w_grp.shape[1] + w_exp.shape[1]
    wr = jnp.concatenate([w_grp, w_exp, jnp.zeros((d, LANES - ncol), F32)], axis=1)
    br = jnp.concatenate([b_grp, b_exp, jnp.zeros((LANES - ncol,), F32)]).reshape(1, LANES)
    tok = lambda w: pl.BlockSpec((tm, w), lambda i: (i, 0))
    return pl.pallas_call(
        _router_body,
        grid=(n_tok // tm,),
        in_specs=[tok(d), pl.BlockSpec((1, d), lambda i: (0, 0)),
                  pl.BlockSpec((d, LANES), lambda i: (0, 0)),
                  pl.BlockSpec((1, LANES), lambda i: (0, 0))],
        out_specs=[pl.BlockSpec((tm * (d // LANES), LANES), lambda i: (i, 0)), tok(LANES), tok(LANES)],
        out_shape=[jax.ShapeDtypeStruct((n_tok * (d // LANES), LANES), F32),
                   jax.ShapeDtypeStruct((n_tok, LANES), I32),
                   jax.ShapeDtypeStruct((n_tok, LANES), F32)],
        compiler_params=_cparams(("arbitrary",)),
        name="moe_router",
    )(h, g.reshape(1, d), wr, br)


def _gather_copy(x_hbm, xbuf, sem, slot, i, tok, dc):
    src = x_hbm.at[pl.ds(pl.multiple_of(tok * dc, dc), dc)]
    dst = xbuf.at[slot, pl.ds(pl.multiple_of(i * dc, dc), dc)]
    return pltpu.make_async_copy(src, dst, sem.at[slot])


def _moe_ffn_body(te_ref, nvalid_ref, tok_ref, x_hbm, rw_ref, w1_ref, w3_ref, w2_ref, y_ref,
                  xbuf, sem, w1b, w3b, w2b, *, tm):
    t = pl.program_id(0)
    nv = nvalid_ref[0]
    valid = t < nv
    dc = w1_ref.shape[0] // LANES

    def issue(tile, slot):
        def body(i, c):
            _gather_copy(x_hbm, xbuf, sem, slot, i, tok_ref[tile * tm + i], dc).start()
            return c
        lax.fori_loop(0, tm, body, 0, unroll=8)

    @pl.when((t == 0) & valid)
    def _():
        issue(0, 0)

    @pl.when(t + 1 < nv)
    def _():
        issue(t + 1, (t + 1) % 2)

    @pl.when(valid)
    def _():
        prev = te_ref[jnp.maximum(t - 1, 0)]

        @pl.when((t == 0) | (te_ref[t] != prev))
        def _():
            w1b[...] = w1_ref[...].astype(BF16)
            w3b[...] = w3_ref[...].astype(BF16)
            w2b[...] = w2_ref[...].astype(BF16)

        slot = t % 2

        def drain(i, c):
            _gather_copy(x_hbm, xbuf, sem, slot, 0, 0, dc).wait()
            return c
        lax.fori_loop(0, tm, drain, 0, unroll=8)

        x = _load_token_rows(xbuf.at[slot], dc).astype(BF16)
        h1 = jnp.dot(x, w1b[...], preferred_element_type=F32)
        h3 = jnp.dot(x, w3b[...], preferred_element_type=F32)
        hh = (h1 * _sigmoid(h1)) * h3 * rw_ref[...]
        _store_token_rows(y_ref, jnp.dot(hh.astype(BF16), w2b[...], preferred_element_type=F32))

    @pl.when(jnp.logical_not(valid))
    def _():
        y_ref[...] = jnp.zeros_like(y_ref)


def _moe_ffn(xn, row_tok, row_w, tile_exp, nvalid_tiles, w1, w3, w2, layer, n_tiles, tm):
    d, f = w1.shape[2], w1.shape[3]
    dc = d // LANES
    rows = n_tiles * tm
    def last_valid(t, nv):
        return jnp.minimum(t, jnp.maximum(nv[0] - 1, 0))
    return pl.pallas_call(
        functools.partial(_moe_ffn_body, tm=tm),
        grid_spec=pltpu.PrefetchScalarGridSpec(
            num_scalar_prefetch=3,
            grid=(n_tiles,),
            in_specs=[pl.BlockSpec(memory_space=pl.ANY),
                      pl.BlockSpec((tm, 1), lambda t, te, nv, tk: (last_valid(t, nv), 0)),
                      pl.BlockSpec((None, None, d, f), lambda t, te, nv, tk: (layer, te[t], 0, 0)),
                      pl.BlockSpec((None, None, d, f), lambda t, te, nv, tk: (layer, te[t], 0, 0)),
                      pl.BlockSpec((None, None, f, d), lambda t, te, nv, tk: (layer, te[t], 0, 0))],
            out_specs=pl.BlockSpec((tm * dc, LANES), lambda t, te, nv, tk: (t, 0)),
            scratch_shapes=[pltpu.VMEM((2, tm * dc, LANES), F32), pltpu.SemaphoreType.DMA((2,)),
                            pltpu.VMEM((d, f), BF16), pltpu.VMEM((d, f), BF16), pltpu.VMEM((f, d), BF16)]),
        out_shape=jax.ShapeDtypeStruct((rows * dc, LANES), F32),
        compiler_params=_cparams(("arbitrary",)),
        name="moe_ffn",
    )(tile_exp, nvalid_tiles, row_tok, xn, row_w, w1, w3, w2)


def _combine_copy(y_hbm, buf, sem, slot, k, i, row, dc):
    src = y_hbm.at[pl.ds(pl.multiple_of(row * dc, dc), dc)]
    dst = buf.at[slot, k, pl.ds(pl.multiple_of(i * dc, dc), dc)]
    return pltpu.make_async_copy(src, dst, sem.at[slot])


def _moe_combine_body(pos_ref, h_ref, y_hbm, *rest, tm, final_eps):
    if final_eps is None:
        o_ref, buf, sem = rest
    else:
        gf_ref, o_ref, on_ref, buf, sem = rest
    t = pl.program_id(0)
    nt = pl.num_programs(0)
    dc = h_ref.shape[1] // LANES

    def issue(tile, slot):
        def body(i, c):
            a = 2 * (tile * tm + i)
            _combine_copy(y_hbm, buf, sem, slot, 0, i, pos_ref[a], dc).start()
            _combine_copy(y_hbm, buf, sem, slot, 1, i, pos_ref[a + 1], dc).start()
            return c
        lax.fori_loop(0, tm, body, 0, unroll=8)

    @pl.when(t == 0)
    def _():
        issue(0, 0)

    @pl.when(t + 1 < nt)
    def _():
        issue(t + 1, (t + 1) % 2)

    slot = t % 2

    def drain(i, c):
        _combine_copy(y_hbm, buf, sem, slot, 0, 0, 0, dc).wait()
        _combine_copy(y_hbm, buf, sem, slot, 1, 0, 0, dc).wait()
        return c
    lax.fori_loop(0, tm, drain, 0, unroll=8)

    out = h_ref[...] + _load_token_rows(buf.at[slot, 0], dc) + _load_token_rows(buf.at[slot, 1], dc)
    o_ref[...] = out
    if final_eps is not None:
        inv = lax.rsqrt(jnp.mean(out * out, axis=-1, keepdims=True) + final_eps)
        on_ref[...] = (out * inv) * gf_ref[...]


def _moe_combine(h, ys, pos, tm, final_g=None):
    n_tok, d = h.shape
    tok = pl.BlockSpec((tm, d), lambda i, p: (i, 0))
    in_specs = [tok, pl.BlockSpec(memory_space=pl.ANY)]
    args = [pos, h, ys]
    out_specs = [tok]
    out_shape = [jax.ShapeDtypeStruct((n_tok, d), F32)]
    if final_g is not None:
        in_specs.append(pl.BlockSpec((1, d), lambda i, p: (0, 0)))
        args.append(final_g.reshape(1, d))
        out_specs.append(tok)
        out_shape.append(jax.ShapeDtypeStruct((n_tok, d), F32))
    return pl.pallas_call(
        functools.partial(_moe_combine_body, tm=tm, final_eps=None if final_g is None else NORM_EPS),
        grid_spec=pltpu.PrefetchScalarGridSpec(
            num_scalar_prefetch=1,
            grid=(n_tok // tm,),
            in_specs=in_specs,
            out_specs=out_specs,
            scratch_shapes=[pltpu.VMEM((2, 2, tm * (d // LANES), LANES), F32), pltpu.SemaphoreType.DMA((2,))]),
        out_shape=out_shape,
        compiler_params=_cparams(("arbitrary",)),
        name="moe_combine",
    )(*args)


def _hier_moe(h, g, w_grp, b_grp, w_exp, b_exp, w1, w3, w2, layer, final_g=None):
    n_tok, d = h.shape
    n_exp = w1.shape[1]
    tm = MOE_TILE
    xn, idx, cw = _router(h, g, w_grp, b_grp, w_exp, b_exp, tm=256)

    n_asg = 2 * n_tok
    flat_e = idx[:, :2].reshape(n_asg)
    flat_c = cw[:, :2].reshape(n_asg)
    onehot = (flat_e[:, None] == jnp.arange(n_exp, dtype=I32)[None, :]).astype(I32)
    csum = jnp.cumsum(onehot, axis=0)
    counts = csum[-1]
    rank = jnp.take_along_axis(csum - onehot, flat_e[:, None], axis=1)[:, 0]
    padded = ((counts + tm - 1) // tm) * tm
    ends = jnp.cumsum(padded)
    pos = (ends - padded)[flat_e] + rank
    n_tiles = n_asg // tm + n_exp
    rows = n_tiles * tm
    row_tok = jnp.zeros((rows,), I32).at[pos].set(jnp.arange(n_asg, dtype=I32) // 2)
    row_w = jnp.zeros((rows,), F32).at[pos].set(flat_c).reshape(rows, 1)
    nvalid = (ends[-1] // tm).astype(I32).reshape(1)
    tile_ids = jnp.minimum(jnp.arange(n_tiles, dtype=I32), nvalid - 1)
    tile_exp = jnp.searchsorted(ends, tile_ids * tm, side="right").astype(I32)

    ys = _moe_ffn(xn, row_tok, row_w, tile_exp, nvalid, w1, w3, w2, layer, n_tiles, tm)
    return _moe_combine(h, ys, pos.astype(I32), 128, final_g)


def kernel(x_prompt, x_sample, cache_k, cache_v, page_table, state_wkv, state_shift, norm_mix, norm_ffn, norm_final, even_w_in, even_w_out, rwkv_mu, rwkv_w0, rwkv_w2, rwkv_a0, rwkv_a2, rwkv_g2, rwkv_k_k, rwkv_k_a, rwkv_r_k, rwkv_lnx_g, rwkv_lnx_b, diff_lam_q1, diff_lam_k1, diff_lam_q2, diff_lam_k2, diff_subln_g, gmlp_w_in, gmlp_ln_g, gmlp_ln_b, gmlp_w_s, gmlp_b_s, gmlp_w_out, moe_w_grp, moe_b_grp, moe_w_exp, moe_b_exp, moe_w1, moe_w3, moe_w2):
    bp, tp, d = x_prompt.shape
    bs, ts, _ = x_sample.shape
    n_p, n_s = bp * tp, bs * ts
    n_tok = n_p + n_s
    depth = norm_mix.shape[0]
    npages, psz = page_table.shape[1], cache_k.shape[2]
    past_len = npages * psz
    d_a = rwkv_w0.shape[1]
    d_ap = rwkv_mu.shape[1]
    d_q = cache_k.shape[3] * cache_k.shape[4]
    n_pool = cache_k.shape[1]
    nheads = d_a // RWKV_HEAD_DIM
    tm = 512 if n_p % 512 == 0 and n_s % 512 == 0 else 128
    tq = 512 if tp % 512 == 0 else 128
    tc = min(tp, CHUNK)

    h = jnp.concatenate([x_prompt.reshape(n_p, d), x_sample.reshape(n_s, d)], axis=0)
    pos = jnp.concatenate([jnp.tile(jnp.arange(tp), bp), jnp.tile(past_len + jnp.arange(ts), bs)])
    moe = lambda l, hh, fg=None: _hier_moe(hh, norm_ffn[l], moe_w_grp[l], moe_b_grp[l], moe_w_exp[l],
                                          moe_b_exp[l], moe_w1, moe_w3, moe_w2, l, fg)

    outs = {k: [] for k in ("k_p", "v_p", "wkv_p", "sh_p", "k_s", "v_s", "wkv_s", "sh_s", "gv_s")}
    y_norm = None
    for layer in range(depth):
        j = layer // 2
        xn = _rmsnorm(h, norm_mix[layer], NORM_EPS, BF16, tm)
        last = layer == depth - 1
        if layer % 2 == 0:
            proj = _matmul([xn], even_w_in[j], bm=tm, bn=even_w_in.shape[2] // 5, name="even_in_proj")
            prm = dict(mu=rwkv_mu[j].reshape(1, d_ap), w0=rwkv_w0[j].reshape(1, d_a), w2=rwkv_w2[j],
                       a0=rwkv_a0[j].reshape(1, d_a), a2=rwkv_a2[j], g2=rwkv_g2[j],
                       k_k=rwkv_k_k[j].reshape(1, d_a), k_a=rwkv_k_a[j].reshape(1, d_a),
                       r_k=rwkv_r_k[j].reshape(1, d_a))
            tmr = 256 if tp % 256 == 0 else 128
            proj_p = proj[:n_p].reshape(bp, tp, proj.shape[1])
            proj_s = proj[n_p:].reshape(bs, ts, proj.shape[1])
            bnd = jnp.concatenate([jnp.zeros((bp, 1, d_ap), F32), proj_p[:, tmr - 1:tp - 1:tmr, :d_ap]], axis=1)
            repl_p = jnp.zeros((n_p // tmr, 8, d_ap), F32).at[:, 0].set(bnd.reshape(n_p // tmr, d_ap))
            repl_p = repl_p.reshape(n_p // tmr * 8, d_ap)
            repl_s = jnp.repeat(state_shift[j], ts, axis=0)
            pre_p = _rwkv_pre(proj, repl_p, tmr, 0, n_p, tmr, prm)
            pre_s = _rwkv_pre(proj, repl_s, ts, n_p, n_s, tmr, prm)
            oa_p, wkv_p = _rwkv_scan(pre_p, rwkv_lnx_g[j], rwkv_lnx_b[j], None, bp, tp, tc,
                                     2 if bp % 2 == 0 else 1)
            oa_s, wkv_s = _rwkv_scan(pre_s, rwkv_lnx_g[j], rwkv_lnx_b[j], state_wkv[j], bs, ts, ts,
                                     2 if bs % 2 == 0 else 1)

            qs, qf, kf, kb, vb = _rope(proj, pos, d_ap, d_q, tm)
            vf = proj[:, d_ap + 2 * d_q:]
            lam_init = 0.8 - 0.6 * math.exp(-0.3 * layer)
            lam_p = jnp.stack([diff_lam_q1[j], diff_lam_k1[j], diff_lam_q2[j], diff_lam_k2[j]])
            ob_p = _attn_prompt(qs[:n_p], kb[:n_p], vb[:n_p], lam_p, diff_subln_g[j], bp, tp, tq, lam_init)
            ob_s = _attn_sample(qf[n_p:], kf[n_p:], vf[n_p:], cache_k, cache_v, j,
                                page_table.reshape(-1).astype(I32),
                                lam_p, diff_subln_g[j], bs, ts, npages, lam_init)
            oa = jnp.concatenate([oa_p, oa_s.astype(BF16)], axis=0)
            ob = jnp.concatenate([ob_p, ob_s.astype(BF16)], axis=0)
            h = _matmul([oa, ob], even_w_out[j], bm=tm, bn=1024, res=h, name="even_out_proj")

            outs["k_p"].append(kf[:n_p].reshape(bp, tp, -1, DIFF_QK_DIM))
            outs["v_p"].append(vf[:n_p].reshape(bp, tp, -1, DIFF_V_DIM))
            outs["wkv_p"].append(wkv_p)
            outs["sh_p"].append(proj_p[:, -1, :d_ap])
            outs["k_s"].append(kf[n_p:].reshape(bs, ts, -1, DIFF_QK_DIM))
            outs["v_s"].append(vf[n_p:].reshape(bs, ts, -1, DIFF_V_DIM))
            outs["wkv_s"].append(wkv_s)
            outs["sh_s"].append(proj_s[:, -1, :d_ap])
        else:
            d_c = gmlp_ln_g.shape[1]
            z = _matmul([xn], gmlp_w_in[j], bm=tm, bn=1024, act="gelu", name="gmlp_in_proj")
            ws = gmlp_w_s[j]
            bsb = gmlp_b_s[j]
            gd = d_c // GMLP_GROUPS
            lp = min(tp, CHUNK)
            wm_p = jnp.tril(ws[:, :lp, :lp])
            bm_p = jnp.broadcast_to(bsb[:, :lp, None], (GMLP_GROUPS, lp, gd))
            rep = CHUNK // ts
            eye = jnp.eye(rep, dtype=F32)
            wm_s = jnp.einsum("ab,gts->gatbs", eye, jnp.tril(ws[:, :ts, :ts])).reshape(GMLP_GROUPS, CHUNK, CHUNK)
            bm_s = jnp.broadcast_to(jnp.tile(bsb[:, :ts], (1, rep))[:, :, None], (GMLP_GROUPS, CHUNK, gd))
            (y_p,) = _gmlp(z, 0, n_p, gmlp_ln_g[j], gmlp_ln_b[j], wm_p, bm_p, False)
            y_s, v_rows = _gmlp(z, n_p, n_s, gmlp_ln_g[j], gmlp_ln_b[j], wm_s, bm_s, True)
            yin = jnp.concatenate([y_p, y_s], axis=0)
            h = _matmul([yin], gmlp_w_out[j], bm=tm, bn=1024, res=h, name="gmlp_out_proj")
            outs["gv_s"].append(v_rows.reshape(bs, ts, d_c))
        res = moe(layer, h, norm_final if last else None)
        h = res[0]
        if last:
            y_norm = res[1]

    y_prompt = y_norm[:n_p].reshape(bp, tp, d)
    y_sample = y_norm[n_p:].reshape(bs, ts, d)
    st = lambda k: jnp.stack(outs[k])
    return (y_prompt, y_sample, st("k_p"), st("v_p"), st("wkv_p"), st("sh_p"),
            st("k_s"), st("v_s"), st("wkv_s"), st("sh_s"), st("gv_s"))
```

```python
import functools
import math

import jax
import jax.numpy as jnp
from jax import lax
from jax.experimental import pallas as pl
from jax.experimental.pallas import tpu as pltpu

F32 = jnp.float32
BF16 = jnp.bfloat16
I32 = jnp.int32

RWKV_HEAD_DIM = 64
RWKV_GN_EPS = 64e-5
DIFF_QK_DIM = 64
DIFF_V_DIM = 128
ROT_DIM = 16
ROPE_THETA = 500000.0
ATTN_SCALE = DIFF_QK_DIM ** -0.5
NEG_INF = -1e30
NORM_EPS = 1e-6
SUBLN_EPS = 1e-5
GMLP_LN_EPS = 1e-5
GMLP_GROUPS = 8
CHUNK = 128
N_EXP_GROUPS = 4
EXP_PER_GROUP = 8
LORA_W, LORA_A, LORA_G = 64, 64, 128

LANES = 128
MXU_TILE = 256
VMEM_LIMIT = 56 * 1024 * 1024
MOE_TILE = 256
HIGHEST = lax.Precision.HIGHEST


def _cparams(sem):
    return pltpu.CompilerParams(dimension_semantics=sem, vmem_limit_bytes=VMEM_LIMIT)


def _rmsnorm_body(x_ref, g_ref, o_ref, *, eps):
    x = x_ref[...]
    inv = lax.rsqrt(jnp.mean(x * x, axis=-1, keepdims=True) + eps)
    o_ref[...] = ((x * inv) * g_ref[...]).astype(o_ref.dtype)


def _rmsnorm(x, g, eps, out_dtype, tm):
    m, d = x.shape
    return pl.pallas_call(
        functools.partial(_rmsnorm_body, eps=eps),
        grid=(m // tm,),
        in_specs=[pl.BlockSpec((tm, d), lambda i: (i, 0)),
                  pl.BlockSpec((1, d), lambda i: (0, 0))],
        out_specs=pl.BlockSpec((tm, d), lambda i: (i, 0)),
        out_shape=jax.ShapeDtypeStruct((m, d), out_dtype),
        compiler_params=_cparams(("arbitrary",)),
        name="rmsnorm",
    )(x, g.reshape(1, d))


def _gelu_exact(x):
    return 0.5 * x * (1.0 + lax.erf(x * (1.0 / math.sqrt(2.0))))


def _mm_body(*refs, n_x, act, has_res):
    xs = refs[:n_x]
    w_ref = refs[n_x]
    res_ref = refs[n_x + 1] if has_res else None
    o_ref = refs[n_x + 1 + has_res]
    wb_ref = refs[n_x + 2 + has_res]

    @pl.when(pl.program_id(1) == 0)
    def _():
        wb_ref[...] = w_ref[...].astype(BF16)

    kx = w_ref.shape[0] // n_x
    acc = None
    for i, x_ref in enumerate(xs):
        part = jnp.dot(x_ref[...], wb_ref[i * kx:(i + 1) * kx, :], preferred_element_type=F32)
        acc = part if acc is None else acc + part
    if act == "gelu":
        acc = _gelu_exact(acc)
    if has_res:
        acc = acc + res_ref[...]
    o_ref[...] = acc.astype(o_ref.dtype)


def _matmul(xs, w, *, bm, bn, act=None, res=None, out_dtype=F32, name="matmul"):
    m = xs[0].shape[0]
    k, n = w.shape
    kx = k // len(xs)
    in_specs = [pl.BlockSpec((bm, kx), lambda j, i: (i, 0)) for _ in xs]
    in_specs.append(pl.BlockSpec((k, bn), lambda j, i: (0, j)))
    args = list(xs) + [w]
    if res is not None:
        in_specs.append(pl.BlockSpec((bm, bn), lambda j, i: (i, j)))
        args.append(res)
    return pl.pallas_call(
        functools.partial(_mm_body, n_x=len(xs), act=act, has_res=res is not None),
        grid=(n // bn, m // bm),
        in_specs=in_specs,
        out_specs=pl.BlockSpec((bm, bn), lambda j, i: (i, j)),
        out_shape=jax.ShapeDtypeStruct((m, n), out_dtype),
        scratch_shapes=[pltpu.VMEM((k, bn), BF16)],
        compiler_params=_cparams(("arbitrary", "arbitrary")),
        name=name,
    )(*args)


def _block_ones(n, seg):
    r = lax.broadcasted_iota(I32, (n, n), 0) // seg
    c = lax.broadcasted_iota(I32, (n, n), 1) // seg
    return jnp.where(r == c, 1.0, 0.0).astype(BF16)


def _seg_sum(x, ones_bd):
    hi = x.astype(BF16)
    lo = (x - hi.astype(F32)).astype(BF16)
    outs = []
    for c in range(x.shape[1] // MXU_TILE):
        sl = slice(c * MXU_TILE, (c + 1) * MXU_TILE)
        outs.append(jnp.dot(hi[:, sl], ones_bd, preferred_element_type=F32)
                    + jnp.dot(lo[:, sl], ones_bd, preferred_element_type=F32))
    return jnp.concatenate(outs, axis=1)


def _sigmoid(x):
    return 1.0 / (1.0 + jnp.exp(-x))


def _rwkv_pre_body(pa_ref, repl_ref, mu_ref, w0_ref, w2_ref, a0_ref, a2_ref, g2_ref,
                   kk_ref, ka_ref, rk_ref,
                   r_o, dec_o, k_o, v_o, kkn_o, b_o, g_o, bonus_o, *, period, d_a):
    pa = pa_ref[...]
    tm = pa.shape[0]
    row = lax.broadcasted_iota(I32, pa.shape, 0)
    prev = pltpu.roll(pa, 1, 0)
    repl = repl_ref[...]
    if repl.shape[0] != tm:
        repl = jnp.broadcast_to(repl[0:1, :], pa.shape)
    prev = jnp.where(row % period == 0, repl, prev)
    xm = pa + (prev - pa) * mu_ref[...]
    r = xm[:, 0:d_a]
    k = xm[:, d_a:2 * d_a]
    v = xm[:, 2 * d_a:3 * d_a]
    o = 3 * d_a
    wd = xm[:, o:o + LORA_W]
    ad = xm[:, o + LORA_W:o + LORA_W + LORA_A]
    gd = xm[:, o + LORA_W + LORA_A:o + LORA_W + LORA_A + LORA_G]

    z = w0_ref[...] + jnp.dot(jnp.tanh(wd), w2_ref[...], precision=HIGHEST, preferred_element_type=F32)
    w = jnp.minimum(z, 0.0) - jnp.log1p(jnp.exp(-jnp.abs(z))) - 0.5
    dec = jnp.exp(-jnp.exp(w))
    a = _sigmoid(a0_ref[...] + jnp.dot(ad, a2_ref[...], precision=HIGHEST, preferred_element_type=F32))
    g = jnp.dot(_sigmoid(gd).astype(BF16), g2_ref[...].astype(BF16), preferred_element_type=F32)

    ones_bd = _block_ones(MXU_TILE, RWKV_HEAD_DIM)
    kk = k * kk_ref[...]
    nrm = jnp.sqrt(_seg_sum(kk * kk, ones_bd))
    kkn = kk / jnp.maximum(nrm, 1e-12)
    kh = k * (1.0 + (a - 1.0) * ka_ref[...])
    bonus = _seg_sum(r * kh * rk_ref[...], ones_bd) * v

    r_o[...] = r
    dec_o[...] = dec
    k_o[...] = kh
    v_o[...] = v
    kkn_o[...] = kkn
    b_o[...] = kkn * a
    g_o[...] = g
    bonus_o[...] = bonus


def _rwkv_pre(proj, repl, period, row0, rows, tm, prm):
    d_a = prm["w0"].shape[1]
    d_ap = prm["mu"].shape[1]
    nt = rows // tm
    t0 = row0 // tm
    vec = lambda n: pl.BlockSpec((1, n), lambda i: (0, 0))
    full = lambda a: pl.BlockSpec(a.shape, lambda i: (0, 0))
    out_spec = pl.BlockSpec((tm, d_a), lambda i: (i, 0))
    outs = pl.pallas_call(
        functools.partial(_rwkv_pre_body, period=period, d_a=d_a),
        grid=(nt,),
        in_specs=[pl.BlockSpec((tm, d_ap), lambda i: (t0 + i, 0)),
                  pl.BlockSpec((repl.shape[0] // nt, d_ap), lambda i: (i, 0)),
                  vec(d_ap), vec(d_a), full(prm["w2"]), vec(d_a), full(prm["a2"]), full(prm["g2"]),
                  vec(d_a), vec(d_a), vec(d_a)],
        out_specs=[out_spec] * 8,
        out_shape=[jax.ShapeDtypeStruct((rows, d_a), F32)] * 8,
        compiler_params=_cparams(("arbitrary",)),
        name="rwkv_pre",
    )(proj, repl, prm["mu"], prm["w0"], prm["w2"], prm["a0"], prm["a2"], prm["g2"],
      prm["k_k"], prm["k_a"], prm["r_k"])
    return outs


def _rwkv_scan_body(*refs, tc, tp, nheads, nb, has_state):
    if has_state:
        s0_ref = refs[0]
        refs = refs[1:]
    (r_ref, w_ref, k_ref, v_ref, kk_ref, b_ref, g_ref, bonus_ref, lg_ref, lb_ref,
     o_ref, sout_ref, s_scr, vth_scr, ot_scr) = refs
    n = RWKV_HEAD_DIM
    c = pl.program_id(1)
    assert tp == 2 * n
    nh = nb * nheads
    heads = [(b, h) for b in range(nb) for h in range(nheads)]

    @pl.when(c == 0)
    def _():
        if has_state:
            zero = jnp.zeros((n, n), F32)
            for i, (b, h) in enumerate(heads):
                s0 = s0_ref[b, h]
                s_scr[i] = jnp.concatenate([s0, zero] if h % 2 == 0 else [zero, s0], axis=1)
        else:
            s_scr[...] = jnp.zeros_like(s_scr)

    for i, (b, h) in enumerate(heads):
        vh = v_ref[b, :, h * n:(h + 1) * n]
        if tp != tc:
            vh = jnp.concatenate([vh, jnp.zeros((tp - tc, n), F32)], axis=0)
        vth_scr[i * n:(i + 1) * n, :] = vh.T.astype(BF16)
    ot_scr[...] = jnp.zeros_like(ot_scr)

    trow = lax.broadcasted_iota(I32, (tp, 2 * n), 0)
    tcol = lax.broadcasted_iota(I32, (2 * n, tp), 1)
    head_par = lax.broadcasted_iota(I32, (nh, 1, 2 * n), 0) % 2
    lane_par = lax.broadcasted_iota(I32, (nh, 1, 2 * n), 2) // n
    own = head_par == lane_par

    sub = 8

    def per_head(ref, base, masked):
        xs = [ref[b, pl.ds(base, sub), :] for b in range(nb)]
        x = jnp.stack([xs[b][:, (h // 2) * 2 * n:(h // 2 + 1) * 2 * n] for b, h in heads], axis=0)
        return jnp.where(own, x, 0.0) if masked else x

    def group(tg, carry):
        base = pl.multiple_of(tg * sub, sub)
        kk8 = per_head(kk_ref, base, True)
        b8 = per_head(b_ref, base, True)
        k8 = per_head(k_ref, base, True)
        w8 = per_head(w_ref, base, False)
        r8 = per_head(r_ref, base, False)
        st = s_scr[...]
        vth = vth_scr[...]
        o8 = None
        for s in range(sub):
            t = base + s
            pick = jnp.where(trow == t, 1.0, 0.0).astype(BF16)
            v_col = jnp.dot(vth, pick, preferred_element_type=F32).reshape(nh, n, 2 * n)
            sa = -jnp.sum(st * kk8[:, s:s + 1], axis=-1, keepdims=True)
            st = st * w8[:, s:s + 1] + sa * b8[:, s:s + 1] + v_col * k8[:, s:s + 1]
            q = (st * r8[:, s:s + 1]).reshape(nh * n, 2 * n).astype(BF16)
            put = jnp.where(tcol == t, 1.0, 0.0).astype(BF16)
            o_t = jnp.dot(q, put, preferred_element_type=F32)
            o8 = o_t if o8 is None else o8 + o_t
        s_scr[...] = st
        ot_scr[...] = ot_scr[...] + o8.reshape(nh, n, tp)
        return carry

    lax.fori_loop(0, tc // sub, group, 0)

    for b in range(nb):
        for hp in range(nheads // 2):
            parts = []
            for h in (2 * hp, 2 * hp + 1):
                oc = ot_scr[b * nheads + h]
                mean = jnp.mean(oc, axis=0, keepdims=True)
                d = oc - mean
                var = jnp.mean(d * d, axis=0, keepdims=True)
                on = d * lax.rsqrt(var + RWKV_GN_EPS)
                parts.append(on.T[:tc])
            sl = slice(hp * 2 * n, (hp + 1) * 2 * n)
            on2 = jnp.concatenate(parts, axis=1)
            o = (on2 * lg_ref[:, sl] + lb_ref[:, sl] + bonus_ref[b, :, sl]) * g_ref[b, :, sl]
            o_ref[b, :, sl] = o.astype(o_ref.dtype)

    @pl.when(c == pl.num_programs(1) - 1)
    def _():
        for i, (b, h) in enumerate(heads):
            s_h = s_scr[i]
            sout_ref[b, h] = s_h[:, 0:n] if h % 2 == 0 else s_h[:, n:2 * n]


def _rwkv_scan(pre, lnx_g, lnx_b, state, nseq, tseq, tc, nb):
    rows, d_a = pre[0].shape
    nheads = d_a // RWKV_HEAD_DIM
    nch = tseq // tc
    tp = max(tc, LANES)
    n = RWKV_HEAD_DIM
    nh = nb * nheads
    tok = pl.BlockSpec((nb, tc, d_a), lambda s, c: (s, c, 0))
    vec = pl.BlockSpec((1, d_a), lambda s, c: (0, 0))
    st = pl.BlockSpec((nb, nheads, n, n), lambda s, c: (s, 0, 0, 0))
    has_state = state is not None
    in_specs = ([st] if has_state else []) + [tok] * 8 + [vec, vec]
    args = ([state] if has_state else []) + [a.reshape(nseq, tseq, d_a) for a in pre] + [
        lnx_g.reshape(1, d_a), lnx_b.reshape(1, d_a)]
    oa, s_fin = pl.pallas_call(
        functools.partial(_rwkv_scan_body, tc=tc, tp=tp, nheads=nheads, nb=nb, has_state=has_state),
        grid=(nseq // nb, nch),
        in_specs=in_specs,
        out_specs=[tok, st],
        out_shape=[jax.ShapeDtypeStruct((nseq, tseq, d_a), BF16 if tc % 16 == 0 else F32),
                   jax.ShapeDtypeStruct((nseq, nheads, n, n), F32)],
        scratch_shapes=[pltpu.VMEM((nh, n, 2 * n), F32),
                        pltpu.VMEM((nh * n, tp), BF16),
                        pltpu.VMEM((nh, n, tp), F32)],
        compiler_params=_cparams(("arbitrary", "arbitrary")),
        name="rwkv_scan",
    )(*args)
    return oa.reshape(rows, d_a), s_fin


def _rope_body(q_ref, k_ref, v_ref, c_ref, s1_ref, s2_ref, qs_o, qf_o, kf_o, kb_o, vb_o):
    c, s1, s2 = c_ref[...], s1_ref[...], s2_ref[...]
    half = ROT_DIM // 2
    w = q_ref.shape[1]

    def rot(x):
        return x * c + pltpu.roll(x, w - half, 1) * s1 + pltpu.roll(x, half, 1) * s2

    qr = rot(q_ref[...]) * ATTN_SCALE
    qf_o[...] = qr
    qs_o[...] = qr.astype(BF16)
    kr = rot(k_ref[...])
    kf_o[...] = kr
    kb_o[...] = kr.astype(BF16)
    vb_o[...] = v_ref[...].astype(BF16)


def _rope(proj, pos, d_ap, d_q, tm):
    n_tok = proj.shape[0]
    w = MXU_TILE
    half = ROT_DIM // 2
    inv_freq = ROPE_THETA ** (-jnp.arange(half, dtype=F32) / half)
    ang = pos.astype(F32)[:, None] * inv_freq[None, :]
    cos, sin = jnp.cos(ang), jnp.sin(ang)
    pad = DIFF_QK_DIM - ROT_DIM
    ones = jnp.ones((n_tok, pad), F32)
    zeros = jnp.zeros((n_tok, pad), F32)
    zh = jnp.zeros((n_tok, half), F32)
    reps = w // DIFF_QK_DIM
    c_t = jnp.tile(jnp.concatenate([cos, cos, ones], axis=1), (1, reps))
    s1_t = jnp.tile(jnp.concatenate([-sin, zh, zeros], axis=1), (1, reps))
    s2_t = jnp.tile(jnp.concatenate([zh, sin, zeros], axis=1), (1, reps))
    nq = d_q // w
    qb, kb, vb = d_ap // w, d_ap // w + nq, d_ap // w + 2 * nq
    tab = pl.BlockSpec((tm, w), lambda i, j: (i, 0))
    out = pl.BlockSpec((tm, w), lambda i, j: (i, j))
    return pl.pallas_call(
        _rope_body,
        grid=(n_tok // tm, nq),
        in_specs=[pl.BlockSpec((tm, w), lambda i, j: (i, qb + j)),
                  pl.BlockSpec((tm, w), lambda i, j: (i, kb + j)),
                  pl.BlockSpec((tm, w), lambda i, j: (i, vb + j)),
                  tab, tab, tab],
        out_specs=[out, out, out, out, out],
        out_shape=[jax.ShapeDtypeStruct((n_tok, d_q), BF16),
                   jax.ShapeDtypeStruct((n_tok, d_q), F32),
                   jax.ShapeDtypeStruct((n_tok, d_q), F32),
                   jax.ShapeDtypeStruct((n_tok, d_q), BF16),
                   jax.ShapeDtypeStruct((n_tok, d_q), BF16)],
        compiler_params=_cparams(("arbitrary", "arbitrary")),
        name="rope",
    )(proj, proj, proj, c_t, s1_t, s2_t)


def _lambda(lam_ref, lam_init):
    lp = lam_ref[...]
    l1 = jnp.sum(lp[0:1] * lp[1:2], axis=-1, keepdims=True)
    l2 = jnp.sum(lp[2:3] * lp[3:4], axis=-1, keepdims=True)
    return jnp.exp(l1) - jnp.exp(l2) + lam_init


def _subln(o, g_ref, lam_init):
    inv = lax.rsqrt(jnp.mean(o * o, axis=-1, keepdims=True) + SUBLN_EPS)
    return (o * inv) * g_ref[...] * (1.0 - lam_init)


def _online_softmax_update(s, v, m_scr, l_scr, acc_scr):
    m_prev = m_scr[...]
    m_new = jnp.maximum(m_prev, jnp.max(s, axis=-1, keepdims=True))
    alpha = jnp.exp(m_prev - m_new)
    p = jnp.exp(s - m_new[:, 0:1])
    l_scr[...] = alpha * l_scr[...] + jnp.sum(p, axis=-1, keepdims=True)
    acc_scr[...] = alpha[:, 0:1] * acc_scr[...] + jnp.dot(p.astype(BF16), v, preferred_element_type=F32)
    m_scr[...] = m_new


def _attn_prompt_body(q_ref, k_ref, v_ref, lam_ref, g_ref, o_ref, qq_scr, m_scr, l_scr, acc_scr,
                      *, tq, lam_init):
    qi = pl.program_id(2)
    ki = pl.program_id(3)

    @pl.when(ki == 0)
    def _():
        q = q_ref[...]
        lane = lax.broadcasted_iota(I32, q.shape, 1)
        zero = jnp.zeros_like(q)
        qq_scr[0:tq] = jnp.where(lane < DIFF_QK_DIM, q, zero)
        qq_scr[tq:2 * tq] = jnp.where(lane >= DIFF_QK_DIM, q, zero)
        m_scr[...] = jnp.full_like(m_scr, NEG_INF)
        l_scr[...] = jnp.zeros_like(l_scr)
        acc_scr[...] = jnp.zeros_like(acc_scr)

    rb = min(tq, 256)

    def tile(diagonal):
        for r0 in range(0, 2 * tq, rb):
            rows = slice(r0, r0 + rb)
            nk = min(tq, (r0 % tq) + rb) if diagonal else tq
            s = lax.dot_general(qq_scr[rows], k_ref[0:nk], (((1,), (1,)), ((), ())),
                                preferred_element_type=F32)
            if diagonal:
                row = (r0 % tq) + lax.broadcasted_iota(I32, s.shape, 0)
                col = lax.broadcasted_iota(I32, s.shape, 1)
                s = jnp.where(row >= col, s, NEG_INF)
            m_prev = m_scr[rows]
            m_new = jnp.maximum(m_prev, jnp.max(s, axis=-1, keepdims=True))
            alpha = jnp.exp(m_prev - m_new)
            p = jnp.exp(s - jnp.concatenate([m_new] * (nk // LANES), axis=1))
            l_scr[rows] = alpha * l_scr[rows] + jnp.sum(p, axis=-1, keepdims=True)
            acc_scr[rows] = alpha * acc_scr[rows] + jnp.dot(p.astype(BF16), v_ref[0:nk],
                                                            preferred_element_type=F32)
            m_scr[rows] = m_new

    @pl.when(ki < qi)
    def _():
        tile(False)

    @pl.when(ki == qi)
    def _():
        tile(True)
        lam = _lambda(lam_ref, lam_init)
        o1 = acc_scr[0:tq] / l_scr[0:tq, 0:1]
        o2 = acc_scr[tq:2 * tq] / l_scr[tq:2 * tq, 0:1]
        o_ref[...] = _subln(o1 - lam * o2, g_ref, lam_init).astype(o_ref.dtype)


def _attn_prompt(qs, kb, vb, lam_p, subln_g, nseq, tseq, tq, lam_init):
    rows, d_q = qs.shape
    nh = d_q // DIFF_V_DIM
    nq = tseq // tq
    w = DIFF_V_DIM
    return pl.pallas_call(
        functools.partial(_attn_prompt_body, tq=tq, lam_init=lam_init),
        grid=(nseq, nh, nq, nq),
        in_specs=[pl.BlockSpec((tq, w), lambda b, h, i, j: (b * nq + i, h)),
                  pl.BlockSpec((tq, w), lambda b, h, i, j: (b * nq + jnp.minimum(i, j), h)),
                  pl.BlockSpec((tq, w), lambda b, h, i, j: (b * nq + jnp.minimum(i, j), h)),
                  pl.BlockSpec(lam_p.shape, lambda b, h, i, j: (0, 0)),
                  pl.BlockSpec((1, w), lambda b, h, i, j: (0, 0))],
        out_specs=pl.BlockSpec((tq, w), lambda b, h, i, j: (b * nq + i, h)),
        out_shape=jax.ShapeDtypeStruct((nseq * tseq, d_q), BF16),
        scratch_shapes=[pltpu.VMEM((2 * tq, w), BF16),
                        pltpu.VMEM((2 * tq, LANES), F32),
                        pltpu.VMEM((2 * tq, LANES), F32),
                        pltpu.VMEM((2 * tq, w), F32)],
        compiler_params=_cparams(("arbitrary", "arbitrary", "arbitrary", "arbitrary")),
        name="attn_prompt",
    )(qs, kb, vb, lam_p, subln_g.reshape(1, w))


def _attn_sample_body(pt_ref, q_ref, *refs, ts, lam_init, pages_per_step):
    kc_refs = refs[:pages_per_step]
    vc_refs = refs[pages_per_step:2 * pages_per_step]
    kn_ref, vn_ref, lam_ref, g_ref, o_ref, qbd_scr, m_scr, l_scr, acc_scr = refs[2 * pages_per_step:]
    p = pl.program_id(1)
    nheads_qk = q_ref.shape[1] // DIFF_QK_DIM

    @pl.when(p == 0)
    def _():
        q = q_ref[...]
        head = lax.broadcasted_iota(I32, q.shape, 1) // DIFF_QK_DIM
        zero = jnp.zeros_like(q)
        rows = [jnp.where(head == m, q, zero) for m in range(nheads_qk)]
        qbd_scr[...] = jnp.concatenate(rows, axis=0).astype(BF16)
        m_scr[...] = jnp.full_like(m_scr, NEG_INF)
        l_scr[...] = jnp.zeros_like(l_scr)
        acc_scr[...] = jnp.zeros_like(acc_scr)

    nh_v = nheads_qk // 2
    w = DIFF_V_DIM

    def update(s, v_heads):
        m_prev = m_scr[...]
        m_new = jnp.maximum(m_prev, jnp.max(s, axis=-1, keepdims=True))
        alpha = jnp.exp(m_prev - m_new)
        pr = jnp.exp(s - jnp.concatenate([m_new] * (s.shape[1] // LANES), axis=1))
        l_scr[...] = alpha * l_scr[...] + jnp.sum(pr, axis=-1, keepdims=True)
        m_scr[...] = m_new
        pb = pr.astype(BF16)
        for h in range(nh_v):
            rows = slice(2 * h * ts, (2 * h + 2) * ts)
            acc_scr[rows] = alpha[rows] * acc_scr[rows] + jnp.dot(pb[rows], v_heads[h],
                                                                  preferred_element_type=F32)

    kt = jnp.concatenate([r[...].reshape(nheads_qk * DIFF_QK_DIM, r.shape[2]) for r in kc_refs],
                         axis=1).astype(BF16)
    s_past = jnp.dot(qbd_scr[...], kt, preferred_element_type=F32)
    npos = vc_refs[0].shape[0] // nh_v
    v_past = [jnp.concatenate([r[pl.ds(h, npos, stride=nh_v), :] for r in vc_refs], axis=0).astype(BF16)
              for h in range(nh_v)]
    update(s_past, v_past)

    @pl.when(p == pl.num_programs(1) - 1)
    def _():
        padk = jnp.zeros((LANES - ts, kn_ref.shape[1]), F32)
        kn = jnp.concatenate([kn_ref[...], padk], axis=0).astype(BF16)
        vn = jnp.concatenate([vn_ref[...], padk], axis=0).astype(BF16)
        s = lax.dot_general(qbd_scr[...], kn, (((1,), (1,)), ((), ())), preferred_element_type=F32)
        row = lax.broadcasted_iota(I32, s.shape, 0) % ts
        col = lax.broadcasted_iota(I32, s.shape, 1)
        s = jnp.where(row >= col, s, NEG_INF)
        update(s, [vn[:, h * w:(h + 1) * w] for h in range(nh_v)])
        lam = _lambda(lam_ref, lam_init)
        for h in range(nh_v):
            r1 = slice(2 * h * ts, (2 * h + 1) * ts)
            r2 = slice((2 * h + 1) * ts, (2 * h + 2) * ts)
            o1 = acc_scr[r1] / l_scr[r1, 0:1]
            o2 = acc_scr[r2] / l_scr[r2, 0:1]
            o_ref[:, h * w:(h + 1) * w] = _subln(o1 - lam * o2, g_ref, lam_init).astype(o_ref.dtype)


def _attn_sample(qs, kb, vb, cache_k, cache_v, layer_j, page_flat, lam_p, subln_g, nb, ts, npages, lam_init):
    d_q = qs.shape[1]
    nrow = (d_q // DIFF_QK_DIM) * ts
    gpp = 4 if npages % 4 == 0 else (2 if npages % 2 == 0 else 1)
    tok = pl.BlockSpec((ts, d_q), lambda b, p, pt: (b, 0))
    cache_k = jnp.transpose(cache_k, (0, 1, 3, 4, 2))
    cache_v = cache_v.reshape(cache_v.shape[:2] + (-1, cache_v.shape[-1]))

    def page(cache, g):
        return pl.BlockSpec((None, None) + cache.shape[2:],
                            lambda b, p, pt: (layer_j, pt[b * npages + p * gpp + g]) + (0,) * (cache.ndim - 2))

    return pl.pallas_call(
        functools.partial(_attn_sample_body, ts=ts, lam_init=lam_init, pages_per_step=gpp),
        grid_spec=pltpu.PrefetchScalarGridSpec(
            num_scalar_prefetch=1,
            grid=(nb, npages // gpp),
            in_specs=[tok] + [page(cache_k, g) for g in range(gpp)] + [page(cache_v, g) for g in range(gpp)]
                     + [tok, tok,
                        pl.BlockSpec(lam_p.shape, lambda b, p, pt: (0, 0)),
                        pl.BlockSpec((1, DIFF_V_DIM), lambda b, p, pt: (0, 0))],
            out_specs=tok,
            scratch_shapes=[pltpu.VMEM((nrow, d_q), BF16),
                            pltpu.VMEM((nrow, LANES), F32),
                            pltpu.VMEM((nrow, LANES), F32),
                            pltpu.VMEM((nrow, DIFF_V_DIM), F32)]),
        out_shape=jax.ShapeDtypeStruct((nb * ts, d_q), F32),
        compiler_params=_cparams(("arbitrary", "arbitrary")),
        name="attn_sample",
    )(page_flat, qs, *([cache_k] * gpp), *([cache_v] * gpp), kb, vb, lam_p, subln_g.reshape(1, DIFF_V_DIM))


def _gmlp_body(z_ref, lg_ref, lb_ref, wm_ref, bm_ref, y_ref, *maybe_v, d_c):
    u = z_ref[:, 0:d_c]
    v = z_ref[:, d_c:2 * d_c]
    mean = jnp.mean(v, axis=-1, keepdims=True)
    d = v - mean
    var = jnp.mean(d * d, axis=-1, keepdims=True)
    vn = (d * lax.rsqrt(var + GMLP_LN_EPS)) * lg_ref[...] + lb_ref[...]
    if maybe_v:
        maybe_v[0][...] = vn
    gd = d_c // GMLP_GROUPS
    vb = vn.astype(BF16)
    for g in range(GMLP_GROUPS):
        sl = slice(g * gd, (g + 1) * gd)
        mixed = jnp.dot(wm_ref[g].astype(BF16), vb[:, sl], preferred_element_type=F32) + bm_ref[g]
        y_ref[:, sl] = (u[:, sl] * mixed).astype(y_ref.dtype)


def _gmlp(z, row0, rows, ln_g, ln_b, wmix, bmix, emit_v):
    d_c = z.shape[1] // 2
    tm = CHUNK
    t0 = row0 // tm
    gd = d_c // GMLP_GROUPS
    out_specs = [pl.BlockSpec((tm, d_c), lambda i: (i, 0))]
    out_shape = [jax.ShapeDtypeStruct((rows, d_c), BF16)]
    if emit_v:
        out_specs.append(pl.BlockSpec((tm, d_c), lambda i: (i, 0)))
        out_shape.append(jax.ShapeDtypeStruct((rows, d_c), F32))
    return pl.pallas_call(
        functools.partial(_gmlp_body, d_c=d_c),
        grid=(rows // tm,),
        in_specs=[pl.BlockSpec((tm, 2 * d_c), lambda i: (t0 + i, 0)),
                  pl.BlockSpec((1, d_c), lambda i: (0, 0)),
                  pl.BlockSpec((1, d_c), lambda i: (0, 0)),
                  pl.BlockSpec((GMLP_GROUPS, tm, tm), lambda i: (0, 0, 0)),
                  pl.BlockSpec((GMLP_GROUPS, tm, gd), lambda i: (0, 0, 0))],
        out_specs=out_specs,
        out_shape=out_shape,
        compiler_params=_cparams(("arbitrary",)),
        name="gmlp_gate",
    )(z, ln_g.reshape(1, d_c), ln_b.reshape(1, d_c), wmix, bmix)


def _store_token_rows(ref2, x2d):
    dc = x2d.shape[1] // LANES
    for c in range(dc):
        ref2[pl.ds(c, x2d.shape[0], stride=dc), :] = x2d[:, c * LANES:(c + 1) * LANES]


def _load_token_rows(ref2, dc):
    n = ref2.shape[0] // dc
    return jnp.concatenate([ref2[pl.ds(c, n, stride=dc), :] for c in range(dc)], axis=1)


def _router_body(h_ref, g_ref, wr_ref, br_ref, xn_o, idx_o, cw_o):
    x = h_ref[...]
    inv = lax.rsqrt(jnp.mean(x * x, axis=-1, keepdims=True) + NORM_EPS)
    xn = (x * inv) * g_ref[...]
    _store_token_rows(xn_o, xn)
    logits = jnp.dot(xn, wr_ref[...], precision=HIGHEST, preferred_element_type=F32) + br_ref[...]
    lane = lax.broadcasted_iota(I32, logits.shape, 1)
    big = jnp.int32(1 << 20)
    ninf = jnp.float32(-jnp.inf)

    def first_max(vals):
        mx = jnp.max(vals, axis=-1, keepdims=True)
        ix = jnp.min(jnp.where(vals == mx, lane, big), axis=-1, keepdims=True)
        return mx, ix

    gl = jnp.where(lane < N_EXP_GROUPS, logits, ninf)
    gmax, gtop = first_max(gl)
    gate = 1.0 / jnp.sum(jnp.exp(gl - gmax), axis=-1, keepdims=True)
    lo = N_EXP_GROUPS + EXP_PER_GROUP * gtop
    el = jnp.where((lane >= lo) & (lane < lo + EXP_PER_GROUP), logits, ninf)
    m1, i1 = first_max(el)
    m2, i2 = first_max(jnp.where(lane == i1, ninf, el))
    p2 = jnp.exp(m2 - m1)
    c1 = gate / (1.0 + p2)
    c2 = gate * p2 / (1.0 + p2)
    idx_o[...] = jnp.where(lane == 0, i1 - N_EXP_GROUPS, jnp.where(lane == 1, i2 - N_EXP_GROUPS, 0))
    cw_o[...] = jnp.where(lane == 0, c1, jnp.where(lane == 1, c2, 0.0))


def _router(h, g, w_grp, b_grp, w_exp, b_exp, tm):
    n_tok, d = h.shape
    ncol = w_grp.shape[1] + w_exp.shape[1]
    wr = jnp.concatenate([w_grp, w_exp, jnp.zeros((d, LANES - ncol), F32)], axis=1)
    br = jnp.concatenate([b_grp, b_exp, jnp.zeros((LANES - ncol,), F32)]).reshape(1, LANES)
    tok = lambda w: pl.BlockSpec((tm, w), lambda i: (i, 0))
    return pl.pallas_call(
        _router_body,
        grid=(n_tok // tm,),
        in_specs=[tok(d), pl.BlockSpec((1, d), lambda i: (0, 0)),
                  pl.BlockSpec((d, LANES), lambda i: (0, 0)),
                  pl.BlockSpec((1, LANES), lambda i: (0, 0))],
        out_specs=[pl.BlockSpec((tm * (d // LANES), LANES), lambda i: (i, 0)), tok(LANES), tok(LANES)],
        out_shape=[jax.ShapeDtypeStruct((n_tok * (d // LANES), LANES), F32),
                   jax.ShapeDtypeStruct((n_tok, LANES), I32),
                   jax.ShapeDtypeStruct((n_tok, LANES), F32)],
        compiler_params=_cparams(("arbitrary",)),
        name="moe_router",
    )(h, g.reshape(1, d), wr, br)


def _gather_copy(x_hbm, xbuf, sem, slot, i, tok, dc):
    src = x_hbm.at[pl.ds(pl.multiple_of(tok * dc, dc), dc)]
    dst = xbuf.at[slot, pl.ds(pl.multiple_of(i * dc, dc), dc)]
    return pltpu.make_async_copy(src, dst, sem.at[slot])


def _moe_ffn_body(te_ref, nvalid_ref, tok_ref, x_hbm, rw_ref, w1_ref, w3_ref, w2_ref, y_ref,
                  xbuf, sem, w1b, w3b, w2b, *, tm):
    t = pl.program_id(0)
    nv = nvalid_ref[0]
    valid = t < nv
    dc = w1_ref.shape[0] // LANES

    def issue(tile, slot):
        def body(i, c):
            _gather_copy(x_hbm, xbuf, sem, slot, i, tok_ref[tile * tm + i], dc).start()
            return c
        lax.fori_loop(0, tm, body, 0, unroll=8)

    @pl.when((t == 0) & valid)
    def _():
        issue(0, 0)

    @pl.when(t + 1 < nv)
    def _():
        issue(t + 1, (t + 1) % 2)

    @pl.when(valid)
    def _():
        prev = te_ref[jnp.maximum(t - 1, 0)]

        @pl.when((t == 0) | (te_ref[t] != prev))
        def _():
            w1b[...] = w1_ref[...].astype(BF16)
            w3b[...] = w3_ref[...].astype(BF16)
            w2b[...] = w2_ref[...].astype(BF16)

        slot = t % 2

        def drain(i, c):
            _gather_copy(x_hbm, xbuf, sem, slot, 0, 0, dc).wait()
            return c
        lax.fori_loop(0, tm, drain, 0, unroll=8)

        x = _load_token_rows(xbuf.at[slot], dc).astype(BF16)
        h1 = jnp.dot(x, w1b[...], preferred_element_type=F32)
        h3 = jnp.dot(x, w3b[...], preferred_element_type=F32)
        hh = (h1 * _sigmoid(h1)) * h3 * rw_ref[...]
        _store_token_rows(y_ref, jnp.dot(hh.astype(BF16), w2b[...], preferred_element_type=F32))

    @pl.when(jnp.logical_not(valid))
    def _():
        y_ref[...] = jnp.zeros_like(y_ref)


def _moe_ffn(xn, row_tok, row_w, tile_exp, nvalid_tiles, w1, w3, w2, layer, n_tiles, tm):
    d, f = w1.shape[2], w1.shape[3]
    dc = d // LANES
    rows = n_tiles * tm
    def last_valid(t, nv):
        return jnp.minimum(t, jnp.maximum(nv[0] - 1, 0))
    return pl.pallas_call(
        functools.partial(_moe_ffn_body, tm=tm),
        grid_spec=pltpu.PrefetchScalarGridSpec(
            num_scalar_prefetch=3,
            grid=(n_tiles,),
            in_specs=[pl.BlockSpec(memory_space=pl.ANY),
                      pl.BlockSpec((tm, 1), lambda t, te, nv, tk: (last_valid(t, nv), 0)),
                      pl.BlockSpec((None, None, d, f), lambda t, te, nv, tk: (layer, te[t], 0, 0)),
                      pl.BlockSpec((None, None, d, f), lambda t, te, nv, tk: (layer, te[t], 0, 0)),
                      pl.BlockSpec((None, None, f, d), lambda t, te, nv, tk: (layer, te[t], 0, 0))],
            out_specs=pl.BlockSpec((tm * dc, LANES), lambda t, te, nv, tk: (t, 0)),
            scratch_shapes=[pltpu.VMEM((2, tm * dc, LANES), F32), pltpu.SemaphoreType.DMA((2,)),
                            pltpu.VMEM((d, f), BF16), pltpu.VMEM((d, f), BF16), pltpu.VMEM((f, d), BF16)]),
        out_shape=jax.ShapeDtypeStruct((rows * dc, LANES), F32),
        compiler_params=_cparams(("arbitrary",)),
        name="moe_ffn",
    )(tile_exp, nvalid_tiles, row_tok, xn, row_w, w1, w3, w2)


def _combine_copy(y_hbm, buf, sem, slot, k, i, row, dc):
    src = y_hbm.at[pl.ds(pl.multiple_of(row * dc, dc), dc)]
    dst = buf.at[slot, k, pl.ds(pl.multiple_of(i * dc, dc), dc)]
    return pltpu.make_async_copy(src, dst, sem.at[slot])


def _moe_combine_body(pos_ref, h_ref, y_hbm, *rest, tm, final_eps):
    if final_eps is None:
        o_ref, buf, sem = rest
    else:
        gf_ref, o_ref, on_ref, buf, sem = rest
    t = pl.program_id(0)
    nt = pl.num_programs(0)
    dc = h_ref.shape[1] // LANES

    def issue(tile, slot):
        def body(i, c):
            a = 2 * (tile * tm + i)
            _combine_copy(y_hbm, buf, sem, slot, 0, i, pos_ref[a], dc).start()
            _combine_copy(y_hbm, buf, sem, slot, 1, i, pos_ref[a + 1], dc).start()
            return c
        lax.fori_loop(0, tm, body, 0, unroll=8)

    @pl.when(t == 0)
    def _():
        issue(0, 0)

    @pl.when(t + 1 < nt)
    def _():
        issue(t + 1, (t + 1) % 2)

    slot = t % 2

    def drain(i, c):
        _combine_copy(y_hbm, buf, sem, slot, 0, 0, 0, dc).wait()
        _combine_copy(y_hbm, buf, sem, slot, 1, 0, 0, dc).wait()
        return c
    lax.fori_loop(0, tm, drain, 0, unroll=8)

    out = h_ref[...] + _load_token_rows(buf.at[slot, 0], dc) + _load_token_rows(buf.at[slot, 1], dc)
    o_ref[...] = out
    if final_eps is not None:
        inv = lax.rsqrt(jnp.mean(out * out, axis=-1, keepdims=True) + final_eps)
        on_ref[...] = (out * inv) * gf_ref[...]


def _moe_combine(h, ys, pos, tm, final_g=None):
    n_tok, d = h.shape
    tok = pl.BlockSpec((tm, d), lambda i, p: (i, 0))
    in_specs = [tok, pl.BlockSpec(memory_space=pl.ANY)]
    args = [pos, h, ys]
    out_specs = [tok]
    out_shape = [jax.ShapeDtypeStruct((n_tok, d), F32)]
    if final_g is not None:
        in_specs.append(pl.BlockSpec((1, d), lambda i, p: (0, 0)))
        args.append(final_g.reshape(1, d))
        out_specs.append(tok)
        out_shape.append(jax.ShapeDtypeStruct((n_tok, d), F32))
    return pl.pallas_call(
        functools.partial(_moe_combine_body, tm=tm, final_eps=None if final_g is None else NORM_EPS),
        grid_spec=pltpu.PrefetchScalarGridSpec(
            num_scalar_prefetch=1,
            grid=(n_tok // tm,),
            in_specs=in_specs,
            out_specs=out_specs,
            scratch_shapes=[pltpu.VMEM((2, 2, tm * (d // LANES), LANES), F32), pltpu.SemaphoreType.DMA((2,))]),
        out_shape=out_shape,
        compiler_params=_cparams(("arbitrary",)),
        name="moe_combine",
    )(*args)


def _hier_moe(h, g, w_grp, b_grp, w_exp, b_exp, w1, w3, w2, layer, final_g=None):
    n_tok, d = h.shape
    n_exp = w1.shape[1]
    tm = MOE_TILE
    xn, idx, cw = _router(h, g, w_grp, b_grp, w_exp, b_exp, tm=256)

    n_asg = 2 * n_tok
    flat_e = idx[:, :2].reshape(n_asg)
    flat_c = cw[:, :2].reshape(n_asg)
    onehot = (flat_e[:, None] == jnp.arange(n_exp, dtype=I32)[None, :]).astype(I32)
    csum = jnp.cumsum(onehot, axis=0)
    counts = csum[-1]
    rank = jnp.take_along_axis(csum - onehot, flat_e[:, None], axis=1)[:, 0]
    padded = ((counts + tm - 1) // tm) * tm
    ends = jnp.cumsum(padded)
    pos = (ends - padded)[flat_e] + rank
    n_tiles = n_asg // tm + n_exp
    rows = n_tiles * tm
    row_tok = jnp.zeros((rows,), I32).at[pos].set(jnp.arange(n_asg, dtype=I32) // 2)
    row_w = jnp.zeros((rows,), F32).at[pos].set(flat_c).reshape(rows, 1)
    nvalid = (ends[-1] // tm).astype(I32).reshape(1)
    tile_ids = jnp.minimum(jnp.arange(n_tiles, dtype=I32), nvalid - 1)
    tile_exp = jnp.searchsorted(ends, tile_ids * tm, side="right").astype(I32)

    ys = _moe_ffn(xn, row_tok, row_w, tile_exp, nvalid, w1, w3, w2, layer, n_tiles, tm)
    return _moe_combine(h, ys, pos.astype(I32), 128, final_g)


def kernel(x_prompt, x_sample, cache_k, cache_v, page_table, state_wkv, state_shift, norm_mix, norm_ffn, norm_final, even_w_in, even_w_out, rwkv_mu, rwkv_w0, rwkv_w2, rwkv_a0, rwkv_a2, rwkv_g2, rwkv_k_k, rwkv_k_a, rwkv_r_k, rwkv_lnx_g, rwkv_lnx_b, diff_lam_q1, diff_lam_k1, diff_lam_q2, diff_lam_k2, diff_subln_g, gmlp_w_in, gmlp_ln_g, gmlp_ln_b, gmlp_w_s, gmlp_b_s, gmlp_w_out, moe_w_grp, moe_b_grp, moe_w_exp, moe_b_exp, moe_w1, moe_w3, moe_w2):
    bp, tp, d = x_prompt.shape
    bs, ts, _ = x_sample.shape
    n_p, n_s = bp * tp, bs * ts
    n_tok = n_p + n_s
    depth = norm_mix.shape[0]
    npages, psz = page_table.shape[1], cache_k.shape[2]
    past_len = npages * psz
    d_a = rwkv_w0.shape[1]
    d_ap = rwkv_mu.shape[1]
    d_q = cache_k.shape[3] * cache_k.shape[4]
    n_pool = cache_k.shape[1]
    nheads = d_a // RWKV_HEAD_DIM
    tm = 512 if n_p % 512 == 0 and n_s % 512 == 0 else 128
    tq = 512 if tp % 512 == 0 else 128
    tc = min(tp, CHUNK)

    h = jnp.concatenate([x_prompt.reshape(n_p, d), x_sample.reshape(n_s, d)], axis=0)
    pos = jnp.concatenate([jnp.tile(jnp.arange(tp), bp), jnp.tile(past_len + jnp.arange(ts), bs)])
    moe = lambda l, hh, fg=None: _hier_moe(hh, norm_ffn[l], moe_w_grp[l], moe_b_grp[l], moe_w_exp[l],
                                          moe_b_exp[l], moe_w1, moe_w3, moe_w2, l, fg)

    outs = {k: [] for k in ("k_p", "v_p", "wkv_p", "sh_p", "k_s", "v_s", "wkv_s", "sh_s", "gv_s")}
    y_norm = None
    for layer in range(depth):
        j = layer // 2
        xn = _rmsnorm(h, norm_mix[layer], NORM_EPS, BF16, tm)
        last = layer == depth - 1
        if layer % 2 == 0:
            proj = _matmul([xn], even_w_in[j], bm=tm, bn=even_w_in.shape[2] // 5, name="even_in_proj")
            prm = dict(mu=rwkv_mu[j].reshape(1, d_ap), w0=rwkv_w0[j].reshape(1, d_a), w2=rwkv_w2[j],
                       a0=rwkv_a0[j].reshape(1, d_a), a2=rwkv_a2[j], g2=rwkv_g2[j],
                       k_k=rwkv_k_k[j].reshape(1, d_a), k_a=rwkv_k_a[j].reshape(1, d_a),
                       r_k=rwkv_r_k[j].reshape(1, d_a))
            tmr = 256 if tp % 256 == 0 else 128
            tile_last = proj[tmr - 1:n_p:tmr, :d_ap]
            seq_last_s = proj[n_p + ts - 1::ts, :d_ap]
            starts_seq = (jnp.arange(n_p // tmr) % (tp // tmr) == 0)[:, None]
            bnd = jnp.concatenate([jnp.zeros((1, d_ap), F32), tile_last[:-1]], axis=0)
            bnd = jnp.where(starts_seq, 0.0, bnd)
            repl_p = jnp.zeros((n_p // tmr, 8, d_ap), F32).at[:, 0].set(bnd)
            repl_p = repl_p.reshape(n_p // tmr * 8, d_ap)
            repl_s = jnp.repeat(state_shift[j], ts, axis=0)
            pre_p = _rwkv_pre(proj, repl_p, tmr, 0, n_p, tmr, prm)
            pre_s = _rwkv_pre(proj, repl_s, ts, n_p, n_s, tmr, prm)
            oa_p, wkv_p = _rwkv_scan(pre_p, rwkv_lnx_g[j], rwkv_lnx_b[j], None, bp, tp, tc,
                                     2 if bp % 2 == 0 else 1)
            oa_s, wkv_s = _rwkv_scan(pre_s, rwkv_lnx_g[j], rwkv_lnx_b[j], state_wkv[j], bs, ts, ts,
                                     2 if bs % 2 == 0 else 1)

            qs, qf, kf, kb, vb = _rope(proj, pos, d_ap, d_q, tm)
            vf = proj[:, d_ap + 2 * d_q:]
            lam_init = 0.8 - 0.6 * math.exp(-0.3 * layer)
            lam_p = jnp.stack([diff_lam_q1[j], diff_lam_k1[j], diff_lam_q2[j], diff_lam_k2[j]])
            ob_p = _attn_prompt(qs[:n_p], kb[:n_p], vb[:n_p], lam_p, diff_subln_g[j], bp, tp, tq, lam_init)
            ob_s = _attn_sample(qf[n_p:], kf[n_p:], vf[n_p:], cache_k, cache_v, j,
                                page_table.reshape(-1).astype(I32),
                                lam_p, diff_subln_g[j], bs, ts, npages, lam_init)
            oa = jnp.concatenate([oa_p, oa_s.astype(BF16)], axis=0)
            ob = jnp.concatenate([ob_p, ob_s.astype(BF16)], axis=0)
            h = _matmul([oa, ob], even_w_out[j], bm=tm, bn=1024, res=h, name="even_out_proj")

            outs["k_p"].append(kf[:n_p].reshape(bp, tp, -1, DIFF_QK_DIM))
            outs["v_p"].append(vf[:n_p].reshape(bp, tp, -1, DIFF_V_DIM))
            outs["wkv_p"].append(wkv_p)
            outs["sh_p"].append(tile_last[tp // tmr - 1::tp // tmr])
            outs["k_s"].append(kf[n_p:].reshape(bs, ts, -1, DIFF_QK_DIM))
            outs["v_s"].append(vf[n_p:].reshape(bs, ts, -1, DIFF_V_DIM))
            outs["wkv_s"].append(wkv_s)
            outs["sh_s"].append(seq_last_s)
        else:
            d_c = gmlp_ln_g.shape[1]
            z = _matmul([xn], gmlp_w_in[j], bm=tm, bn=1024, act="gelu", name="gmlp_in_proj")
            ws = gmlp_w_s[j]
            bsb = gmlp_b_s[j]
            gd = d_c // GMLP_GROUPS
            lp = min(tp, CHUNK)
            wm_p = jnp.tril(ws[:, :lp, :lp])
            bm_p = jnp.broadcast_to(bsb[:, :lp, None], (GMLP_GROUPS, lp, gd))
            rep = CHUNK // ts
            eye = jnp.eye(rep, dtype=F32)
            wm_s = jnp.einsum("ab,gts->gatbs", eye, jnp.tril(ws[:, :ts, :ts])).reshape(GMLP_GROUPS, CHUNK, CHUNK)
            bm_s = jnp.broadcast_to(jnp.tile(bsb[:, :ts], (1, rep))[:, :, None], (GMLP_GROUPS, CHUNK, gd))
            (y_p,) = _gmlp(z, 0, n_p, gmlp_ln_g[j], gmlp_ln_b[j], wm_p, bm_p, False)
            y_s, v_rows = _gmlp(z, n_p, n_s, gmlp_ln_g[j], gmlp_ln_b[j], wm_s, bm_s, True)
            yin = jnp.concatenate([y_p, y_s], axis=0)
            h = _matmul([yin], gmlp_w_out[j], bm=tm, bn=1024, res=h, name="gmlp_out_proj")
            outs["gv_s"].append(v_rows.reshape(bs, ts, d_c))
        res = moe(layer, h, norm_final if last else None)
        h = res[0]
        if last:
            y_norm = res[1]

    y_prompt = y_norm[:n_p].reshape(bp, tp, d)
    y_sample = y_norm[n_p:].reshape(bs, ts, d)
    st = lambda k: jnp.stack(outs[k])
    return (y_prompt, y_sample, st("k_p"), st("v_p"), st("wkv_p"), st("sh_p"),
            st("k_s"), st("v_s"), st("wkv_s"), st("sh_s"), st("gv_s"))
```

```python
import functools
import math

import jax
import jax.numpy as jnp
from jax import lax
from jax.experimental import pallas as pl
from jax.experimental.pallas import tpu as pltpu

F32 = jnp.float32
BF16 = jnp.bfloat16
I32 = jnp.int32

RWKV_HEAD_DIM = 64
RWKV_GN_EPS = 64e-5
DIFF_QK_DIM = 64
DIFF_V_DIM = 128
ROT_DIM = 16
ROPE_THETA = 500000.0
ATTN_SCALE = DIFF_QK_DIM ** -0.5
NEG_INF = -1e30
NORM_EPS = 1e-6
SUBLN_EPS = 1e-5
GMLP_LN_EPS = 1e-5
GMLP_GROUPS = 8
CHUNK = 128
N_EXP_GROUPS = 4
EXP_PER_GROUP = 8
LORA_W, LORA_A, LORA_G = 64, 64, 128

LANES = 128
MXU_TILE = 256
VMEM_LIMIT = 56 * 1024 * 1024
MOE_TILE = 256
HIGHEST = lax.Precision.HIGHEST


def _cparams(sem):
    return pltpu.CompilerParams(dimension_semantics=sem, vmem_limit_bytes=VMEM_LIMIT)


def _rmsnorm_body(x_ref, g_ref, o_ref, *, eps):
    x = x_ref[...]
    inv = lax.rsqrt(jnp.mean(x * x, axis=-1, keepdims=True) + eps)
    o_ref[...] = ((x * inv) * g_ref[...]).astype(o_ref.dtype)


def _rmsnorm(x, g, eps, out_dtype, tm):
    m, d = x.shape
    return pl.pallas_call(
        functools.partial(_rmsnorm_body, eps=eps),
        grid=(m // tm,),
        in_specs=[pl.BlockSpec((tm, d), lambda i: (i, 0)),
                  pl.BlockSpec((1, d), lambda i: (0, 0))],
        out_specs=pl.BlockSpec((tm, d), lambda i: (i, 0)),
        out_shape=jax.ShapeDtypeStruct((m, d), out_dtype),
        compiler_params=_cparams(("arbitrary",)),
        name="rmsnorm",
    )(x, g.reshape(1, d))


def _gelu_exact(x):
    return 0.5 * x * (1.0 + lax.erf(x * (1.0 / math.sqrt(2.0))))


def _mm_body(*refs, n_x, act, has_res):
    xs = refs[:n_x]
    w_ref = refs[n_x]
    res_ref = refs[n_x + 1] if has_res else None
    o_ref = refs[n_x + 1 + has_res]
    wb_ref = refs[n_x + 2 + has_res]

    @pl.when(pl.program_id(1) == 0)
    def _():
        wb_ref[...] = w_ref[...].astype(BF16)

    kx = w_ref.shape[0] // n_x
    acc = None
    for i, x_ref in enumerate(xs):
        part = jnp.dot(x_ref[...], wb_ref[i * kx:(i + 1) * kx, :], preferred_element_type=F32)
        acc = part if acc is None else acc + part
    if act == "gelu":
        acc = _gelu_exact(acc)
    if has_res:
        acc = acc + res_ref[...]
    o_ref[...] = acc.astype(o_ref.dtype)


def _matmul(xs, w, *, bm, bn, act=None, res=None, out_dtype=F32, name="matmul"):
    m = xs[0].shape[0]
    k, n = w.shape
    kx = k // len(xs)
    in_specs = [pl.BlockSpec((bm, kx), lambda j, i: (i, 0)) for _ in xs]
    in_specs.append(pl.BlockSpec((k, bn), lambda j, i: (0, j)))
    args = list(xs) + [w]
    if res is not None:
        in_specs.append(pl.BlockSpec((bm, bn), lambda j, i: (i, j)))
        args.append(res)
    return pl.pallas_call(
        functools.partial(_mm_body, n_x=len(xs), act=act, has_res=res is not None),
        grid=(n // bn, m // bm),
        in_specs=in_specs,
        out_specs=pl.BlockSpec((bm, bn), lambda j, i: (i, j)),
        out_shape=jax.ShapeDtypeStruct((m, n), out_dtype),
        scratch_shapes=[pltpu.VMEM((k, bn), BF16)],
        compiler_params=_cparams(("arbitrary", "arbitrary")),
        name=name,
    )(*args)


def _block_ones(n, seg):
    r = lax.broadcasted_iota(I32, (n, n), 0) // seg
    c = lax.broadcasted_iota(I32, (n, n), 1) // seg
    return jnp.where(r == c, 1.0, 0.0).astype(BF16)


def _seg_sum(x, ones_bd):
    hi = x.astype(BF16)
    lo = (x - hi.astype(F32)).astype(BF16)
    outs = []
    for c in range(x.shape[1] // MXU_TILE):
        sl = slice(c * MXU_TILE, (c + 1) * MXU_TILE)
        outs.append(jnp.dot(hi[:, sl], ones_bd, preferred_element_type=F32)
                    + jnp.dot(lo[:, sl], ones_bd, preferred_element_type=F32))
    return jnp.concatenate(outs, axis=1)


def _sigmoid(x):
    return 1.0 / (1.0 + jnp.exp(-x))


def _rwkv_pre_body(pa_ref, repl_ref, mu_ref, w0_ref, w2_ref, a0_ref, a2_ref, g2_ref,
                   kk_ref, ka_ref, rk_ref,
                   r_o, dec_o, k_o, v_o, kkn_o, b_o, g_o, bonus_o, *, period, d_a):
    pa = pa_ref[...]
    tm = pa.shape[0]
    row = lax.broadcasted_iota(I32, pa.shape, 0)
    prev = pltpu.roll(pa, 1, 0)
    repl = repl_ref[...]
    if repl.shape[0] != tm:
        repl = jnp.broadcast_to(repl[0:1, :], pa.shape)
    prev = jnp.where(row % period == 0, repl, prev)
    xm = pa + (prev - pa) * mu_ref[...]
    r = xm[:, 0:d_a]
    k = xm[:, d_a:2 * d_a]
    v = xm[:, 2 * d_a:3 * d_a]
    o = 3 * d_a
    wd = xm[:, o:o + LORA_W]
    ad = xm[:, o + LORA_W:o + LORA_W + LORA_A]
    gd = xm[:, o + LORA_W + LORA_A:o + LORA_W + LORA_A + LORA_G]

    z = w0_ref[...] + jnp.dot(jnp.tanh(wd), w2_ref[...], precision=HIGHEST, preferred_element_type=F32)
    w = jnp.minimum(z, 0.0) - jnp.log1p(jnp.exp(-jnp.abs(z))) - 0.5
    dec = jnp.exp(-jnp.exp(w))
    a = _sigmoid(a0_ref[...] + jnp.dot(ad, a2_ref[...], precision=HIGHEST, preferred_element_type=F32))
    g = jnp.dot(_sigmoid(gd).astype(BF16), g2_ref[...].astype(BF16), preferred_element_type=F32)

    ones_bd = _block_ones(MXU_TILE, RWKV_HEAD_DIM)
    kk = k * kk_ref[...]
    nrm = jnp.sqrt(_seg_sum(kk * kk, ones_bd))
    kkn = kk / jnp.maximum(nrm, 1e-12)
    kh = k * (1.0 + (a - 1.0) * ka_ref[...])
    bonus = _seg_sum(r * kh * rk_ref[...], ones_bd) * v

    r_o[...] = r
    dec_o[...] = dec
    k_o[...] = kh
    v_o[...] = v
    kkn_o[...] = kkn
    b_o[...] = kkn * a
    g_o[...] = g
    bonus_o[...] = bonus


def _rwkv_pre(proj, repl, period, row0, rows, tm, prm):
    d_a = prm["w0"].shape[1]
    d_ap = prm["mu"].shape[1]
    nt = rows // tm
    t0 = row0 // tm
    vec = lambda n: pl.BlockSpec((1, n), lambda i: (0, 0))
    full = lambda a: pl.BlockSpec(a.shape, lambda i: (0, 0))
    out_spec = pl.BlockSpec((tm, d_a), lambda i: (i, 0))
    outs = pl.pallas_call(
        functools.partial(_rwkv_pre_body, period=period, d_a=d_a),
        grid=(nt,),
        in_specs=[pl.BlockSpec((tm, d_ap), lambda i: (t0 + i, 0)),
                  pl.BlockSpec((repl.shape[0] // nt, d_ap), lambda i: (i, 0)),
                  vec(d_ap), vec(d_a), full(prm["w2"]), vec(d_a), full(prm["a2"]), full(prm["g2"]),
                  vec(d_a), vec(d_a), vec(d_a)],
        out_specs=[out_spec] * 8,
        out_shape=[jax.ShapeDtypeStruct((rows, d_a), F32)] * 8,
        compiler_params=_cparams(("arbitrary",)),
        name="rwkv_pre",
    )(proj, repl, prm["mu"], prm["w0"], prm["w2"], prm["a0"], prm["a2"], prm["g2"],
      prm["k_k"], prm["k_a"], prm["r_k"])
    return outs


def _rwkv_scan_body(*refs, tc, tp, nheads, nb, has_state):
    if has_state:
        s0_ref = refs[0]
        refs = refs[1:]
    (r_ref, w_ref, k_ref, v_ref, kk_ref, b_ref, g_ref, bonus_ref, lg_ref, lb_ref,
     o_ref, sout_ref, s_scr, vth_scr, ot_scr) = refs
    n = RWKV_HEAD_DIM
    c = pl.program_id(1)
    assert tp == 2 * n
    nh = nb * nheads
    heads = [(b, h) for b in range(nb) for h in range(nheads)]

    @pl.when(c == 0)
    def _():
        if has_state:
            zero = jnp.zeros((n, n), F32)
            for i, (b, h) in enumerate(heads):
                s0 = s0_ref[b, h]
                s_scr[i] = jnp.concatenate([s0, zero] if h % 2 == 0 else [zero, s0], axis=1)
        else:
            s_scr[...] = jnp.zeros_like(s_scr)

    for i, (b, h) in enumerate(heads):
        vh = v_ref[b, :, h * n:(h + 1) * n]
        if tp != tc:
            vh = jnp.concatenate([vh, jnp.zeros((tp - tc, n), F32)], axis=0)
        vth_scr[i * n:(i + 1) * n, :] = vh.T.astype(BF16)
    ot_scr[...] = jnp.zeros_like(ot_scr)

    trow = lax.broadcasted_iota(I32, (tp, 2 * n), 0)
    tcol = lax.broadcasted_iota(I32, (2 * n, tp), 1)
    head_par = lax.broadcasted_iota(I32, (nh, 1, 2 * n), 0) % 2
    lane_par = lax.broadcasted_iota(I32, (nh, 1, 2 * n), 2) // n
    own = head_par == lane_par

    sub = 8

    def per_head(ref, base, masked):
        xs = [ref[b, pl.ds(base, sub), :] for b in range(nb)]
        x = jnp.stack([xs[b][:, (h // 2) * 2 * n:(h // 2 + 1) * 2 * n] for b, h in heads], axis=0)
        return jnp.where(own, x, 0.0) if masked else x

    def group(tg, carry):
        base = pl.multiple_of(tg * sub, sub)
        kk8 = per_head(kk_ref, base, True)
        b8 = per_head(b_ref, base, True)
        k8 = per_head(k_ref, base, True)
        w8 = per_head(w_ref, base, False)
        r8 = per_head(r_ref, base, False)
        st = s_scr[...]
        vth = vth_scr[...]
        o8 = None
        for s in range(sub):
            t = base + s
            pick = jnp.where(trow == t, 1.0, 0.0).astype(BF16)
            v_col = jnp.dot(vth, pick, preferred_element_type=F32).reshape(nh, n, 2 * n)
            sa = -jnp.sum(st * kk8[:, s:s + 1], axis=-1, keepdims=True)
            st = st * w8[:, s:s + 1] + sa * b8[:, s:s + 1] + v_col * k8[:, s:s + 1]
            q = (st * r8[:, s:s + 1]).reshape(nh * n, 2 * n).astype(BF16)
            put = jnp.where(tcol == t, 1.0, 0.0).astype(BF16)
            o_t = jnp.dot(q, put, preferred_element_type=F32)
            o8 = o_t if o8 is None else o8 + o_t
        s_scr[...] = st
        ot_scr[...] = ot_scr[...] + o8.reshape(nh, n, tp)
        return carry

    lax.fori_loop(0, tc // sub, group, 0)

    for b in range(nb):
        for hp in range(nheads // 2):
            parts = []
            for h in (2 * hp, 2 * hp + 1):
                oc = ot_scr[b * nheads + h]
                mean = jnp.mean(oc, axis=0, keepdims=True)
                d = oc - mean
                var = jnp.mean(d * d, axis=0, keepdims=True)
                on = d * lax.rsqrt(var + RWKV_GN_EPS)
                parts.append(on.T[:tc])
            sl = slice(hp * 2 * n, (hp + 1) * 2 * n)
            on2 = jnp.concatenate(parts, axis=1)
            o = (on2 * lg_ref[:, sl] + lb_ref[:, sl] + bonus_ref[b, :, sl]) * g_ref[b, :, sl]
            o_ref[b, :, sl] = o.astype(o_ref.dtype)

    @pl.when(c == pl.num_programs(1) - 1)
    def _():
        for i, (b, h) in enumerate(heads):
            s_h = s_scr[i]
            sout_ref[b, h] = s_h[:, 0:n] if h % 2 == 0 else s_h[:, n:2 * n]


def _rwkv_scan(pre, lnx_g, lnx_b, state, nseq, tseq, tc, nb):
    rows, d_a = pre[0].shape
    nheads = d_a // RWKV_HEAD_DIM
    nch = tseq // tc
    tp = max(tc, LANES)
    n = RWKV_HEAD_DIM
    nh = nb * nheads
    tok = pl.BlockSpec((nb, tc, d_a), lambda s, c: (s, c, 0))
    vec = pl.BlockSpec((1, d_a), lambda s, c: (0, 0))
    st = pl.BlockSpec((nb, nheads, n, n), lambda s, c: (s, 0, 0, 0))
    has_state = state is not None
    in_specs = ([st] if has_state else []) + [tok] * 8 + [vec, vec]
    args = ([state] if has_state else []) + [a.reshape(nseq, tseq, d_a) for a in pre] + [
        lnx_g.reshape(1, d_a), lnx_b.reshape(1, d_a)]
    oa, s_fin = pl.pallas_call(
        functools.partial(_rwkv_scan_body, tc=tc, tp=tp, nheads=nheads, nb=nb, has_state=has_state),
        grid=(nseq // nb, nch),
        in_specs=in_specs,
        out_specs=[tok, st],
        out_shape=[jax.ShapeDtypeStruct((nseq, tseq, d_a), BF16 if tc % 16 == 0 else F32),
                   jax.ShapeDtypeStruct((nseq, nheads, n, n), F32)],
        scratch_shapes=[pltpu.VMEM((nh, n, 2 * n), F32),
                        pltpu.VMEM((nh * n, tp), BF16),
                        pltpu.VMEM((nh, n, tp), F32)],
        compiler_params=_cparams(("arbitrary", "arbitrary")),
        name="rwkv_scan",
    )(*args)
    return oa.reshape(rows, d_a), s_fin


def _rope_body(q_ref, k_ref, v_ref, c_ref, s1_ref, s2_ref, qs_o, qf_o, kf_o, kb_o, vb_o):
    c, s1, s2 = c_ref[...], s1_ref[...], s2_ref[...]
    half = ROT_DIM // 2
    w = q_ref.shape[1]

    def rot(x):
        return x * c + pltpu.roll(x, w - half, 1) * s1 + pltpu.roll(x, half, 1) * s2

    qr = rot(q_ref[...]) * ATTN_SCALE
    qf_o[...] = qr
    qs_o[...] = qr.astype(BF16)
    kr = rot(k_ref[...])
    kf_o[...] = kr
    kb_o[...] = kr.astype(BF16)
    vb_o[...] = v_ref[...].astype(BF16)


def _rope(proj, pos, d_ap, d_q, tm):
    n_tok = proj.shape[0]
    w = MXU_TILE
    half = ROT_DIM // 2
    inv_freq = ROPE_THETA ** (-jnp.arange(half, dtype=F32) / half)
    ang = pos.astype(F32)[:, None] * inv_freq[None, :]
    cos, sin = jnp.cos(ang), jnp.sin(ang)
    pad = DIFF_QK_DIM - ROT_DIM
    ones = jnp.ones((n_tok, pad), F32)
    zeros = jnp.zeros((n_tok, pad), F32)
    zh = jnp.zeros((n_tok, half), F32)
    reps = w // DIFF_QK_DIM
    c_t = jnp.tile(jnp.concatenate([cos, cos, ones], axis=1), (1, reps))
    s1_t = jnp.tile(jnp.concatenate([-sin, zh, zeros], axis=1), (1, reps))
    s2_t = jnp.tile(jnp.concatenate([zh, sin, zeros], axis=1), (1, reps))
    nq = d_q // w
    qb, kb, vb = d_ap // w, d_ap // w + nq, d_ap // w + 2 * nq
    tab = pl.BlockSpec((tm, w), lambda i, j: (i, 0))
    out = pl.BlockSpec((tm, w), lambda i, j: (i, j))
    return pl.pallas_call(
        _rope_body,
        grid=(n_tok // tm, nq),
        in_specs=[pl.BlockSpec((tm, w), lambda i, j: (i, qb + j)),
                  pl.BlockSpec((tm, w), lambda i, j: (i, kb + j)),
                  pl.BlockSpec((tm, w), lambda i, j: (i, vb + j)),
                  tab, tab, tab],
        out_specs=[out, out, out, out, out],
        out_shape=[jax.ShapeDtypeStruct((n_tok, d_q), BF16),
                   jax.ShapeDtypeStruct((n_tok, d_q), F32),
                   jax.ShapeDtypeStruct((n_tok, d_q), F32),
                   jax.ShapeDtypeStruct((n_tok, d_q), BF16),
                   jax.ShapeDtypeStruct((n_tok, d_q), BF16)],
        compiler_params=_cparams(("arbitrary", "arbitrary")),
        name="rope",
    )(proj, proj, proj, c_t, s1_t, s2_t)


def _lambda(lam_ref, lam_init):
    lp = lam_ref[...]
    l1 = jnp.sum(lp[0:1] * lp[1:2], axis=-1, keepdims=True)
    l2 = jnp.sum(lp[2:3] * lp[3:4], axis=-1, keepdims=True)
    return jnp.exp(l1) - jnp.exp(l2) + lam_init


def _subln(o, g_ref, lam_init):
    inv = lax.rsqrt(jnp.mean(o * o, axis=-1, keepdims=True) + SUBLN_EPS)
    return (o * inv) * g_ref[...] * (1.0 - lam_init)


def _online_softmax_update(s, v, m_scr, l_scr, acc_scr):
    m_prev = m_scr[...]
    m_new = jnp.maximum(m_prev, jnp.max(s, axis=-1, keepdims=True))
    alpha = jnp.exp(m_prev - m_new)
    p = jnp.exp(s - m_new[:, 0:1])
    l_scr[...] = alpha * l_scr[...] + jnp.sum(p, axis=-1, keepdims=True)
    acc_scr[...] = alpha[:, 0:1] * acc_scr[...] + jnp.dot(p.astype(BF16), v, preferred_element_type=F32)
    m_scr[...] = m_new


def _attn_prompt_body(q_ref, k_ref, v_ref, lam_ref, g_ref, o_ref, qq_scr, m_scr, l_scr, acc_scr,
                      *, tq, lam_init):
    qi = pl.program_id(2)
    ki = pl.program_id(3)

    @pl.when(ki == 0)
    def _():
        q = q_ref[...]
        lane = lax.broadcasted_iota(I32, q.shape, 1)
        zero = jnp.zeros_like(q)
        qq_scr[0:tq] = jnp.where(lane < DIFF_QK_DIM, q, zero)
        qq_scr[tq:2 * tq] = jnp.where(lane >= DIFF_QK_DIM, q, zero)
        m_scr[...] = jnp.full_like(m_scr, NEG_INF)
        l_scr[...] = jnp.zeros_like(l_scr)
        acc_scr[...] = jnp.zeros_like(acc_scr)

    rb = min(tq, 256)

    def tile(diagonal):
        for r0 in range(0, 2 * tq, rb):
            rows = slice(r0, r0 + rb)
            nk = min(tq, (r0 % tq) + rb) if diagonal else tq
            s = lax.dot_general(qq_scr[rows], k_ref[0:nk], (((1,), (1,)), ((), ())),
                                preferred_element_type=F32)
            if diagonal:
                row = (r0 % tq) + lax.broadcasted_iota(I32, s.shape, 0)
                col = lax.broadcasted_iota(I32, s.shape, 1)
                s = jnp.where(row >= col, s, NEG_INF)
            m_prev = m_scr[rows]
            m_new = jnp.maximum(m_prev, jnp.max(s, axis=-1, keepdims=True))
            alpha = jnp.exp(m_prev - m_new)
            p = jnp.exp(s - jnp.concatenate([m_new] * (nk // LANES), axis=1))
            l_scr[rows] = alpha * l_scr[rows] + jnp.sum(p, axis=-1, keepdims=True)
            acc_scr[rows] = alpha * acc_scr[rows] + jnp.dot(p.astype(BF16), v_ref[0:nk],
                                                            preferred_element_type=F32)
            m_scr[rows] = m_new

    @pl.when(ki < qi)
    def _():
        tile(False)

    @pl.when(ki == qi)
    def _():
        tile(True)
        lam = _lambda(lam_ref, lam_init)
        o1 = acc_scr[0:tq] / l_scr[0:tq, 0:1]
        o2 = acc_scr[tq:2 * tq] / l_scr[tq:2 * tq, 0:1]
        o_ref[...] = _subln(o1 - lam * o2, g_ref, lam_init).astype(o_ref.dtype)


def _attn_prompt(qs, kb, vb, lam_p, subln_g, nseq, tseq, tq, lam_init):
    rows, d_q = qs.shape
    nh = d_q // DIFF_V_DIM
    nq = tseq // tq
    w = DIFF_V_DIM
    return pl.pallas_call(
        functools.partial(_attn_prompt_body, tq=tq, lam_init=lam_init),
        grid=(nseq, nh, nq, nq),
        in_specs=[pl.BlockSpec((tq, w), lambda b, h, i, j: (b * nq + i, h)),
                  pl.BlockSpec((tq, w), lambda b, h, i, j: (b * nq + jnp.minimum(i, j), h)),
                  pl.BlockSpec((tq, w), lambda b, h, i, j: (b * nq + jnp.minimum(i, j), h)),
                  pl.BlockSpec(lam_p.shape, lambda b, h, i, j: (0, 0)),
                  pl.BlockSpec((1, w), lambda b, h, i, j: (0, 0))],
        out_specs=pl.BlockSpec((tq, w), lambda b, h, i, j: (b * nq + i, h)),
        out_shape=jax.ShapeDtypeStruct((nseq * tseq, d_q), BF16),
        scratch_shapes=[pltpu.VMEM((2 * tq, w), BF16),
                        pltpu.VMEM((2 * tq, LANES), F32),
                        pltpu.VMEM((2 * tq, LANES), F32),
                        pltpu.VMEM((2 * tq, w), F32)],
        compiler_params=_cparams(("arbitrary", "arbitrary", "arbitrary", "arbitrary")),
        name="attn_prompt",
    )(qs, kb, vb, lam_p, subln_g.reshape(1, w))


def _attn_sample_body(pt_ref, q_ref, *refs, ts, lam_init, pages_per_step):
    kc_refs = refs[:pages_per_step]
    vc_refs = refs[pages_per_step:2 * pages_per_step]
    kn_ref, vn_ref, lam_ref, g_ref, o_ref, qbd_scr, m_scr, l_scr, acc_scr = refs[2 * pages_per_step:]
    p = pl.program_id(1)
    nheads_qk = q_ref.shape[1] // DIFF_QK_DIM

    @pl.when(p == 0)
    def _():
        q = q_ref[...]
        head = lax.broadcasted_iota(I32, q.shape, 1) // DIFF_QK_DIM
        zero = jnp.zeros_like(q)
        rows = [jnp.where(head == m, q, zero) for m in range(nheads_qk)]
        qbd_scr[...] = jnp.concatenate(rows, axis=0).astype(BF16)
        m_scr[...] = jnp.full_like(m_scr, NEG_INF)
        l_scr[...] = jnp.zeros_like(l_scr)
        acc_scr[...] = jnp.zeros_like(acc_scr)

    nh_v = nheads_qk // 2
    w = DIFF_V_DIM

    def update(s, v_heads):
        m_prev = m_scr[...]
        m_new = jnp.maximum(m_prev, jnp.max(s, axis=-1, keepdims=True))
        alpha = jnp.exp(m_prev - m_new)
        pr = jnp.exp(s - jnp.concatenate([m_new] * (s.shape[1] // LANES), axis=1))
        l_scr[...] = alpha * l_scr[...] + jnp.sum(pr, axis=-1, keepdims=True)
        m_scr[...] = m_new
        pb = pr.astype(BF16)
        for h in range(nh_v):
            rows = slice(2 * h * ts, (2 * h + 2) * ts)
            acc_scr[rows] = alpha[rows] * acc_scr[rows] + jnp.dot(pb[rows], v_heads[h],
                                                                  preferred_element_type=F32)

    kt = jnp.concatenate([r[...].reshape(nheads_qk * DIFF_QK_DIM, r.shape[2]) for r in kc_refs],
                         axis=1).astype(BF16)
    s_past = jnp.dot(qbd_scr[...], kt, preferred_element_type=F32)
    npos = vc_refs[0].shape[0] // nh_v
    v_past = [jnp.concatenate([r[pl.ds(h, npos, stride=nh_v), :] for r in vc_refs], axis=0).astype(BF16)
              for h in range(nh_v)]
    update(s_past, v_past)

    @pl.when(p == pl.num_programs(1) - 1)
    def _():
        padk = jnp.zeros((LANES - ts, kn_ref.shape[1]), F32)
        kn = jnp.concatenate([kn_ref[...], padk], axis=0).astype(BF16)
        vn = jnp.concatenate([vn_ref[...], padk], axis=0).astype(BF16)
        s = lax.dot_general(qbd_scr[...], kn, (((1,), (1,)), ((), ())), preferred_element_type=F32)
        row = lax.broadcasted_iota(I32, s.shape, 0) % ts
        col = lax.broadcasted_iota(I32, s.shape, 1)
        s = jnp.where(row >= col, s, NEG_INF)
        update(s, [vn[:, h * w:(h + 1) * w] for h in range(nh_v)])
        lam = _lambda(lam_ref, lam_init)
        for h in range(nh_v):
            r1 = slice(2 * h * ts, (2 * h + 1) * ts)
            r2 = slice((2 * h + 1) * ts, (2 * h + 2) * ts)
            o1 = acc_scr[r1] / l_scr[r1, 0:1]
            o2 = acc_scr[r2] / l_scr[r2, 0:1]
            o_ref[:, h * w:(h + 1) * w] = _subln(o1 - lam * o2, g_ref, lam_init).astype(o_ref.dtype)


def _attn_sample(qs, kb, vb, cache_k, cache_v, layer_j, page_flat, lam_p, subln_g, nb, ts, npages, lam_init):
    d_q = qs.shape[1]
    nrow = (d_q // DIFF_QK_DIM) * ts
    gpp = next(g for g in (8, 4, 2, 1) if npages % g == 0)
    tok = pl.BlockSpec((ts, d_q), lambda b, p, pt: (b, 0))
    cache_k = jnp.transpose(cache_k, (0, 1, 3, 4, 2))
    cache_v = cache_v.reshape(cache_v.shape[:2] + (-1, cache_v.shape[-1]))

    def page(cache, g):
        return pl.BlockSpec((None, None) + cache.shape[2:],
                            lambda b, p, pt: (layer_j, pt[b * npages + p * gpp + g]) + (0,) * (cache.ndim - 2))

    return pl.pallas_call(
        functools.partial(_attn_sample_body, ts=ts, lam_init=lam_init, pages_per_step=gpp),
        grid_spec=pltpu.PrefetchScalarGridSpec(
            num_scalar_prefetch=1,
            grid=(nb, npages // gpp),
            in_specs=[tok] + [page(cache_k, g) for g in range(gpp)] + [page(cache_v, g) for g in range(gpp)]
                     + [tok, tok,
                        pl.BlockSpec(lam_p.shape, lambda b, p, pt: (0, 0)),
                        pl.BlockSpec((1, DIFF_V_DIM), lambda b, p, pt: (0, 0))],
            out_specs=tok,
            scratch_shapes=[pltpu.VMEM((nrow, d_q), BF16),
                            pltpu.VMEM((nrow, LANES), F32),
                            pltpu.VMEM((nrow, LANES), F32),
                            pltpu.VMEM((nrow, DIFF_V_DIM), F32)]),
        out_shape=jax.ShapeDtypeStruct((nb * ts, d_q), F32),
        compiler_params=_cparams(("arbitrary", "arbitrary")),
        name="attn_sample",
    )(page_flat, qs, *([cache_k] * gpp), *([cache_v] * gpp), kb, vb, lam_p, subln_g.reshape(1, DIFF_V_DIM))


def _gmlp_body(z_ref, lg_ref, lb_ref, wm_ref, bm_ref, y_ref, *maybe_v, d_c):
    u = z_ref[:, 0:d_c]
    v = z_ref[:, d_c:2 * d_c]
    mean = jnp.mean(v, axis=-1, keepdims=True)
    d = v - mean
    var = jnp.mean(d * d, axis=-1, keepdims=True)
    vn = (d * lax.rsqrt(var + GMLP_LN_EPS)) * lg_ref[...] + lb_ref[...]
    if maybe_v:
        maybe_v[0][...] = vn
    gd = d_c // GMLP_GROUPS
    vb = vn.astype(BF16)
    for g in range(GMLP_GROUPS):
        sl = slice(g * gd, (g + 1) * gd)
        mixed = jnp.dot(wm_ref[g].astype(BF16), vb[:, sl], preferred_element_type=F32) + bm_ref[g]
        y_ref[:, sl] = (u[:, sl] * mixed).astype(y_ref.dtype)


def _gmlp(z, row0, rows, ln_g, ln_b, wmix, bmix, emit_v):
    d_c = z.shape[1] // 2
    tm = CHUNK
    t0 = row0 // tm
    gd = d_c // GMLP_GROUPS
    out_specs = [pl.BlockSpec((tm, d_c), lambda i: (i, 0))]
    out_shape = [jax.ShapeDtypeStruct((rows, d_c), BF16)]
    if emit_v:
        out_specs.append(pl.BlockSpec((tm, d_c), lambda i: (i, 0)))
        out_shape.append(jax.ShapeDtypeStruct((rows, d_c), F32))
    return pl.pallas_call(
        functools.partial(_gmlp_body, d_c=d_c),
        grid=(rows // tm,),
        in_specs=[pl.BlockSpec((tm, 2 * d_c), lambda i: (t0 + i, 0)),
                  pl.BlockSpec((1, d_c), lambda i: (0, 0)),
                  pl.BlockSpec((1, d_c), lambda i: (0, 0)),
                  pl.BlockSpec((GMLP_GROUPS, tm, tm), lambda i: (0, 0, 0)),
                  pl.BlockSpec((GMLP_GROUPS, tm, gd), lambda i: (0, 0, 0))],
        out_specs=out_specs,
        out_shape=out_shape,
        compiler_params=_cparams(("arbitrary",)),
        name="gmlp_gate",
    )(z, ln_g.reshape(1, d_c), ln_b.reshape(1, d_c), wmix, bmix)


def _store_token_rows(ref2, x2d):
    dc = x2d.shape[1] // LANES
    for c in range(dc):
        ref2[pl.ds(c, x2d.shape[0], stride=dc), :] = x2d[:, c * LANES:(c + 1) * LANES]


def _load_token_rows(ref2, dc):
    n = ref2.shape[0] // dc
    return jnp.concatenate([ref2[pl.ds(c, n, stride=dc), :] for c in range(dc)], axis=1)


def _router_body(h_ref, g_ref, wr_ref, br_ref, xn_o, idx_o, cw_o):
    x = h_ref[...]
    inv = lax.rsqrt(jnp.mean(x * x, axis=-1, keepdims=True) + NORM_EPS)
    xn = (x * inv) * g_ref[...]
    _store_token_rows(xn_o, xn)
    logits = jnp.dot(xn, wr_ref[...], precision=HIGHEST, preferred_element_type=F32) + br_ref[...]
    lane = lax.broadcasted_iota(I32, logits.shape, 1)
    big = jnp.int32(1 << 20)
    ninf = jnp.float32(-jnp.inf)

    def first_max(vals):
        mx = jnp.max(vals, axis=-1, keepdims=True)
        ix = jnp.min(jnp.where(vals == mx, lane, big), axis=-1, keepdims=True)
        return mx, ix

    gl = jnp.where(lane < N_EXP_GROUPS, logits, ninf)
    gmax, gtop = first_max(gl)
    gate = 1.0 / jnp.sum(jnp.exp(gl - gmax), axis=-1, keepdims=True)
    lo = N_EXP_GROUPS + EXP_PER_GROUP * gtop
    el = jnp.where((lane >= lo) & (lane < lo + EXP_PER_GROUP), logits, ninf)
    m1, i1 = first_max(el)
    m2, i2 = first_max(jnp.where(lane == i1, ninf, el))
    p2 = jnp.exp(m2 - m1)
    c1 = gate / (1.0 + p2)
    c2 = gate * p2 / (1.0 + p2)
    idx_o[...] = jnp.where(lane == 0, i1 - N_EXP_GROUPS, jnp.where(lane == 1, i2 - N_EXP_GROUPS, 0))
    cw_o[...] = jnp.where(lane == 0, c1, jnp.where(lane == 1, c2, 0.0))


def _router(h, g, w_grp, b_grp, w_exp, b_exp, tm):
    n_tok, d = h.shape
    ncol = w_grp.shape[1] + w_exp.shape[1]
    wr = jnp.concatenate([w_grp, w_exp, jnp.zeros((d, LANES - ncol), F32)], axis=1)
    br = jnp.concatenate([b_grp, b_exp, jnp.zeros((LANES - ncol,), F32)]).reshape(1, LANES)
    tok = lambda w: pl.BlockSpec((tm, w), lambda i: (i, 0))
    return pl.pallas_call(
        _router_body,
        grid=(n_tok // tm,),
        in_specs=[tok(d), pl.BlockSpec((1, d), lambda i: (0, 0)),
                  pl.BlockSpec((d, LANES), lambda i: (0, 0)),
                  pl.BlockSpec((1, LANES), lambda i: (0, 0))],
        out_specs=[pl.BlockSpec((tm * (d // LANES), LANES), lambda i: (i, 0)), tok(LANES), tok(LANES)],
        out_shape=[jax.ShapeDtypeStruct((n_tok * (d // LANES), LANES), F32),
                   jax.ShapeDtypeStruct((n_tok, LANES), I32),
                   jax.ShapeDtypeStruct((n_tok, LANES), F32)],
        compiler_params=_cparams(("arbitrary",)),
        name="moe_router",
    )(h, g.reshape(1, d), wr, br)


def _gather_copy(x_hbm, xbuf, sem, slot, i, tok, dc):
    src = x_hbm.at[pl.ds(pl.multiple_of(tok * dc, dc), dc)]
    dst = xbuf.at[slot, pl.ds(pl.multiple_of(i * dc, dc), dc)]
    return pltpu.make_async_copy(src, dst, sem.at[slot])


def _moe_ffn_body(te_ref, nvalid_ref, tok_ref, x_hbm, rw_ref, w1_ref, w3_ref, w2_ref, y_ref,
                  xbuf, sem, w1b, w3b, w2b, *, tm):
    t = pl.program_id(0)
    nv = nvalid_ref[0]
    valid = t < nv
    dc = w1_ref.shape[0] // LANES

    def issue(tile, slot):
        def body(i, c):
            _gather_copy(x_hbm, xbuf, sem, slot, i, tok_ref[tile * tm + i], dc).start()
            return c
        lax.fori_loop(0, tm, body, 0, unroll=8)

    @pl.when((t == 0) & valid)
    def _():
        issue(0, 0)

    @pl.when(t + 1 < nv)
    def _():
        issue(t + 1, (t + 1) % 2)

    @pl.when(valid)
    def _():
        prev = te_ref[jnp.maximum(t - 1, 0)]

        @pl.when((t == 0) | (te_ref[t] != prev))
        def _():
            w1b[...] = w1_ref[...].astype(BF16)
            w3b[...] = w3_ref[...].astype(BF16)
            w2b[...] = w2_ref[...].astype(BF16)

        slot = t % 2

        def drain(i, c):
            _gather_copy(x_hbm, xbuf, sem, slot, 0, 0, dc).wait()
            return c
        lax.fori_loop(0, tm, drain, 0, unroll=8)

        x = _load_token_rows(xbuf.at[slot], dc).astype(BF16)
        h1 = jnp.dot(x, w1b[...], preferred_element_type=F32)
        h3 = jnp.dot(x, w3b[...], preferred_element_type=F32)
        hh = (h1 * _sigmoid(h1)) * h3 * rw_ref[...]
        _store_token_rows(y_ref, jnp.dot(hh.astype(BF16), w2b[...], preferred_element_type=F32))

    @pl.when(jnp.logical_not(valid))
    def _():
        y_ref[...] = jnp.zeros_like(y_ref)


def _moe_ffn(xn, row_tok, row_w, tile_exp, nvalid_tiles, w1, w3, w2, layer, n_tiles, tm):
    d, f = w1.shape[2], w1.shape[3]
    dc = d // LANES
    rows = n_tiles * tm
    def last_valid(t, nv):
        return jnp.minimum(t, jnp.maximum(nv[0] - 1, 0))
    return pl.pallas_call(
        functools.partial(_moe_ffn_body, tm=tm),
        grid_spec=pltpu.PrefetchScalarGridSpec(
            num_scalar_prefetch=3,
            grid=(n_tiles,),
            in_specs=[pl.BlockSpec(memory_space=pl.ANY),
                      pl.BlockSpec((tm, 1), lambda t, te, nv, tk: (last_valid(t, nv), 0)),
                      pl.BlockSpec((None, None, d, f), lambda t, te, nv, tk: (layer, te[t], 0, 0)),
                      pl.BlockSpec((None, None, d, f), lambda t, te, nv, tk: (layer, te[t], 0, 0)),
                      pl.BlockSpec((None, None, f, d), lambda t, te, nv, tk: (layer, te[t], 0, 0))],
            out_specs=pl.BlockSpec((tm * dc, LANES), lambda t, te, nv, tk: (t, 0)),
            scratch_shapes=[pltpu.VMEM((2, tm * dc, LANES), F32), pltpu.SemaphoreType.DMA((2,)),
                            pltpu.VMEM((d, f), BF16), pltpu.VMEM((d, f), BF16), pltpu.VMEM((f, d), BF16)]),
        out_shape=jax.ShapeDtypeStruct((rows * dc, LANES), F32),
        compiler_params=_cparams(("arbitrary",)),
        name="moe_ffn",
    )(tile_exp, nvalid_tiles, row_tok, xn, row_w, w1, w3, w2)


def _combine_copy(y_hbm, buf, sem, slot, k, i, row, dc):
    src = y_hbm.at[pl.ds(pl.multiple_of(row * dc, dc), dc)]
    dst = buf.at[slot, k, pl.ds(pl.multiple_of(i * dc, dc), dc)]
    return pltpu.make_async_copy(src, dst, sem.at[slot])


def _moe_combine_body(pos_ref, h_ref, y_hbm, *rest, tm, final_eps):
    if final_eps is None:
        o_ref, buf, sem = rest
    else:
        gf_ref, o_ref, on_ref, buf, sem = rest
    t = pl.program_id(0)
    nt = pl.num_programs(0)
    dc = h_ref.shape[1] // LANES

    def issue(tile, slot):
        def body(i, c):
            a = 2 * (tile * tm + i)
            _combine_copy(y_hbm, buf, sem, slot, 0, i, pos_ref[a], dc).start()
            _combine_copy(y_hbm, buf, sem, slot, 1, i, pos_ref[a + 1], dc).start()
            return c
        lax.fori_loop(0, tm, body, 0, unroll=8)

    @pl.when(t == 0)
    def _():
        issue(0, 0)

    @pl.when(t + 1 < nt)
    def _():
        issue(t + 1, (t + 1) % 2)

    slot = t % 2

    def drain(i, c):
        _combine_copy(y_hbm, buf, sem, slot, 0, 0, 0, dc).wait()
        _combine_copy(y_hbm, buf, sem, slot, 1, 0, 0, dc).wait()
        return c
    lax.fori_loop(0, tm, drain, 0, unroll=8)

    out = h_ref[...] + _load_token_rows(buf.at[slot, 0], dc) + _load_token_rows(buf.at[slot, 1], dc)
    o_ref[...] = out
    if final_eps is not None:
        inv = lax.rsqrt(jnp.mean(out * out, axis=-1, keepdims=True) + final_eps)
        on_ref[...] = (out * inv) * gf_ref[...]


def _moe_combine(h, ys, pos, tm, final_g=None):
    n_tok, d = h.shape
    tok = pl.BlockSpec((tm, d), lambda i, p: (i, 0))
    in_specs = [tok, pl.BlockSpec(memory_space=pl.ANY)]
    args = [pos, h, ys]
    out_specs = [tok]
    out_shape = [jax.ShapeDtypeStruct((n_tok, d), F32)]
    if final_g is not None:
        in_specs.append(pl.BlockSpec((1, d), lambda i, p: (0, 0)))
        args.append(final_g.reshape(1, d))
        out_specs.append(tok)
        out_shape.append(jax.ShapeDtypeStruct((n_tok, d), F32))
    return pl.pallas_call(
        functools.partial(_moe_combine_body, tm=tm, final_eps=None if final_g is None else NORM_EPS),
        grid_spec=pltpu.PrefetchScalarGridSpec(
            num_scalar_prefetch=1,
            grid=(n_tok // tm,),
            in_specs=in_specs,
            out_specs=out_specs,
            scratch_shapes=[pltpu.VMEM((2, 2, tm * (d // LANES), LANES), F32), pltpu.SemaphoreType.DMA((2,))]),
        out_shape=out_shape,
        compiler_params=_cparams(("arbitrary",)),
        name="moe_combine",
    )(*args)


def _hier_moe(h, g, w_grp, b_grp, w_exp, b_exp, w1, w3, w2, layer, final_g=None):
    n_tok, d = h.shape
    n_exp = w1.shape[1]
    tm = MOE_TILE
    xn, idx, cw = _router(h, g, w_grp, b_grp, w_exp, b_exp, tm=256)

    n_asg = 2 * n_tok
    flat_e = idx[:, :2].reshape(n_asg)
    flat_c = cw[:, :2].reshape(n_asg)
    onehot = (flat_e[:, None] == jnp.arange(n_exp, dtype=I32)[None, :]).astype(I32)
    csum = jnp.cumsum(onehot, axis=0)
    counts = csum[-1]
    rank = jnp.take_along_axis(csum - onehot, flat_e[:, None], axis=1)[:, 0]
    padded = ((counts + tm - 1) // tm) * tm
    ends = jnp.cumsum(padded)
    pos = (ends - padded)[flat_e] + rank
    n_tiles = n_asg // tm + n_exp
    rows = n_tiles * tm
    upd = jnp.stack([jnp.arange(n_asg, dtype=I32) // 2, lax.bitcast_convert_type(flat_c, I32)], axis=1)
    row_meta = jnp.zeros((rows, 2), I32).at[pos].set(upd)
    row_tok = row_meta[:, 0]
    row_w = lax.bitcast_convert_type(row_meta[:, 1], F32).reshape(rows, 1)
    nvalid = (ends[-1] // tm).astype(I32).reshape(1)
    tile_ids = jnp.minimum(jnp.arange(n_tiles, dtype=I32), nvalid - 1)
    tile_exp = jnp.searchsorted(ends, tile_ids * tm, side="right").astype(I32)

    ys = _moe_ffn(xn, row_tok, row_w, tile_exp, nvalid, w1, w3, w2, layer, n_tiles, tm)
    return _moe_combine(h, ys, pos.astype(I32), 128, final_g)


def kernel(x_prompt, x_sample, cache_k, cache_v, page_table, state_wkv, state_shift, norm_mix, norm_ffn, norm_final, even_w_in, even_w_out, rwkv_mu, rwkv_w0, rwkv_w2, rwkv_a0, rwkv_a2, rwkv_g2, rwkv_k_k, rwkv_k_a, rwkv_r_k, rwkv_lnx_g, rwkv_lnx_b, diff_lam_q1, diff_lam_k1, diff_lam_q2, diff_lam_k2, diff_subln_g, gmlp_w_in, gmlp_ln_g, gmlp_ln_b, gmlp_w_s, gmlp_b_s, gmlp_w_out, moe_w_grp, moe_b_grp, moe_w_exp, moe_b_exp, moe_w1, moe_w3, moe_w2):
    bp, tp, d = x_prompt.shape
    bs, ts, _ = x_sample.shape
    n_p, n_s = bp * tp, bs * ts
    n_tok = n_p + n_s
    depth = norm_mix.shape[0]
    npages, psz = page_table.shape[1], cache_k.shape[2]
    past_len = npages * psz
    d_a = rwkv_w0.shape[1]
    d_ap = rwkv_mu.shape[1]
    d_q = cache_k.shape[3] * cache_k.shape[4]
    n_pool = cache_k.shape[1]
    nheads = d_a // RWKV_HEAD_DIM
    tm = 512 if n_p % 512 == 0 and n_s % 512 == 0 else 128
    tq = next((t for t in (1024, 512) if tp % t == 0), 128)
    tc = min(tp, CHUNK)

    h = jnp.concatenate([x_prompt.reshape(n_p, d), x_sample.reshape(n_s, d)], axis=0)
    pos = jnp.concatenate([jnp.tile(jnp.arange(tp), bp), jnp.tile(past_len + jnp.arange(ts), bs)])
    moe = lambda l, hh, fg=None: _hier_moe(hh, norm_ffn[l], moe_w_grp[l], moe_b_grp[l], moe_w_exp[l],
                                          moe_b_exp[l], moe_w1, moe_w3, moe_w2, l, fg)

    outs = {k: [] for k in ("k_p", "v_p", "wkv_p", "sh_p", "k_s", "v_s", "wkv_s", "sh_s", "gv_s")}
    y_norm = None
    for layer in range(depth):
        j = layer // 2
        xn = _rmsnorm(h, norm_mix[layer], NORM_EPS, BF16, tm)
        last = layer == depth - 1
        if layer % 2 == 0:
            proj = _matmul([xn], even_w_in[j], bm=tm, bn=even_w_in.shape[2] // 5, name="even_in_proj")
            prm = dict(mu=rwkv_mu[j].reshape(1, d_ap), w0=rwkv_w0[j].reshape(1, d_a), w2=rwkv_w2[j],
                       a0=rwkv_a0[j].reshape(1, d_a), a2=rwkv_a2[j], g2=rwkv_g2[j],
                       k_k=rwkv_k_k[j].reshape(1, d_a), k_a=rwkv_k_a[j].reshape(1, d_a),
                       r_k=rwkv_r_k[j].reshape(1, d_a))
            tmr = 256 if tp % 256 == 0 else 128
            tile_last = proj[tmr - 1:n_p:tmr, :d_ap]
            seq_last_s = proj[n_p + ts - 1::ts, :d_ap]
            starts_seq = (jnp.arange(n_p // tmr) % (tp // tmr) == 0)[:, None]
            bnd = jnp.concatenate([jnp.zeros((1, d_ap), F32), tile_last[:-1]], axis=0)
            bnd = jnp.where(starts_seq, 0.0, bnd)
            repl_p = jnp.zeros((n_p // tmr, 8, d_ap), F32).at[:, 0].set(bnd)
            repl_p = repl_p.reshape(n_p // tmr * 8, d_ap)
            repl_s = jnp.repeat(state_shift[j], ts, axis=0)
            pre_p = _rwkv_pre(proj, repl_p, tmr, 0, n_p, tmr, prm)
            pre_s = _rwkv_pre(proj, repl_s, ts, n_p, n_s, tmr, prm)
            oa_p, wkv_p = _rwkv_scan(pre_p, rwkv_lnx_g[j], rwkv_lnx_b[j], None, bp, tp, tc,
                                     2 if bp % 2 == 0 else 1)
            oa_s, wkv_s = _rwkv_scan(pre_s, rwkv_lnx_g[j], rwkv_lnx_b[j], state_wkv[j], bs, ts, ts,
                                     2 if bs % 2 == 0 else 1)

            qs, qf, kf, kb, vb = _rope(proj, pos, d_ap, d_q, tm)
            vf = proj[:, d_ap + 2 * d_q:]
            lam_init = 0.8 - 0.6 * math.exp(-0.3 * layer)
            lam_p = jnp.stack([diff_lam_q1[j], diff_lam_k1[j], diff_lam_q2[j], diff_lam_k2[j]])
            ob_p = _attn_prompt(qs[:n_p], kb[:n_p], vb[:n_p], lam_p, diff_subln_g[j], bp, tp, tq, lam_init)
            ob_s = _attn_sample(qf[n_p:], kf[n_p:], vf[n_p:], cache_k, cache_v, j,
                                page_table.reshape(-1).astype(I32),
                                lam_p, diff_subln_g[j], bs, ts, npages, lam_init)
            oa = jnp.concatenate([oa_p, oa_s.astype(BF16)], axis=0)
            ob = jnp.concatenate([ob_p, ob_s.astype(BF16)], axis=0)
            h = _matmul([oa, ob], even_w_out[j], bm=tm, bn=1024, res=h, name="even_out_proj")

            outs["k_p"].append(kf[:n_p].reshape(bp, tp, -1, DIFF_QK_DIM))
            outs["v_p"].append(vf[:n_p].reshape(bp, tp, -1, DIFF_V_DIM))
            outs["wkv_p"].append(wkv_p)
            outs["sh_p"].append(tile_last[tp // tmr - 1::tp // tmr])
            outs["k_s"].append(kf[n_p:].reshape(bs, ts, -1, DIFF_QK_DIM))
            outs["v_s"].append(vf[n_p:].reshape(bs, ts, -1, DIFF_V_DIM))
            outs["wkv_s"].append(wkv_s)
            outs["sh_s"].append(seq_last_s)
        else:
            d_c = gmlp_ln_g.shape[1]
            z = _matmul([xn], gmlp_w_in[j], bm=tm, bn=1024, act="gelu", name="gmlp_in_proj")
            ws = gmlp_w_s[j]
            bsb = gmlp_b_s[j]
            gd = d_c // GMLP_GROUPS
            lp = min(tp, CHUNK)
            wm_p = jnp.tril(ws[:, :lp, :lp])
            bm_p = jnp.broadcast_to(bsb[:, :lp, None], (GMLP_GROUPS, lp, gd))
            rep = CHUNK // ts
            eye = jnp.eye(rep, dtype=F32)
            wm_s = jnp.einsum("ab,gts->gatbs", eye, jnp.tril(ws[:, :ts, :ts])).reshape(GMLP_GROUPS, CHUNK, CHUNK)
            bm_s = jnp.broadcast_to(jnp.tile(bsb[:, :ts], (1, rep))[:, :, None], (GMLP_GROUPS, CHUNK, gd))
            (y_p,) = _gmlp(z, 0, n_p, gmlp_ln_g[j], gmlp_ln_b[j], wm_p, bm_p, False)
            y_s, v_rows = _gmlp(z, n_p, n_s, gmlp_ln_g[j], gmlp_ln_b[j], wm_s, bm_s, True)
            yin = jnp.concatenate([y_p, y_s], axis=0)
            h = _matmul([yin], gmlp_w_out[j], bm=tm, bn=1024, res=h, name="gmlp_out_proj")
            outs["gv_s"].append(v_rows.reshape(bs, ts, d_c))
        res = moe(layer, h, norm_final if last else None)
        h = res[0]
        if last:
            y_norm = res[1]

    y_prompt = y_norm[:n_p].reshape(bp, tp, d)
    y_sample = y_norm[n_p:].reshape(bs, ts, d)
    st = lambda k: jnp.stack(outs[k])
    return (y_prompt, y_sample, st("k_p"), st("v_p"), st("wkv_p"), st("sh_p"),
            st("k_s"), st("v_s"), st("wkv_s"), st("sh_s"), st("gv_s"))
```

```python
import functools
import math

import jax
import jax.numpy as jnp
from jax import lax
from jax.experimental import pallas as pl
from jax.experimental.pallas import tpu as pltpu

F32 = jnp.float32
BF16 = jnp.bfloat16
I32 = jnp.int32

RWKV_HEAD_DIM = 64
RWKV_GN_EPS = 64e-5
DIFF_QK_DIM = 64
DIFF_V_DIM = 128
ROT_DIM = 16
ROPE_THETA = 500000.0
ATTN_SCALE = DIFF_QK_DIM ** -0.5
NEG_INF = -1e30
NORM_EPS = 1e-6
SUBLN_EPS = 1e-5
GMLP_LN_EPS = 1e-5
GMLP_GROUPS = 8
CHUNK = 128
N_EXP_GROUPS = 4
EXP_PER_GROUP = 8
LORA_W, LORA_A, LORA_G = 64, 64, 128

LANES = 128
MXU_TILE = 256
VMEM_LIMIT = 56 * 1024 * 1024
MOE_TILE = 256
HIGHEST = lax.Precision.HIGHEST


def _cparams(sem):
    return pltpu.CompilerParams(dimension_semantics=sem, vmem_limit_bytes=VMEM_LIMIT)


def _rmsnorm_body(x_ref, g_ref, o_ref, *, eps):
    x = x_ref[...]
    inv = lax.rsqrt(jnp.mean(x * x, axis=-1, keepdims=True) + eps)
    o_ref[...] = ((x * inv) * g_ref[...]).astype(o_ref.dtype)


def _rmsnorm(x, g, eps, out_dtype, tm):
    m, d = x.shape
    return pl.pallas_call(
        functools.partial(_rmsnorm_body, eps=eps),
        grid=(m // tm,),
        in_specs=[pl.BlockSpec((tm, d), lambda i: (i, 0)),
                  pl.BlockSpec((1, d), lambda i: (0, 0))],
        out_specs=pl.BlockSpec((tm, d), lambda i: (i, 0)),
        out_shape=jax.ShapeDtypeStruct((m, d), out_dtype),
        compiler_params=_cparams(("arbitrary",)),
        name="rmsnorm",
    )(x, g.reshape(1, d))


def _gelu_exact(x):
    return 0.5 * x * (1.0 + lax.erf(x * (1.0 / math.sqrt(2.0))))


def _mm_body(*refs, n_x, act, has_res):
    xs = refs[:n_x]
    w_ref = refs[n_x]
    res_ref = refs[n_x + 1] if has_res else None
    o_ref = refs[n_x + 1 + has_res]
    wb_ref = refs[n_x + 2 + has_res]

    @pl.when(pl.program_id(1) == 0)
    def _():
        wb_ref[...] = w_ref[...].astype(BF16)

    kx = w_ref.shape[0] // n_x
    acc = None
    for i, x_ref in enumerate(xs):
        part = jnp.dot(x_ref[...], wb_ref[i * kx:(i + 1) * kx, :], preferred_element_type=F32)
        acc = part if acc is None else acc + part
    if act == "gelu":
        acc = _gelu_exact(acc)
    if has_res:
        acc = acc + res_ref[...]
    o_ref[...] = acc.astype(o_ref.dtype)


def _matmul(xs, w, *, bm, bn, act=None, res=None, out_dtype=F32, name="matmul"):
    m = xs[0].shape[0]
    k, n = w.shape
    kx = k // len(xs)
    in_specs = [pl.BlockSpec((bm, kx), lambda j, i: (i, 0)) for _ in xs]
    in_specs.append(pl.BlockSpec((k, bn), lambda j, i: (0, j)))
    args = list(xs) + [w]
    if res is not None:
        in_specs.append(pl.BlockSpec((bm, bn), lambda j, i: (i, j)))
        args.append(res)
    return pl.pallas_call(
        functools.partial(_mm_body, n_x=len(xs), act=act, has_res=res is not None),
        grid=(n // bn, m // bm),
        in_specs=in_specs,
        out_specs=pl.BlockSpec((bm, bn), lambda j, i: (i, j)),
        out_shape=jax.ShapeDtypeStruct((m, n), out_dtype),
        scratch_shapes=[pltpu.VMEM((k, bn), BF16)],
        compiler_params=_cparams(("arbitrary", "arbitrary")),
        name=name,
    )(*args)


def _block_ones(n, seg):
    r = lax.broadcasted_iota(I32, (n, n), 0) // seg
    c = lax.broadcasted_iota(I32, (n, n), 1) // seg
    return jnp.where(r == c, 1.0, 0.0).astype(BF16)


def _seg_sum(x, ones_bd):
    hi = x.astype(BF16)
    lo = (x - hi.astype(F32)).astype(BF16)
    outs = []
    for c in range(x.shape[1] // MXU_TILE):
        sl = slice(c * MXU_TILE, (c + 1) * MXU_TILE)
        outs.append(jnp.dot(hi[:, sl], ones_bd, preferred_element_type=F32)
                    + jnp.dot(lo[:, sl], ones_bd, preferred_element_type=F32))
    return jnp.concatenate(outs, axis=1)


def _sigmoid(x):
    return 1.0 / (1.0 + jnp.exp(-x))


def _rwkv_pre_body(pa_ref, repl_ref, mu_ref, w0_ref, w2_ref, a0_ref, a2_ref, g2_ref,
                   kk_ref, ka_ref, rk_ref,
                   r_o, dec_o, k_o, v_o, kkn_o, b_o, g_o, bonus_o, *, period, d_a):
    pa = pa_ref[...]
    tm = pa.shape[0]
    row = lax.broadcasted_iota(I32, pa.shape, 0)
    prev = pltpu.roll(pa, 1, 0)
    repl = repl_ref[...]
    if repl.shape[0] != tm:
        repl = jnp.broadcast_to(repl[0:1, :], pa.shape)
    prev = jnp.where(row % period == 0, repl, prev)
    xm = pa + (prev - pa) * mu_ref[...]
    r = xm[:, 0:d_a]
    k = xm[:, d_a:2 * d_a]
    v = xm[:, 2 * d_a:3 * d_a]
    o = 3 * d_a
    wd = xm[:, o:o + LORA_W]
    ad = xm[:, o + LORA_W:o + LORA_W + LORA_A]
    gd = xm[:, o + LORA_W + LORA_A:o + LORA_W + LORA_A + LORA_G]

    z = w0_ref[...] + jnp.dot(jnp.tanh(wd), w2_ref[...], precision=HIGHEST, preferred_element_type=F32)
    w = jnp.minimum(z, 0.0) - jnp.log1p(jnp.exp(-jnp.abs(z))) - 0.5
    dec = jnp.exp(-jnp.exp(w))
    a = _sigmoid(a0_ref[...] + jnp.dot(ad, a2_ref[...], precision=HIGHEST, preferred_element_type=F32))
    g = jnp.dot(_sigmoid(gd).astype(BF16), g2_ref[...].astype(BF16), preferred_element_type=F32)

    ones_bd = _block_ones(MXU_TILE, RWKV_HEAD_DIM)
    kk = k * kk_ref[...]
    nrm = jnp.sqrt(_seg_sum(kk * kk, ones_bd))
    kkn = kk / jnp.maximum(nrm, 1e-12)
    kh = k * (1.0 + (a - 1.0) * ka_ref[...])
    bonus = _seg_sum(r * kh * rk_ref[...], ones_bd) * v

    r_o[...] = r
    dec_o[...] = dec
    k_o[...] = kh
    v_o[...] = v
    kkn_o[...] = kkn
    b_o[...] = kkn * a
    g_o[...] = g
    bonus_o[...] = bonus


def _rwkv_pre(proj, repl, period, row0, rows, tm, prm):
    d_a = prm["w0"].shape[1]
    d_ap = prm["mu"].shape[1]
    nt = rows // tm
    t0 = row0 // tm
    vec = lambda n: pl.BlockSpec((1, n), lambda i: (0, 0))
    full = lambda a: pl.BlockSpec(a.shape, lambda i: (0, 0))
    out_spec = pl.BlockSpec((tm, d_a), lambda i: (i, 0))
    outs = pl.pallas_call(
        functools.partial(_rwkv_pre_body, period=period, d_a=d_a),
        grid=(nt,),
        in_specs=[pl.BlockSpec((tm, d_ap), lambda i: (t0 + i, 0)),
                  pl.BlockSpec((repl.shape[0] // nt, d_ap), lambda i: (i, 0)),
                  vec(d_ap), vec(d_a), full(prm["w2"]), vec(d_a), full(prm["a2"]), full(prm["g2"]),
                  vec(d_a), vec(d_a), vec(d_a)],
        out_specs=[out_spec] * 8,
        out_shape=[jax.ShapeDtypeStruct((rows, d_a), F32)] * 8,
        compiler_params=_cparams(("arbitrary",)),
        name="rwkv_pre",
    )(proj, repl, prm["mu"], prm["w0"], prm["w2"], prm["a0"], prm["a2"], prm["g2"],
      prm["k_k"], prm["k_a"], prm["r_k"])
    return outs


def _rwkv_scan_body(*refs, tc, tp, nheads, nb, has_state):
    if has_state:
        s0_ref = refs[0]
        refs = refs[1:]
    (r_ref, w_ref, k_ref, v_ref, kk_ref, b_ref, g_ref, bonus_ref, lg_ref, lb_ref,
     o_ref, sout_ref, s_scr, vth_scr, ot_scr) = refs
    n = RWKV_HEAD_DIM
    c = pl.program_id(1)
    assert tp == 2 * n
    nh = nb * nheads
    heads = [(b, h) for b in range(nb) for h in range(nheads)]

    @pl.when(c == 0)
    def _():
        if has_state:
            zero = jnp.zeros((n, n), F32)
            for i, (b, h) in enumerate(heads):
                s0 = s0_ref[b, h]
                s_scr[i] = jnp.concatenate([s0, zero] if h % 2 == 0 else [zero, s0], axis=1)
        else:
            s_scr[...] = jnp.zeros_like(s_scr)

    for i, (b, h) in enumerate(heads):
        vh = v_ref[b, :, h * n:(h + 1) * n]
        if tp != tc:
            vh = jnp.concatenate([vh, jnp.zeros((tp - tc, n), F32)], axis=0)
        vth_scr[i * n:(i + 1) * n, :] = vh.T.astype(BF16)
    ot_scr[...] = jnp.zeros_like(ot_scr)

    trow = lax.broadcasted_iota(I32, (tp, 2 * n), 0)
    tcol = lax.broadcasted_iota(I32, (2 * n, tp), 1)
    head_par = lax.broadcasted_iota(I32, (nh, 1, 2 * n), 0) % 2
    lane_par = lax.broadcasted_iota(I32, (nh, 1, 2 * n), 2) // n
    own = head_par == lane_par

    sub = 8

    def per_head(ref, base, masked):
        xs = [ref[b, pl.ds(base, sub), :] for b in range(nb)]
        x = jnp.stack([xs[b][:, (h // 2) * 2 * n:(h // 2 + 1) * 2 * n] for b, h in heads], axis=0)
        return jnp.where(own, x, 0.0) if masked else x

    def group(tg, carry):
        base = pl.multiple_of(tg * sub, sub)
        kk8 = per_head(kk_ref, base, True)
        b8 = per_head(b_ref, base, True)
        k8 = per_head(k_ref, base, True)
        w8 = per_head(w_ref, base, False)
        r8 = per_head(r_ref, base, False)
        st = s_scr[...]
        vth = vth_scr[...]
        o8 = None
        for s in range(sub):
            t = base + s
            pick = jnp.where(trow == t, 1.0, 0.0).astype(BF16)
            v_col = jnp.dot(vth, pick, preferred_element_type=F32).reshape(nh, n, 2 * n)
            sa = -jnp.sum(st * kk8[:, s:s + 1], axis=-1, keepdims=True)
            st = st * w8[:, s:s + 1] + sa * b8[:, s:s + 1] + v_col * k8[:, s:s + 1]
            q = (st * r8[:, s:s + 1]).reshape(nh * n, 2 * n).astype(BF16)
            put = jnp.where(tcol == t, 1.0, 0.0).astype(BF16)
            o_t = jnp.dot(q, put, preferred_element_type=F32)
            o8 = o_t if o8 is None else o8 + o_t
        s_scr[...] = st
        ot_scr[...] = ot_scr[...] + o8.reshape(nh, n, tp)
        return carry

    lax.fori_loop(0, tc // sub, group, 0)

    for b in range(nb):
        for hp in range(nheads // 2):
            parts = []
            for h in (2 * hp, 2 * hp + 1):
                oc = ot_scr[b * nheads + h]
                mean = jnp.mean(oc, axis=0, keepdims=True)
                d = oc - mean
                var = jnp.mean(d * d, axis=0, keepdims=True)
                on = d * lax.rsqrt(var + RWKV_GN_EPS)
                parts.append(on.T[:tc])
            sl = slice(hp * 2 * n, (hp + 1) * 2 * n)
            on2 = jnp.concatenate(parts, axis=1)
            o = (on2 * lg_ref[:, sl] + lb_ref[:, sl] + bonus_ref[b, :, sl]) * g_ref[b, :, sl]
            o_ref[b, :, sl] = o.astype(o_ref.dtype)

    @pl.when(c == pl.num_programs(1) - 1)
    def _():
        for i, (b, h) in enumerate(heads):
            s_h = s_scr[i]
            sout_ref[b, h] = s_h[:, 0:n] if h % 2 == 0 else s_h[:, n:2 * n]


def _rwkv_scan(pre, lnx_g, lnx_b, state, nseq, tseq, tc, nb):
    rows, d_a = pre[0].shape
    nheads = d_a // RWKV_HEAD_DIM
    nch = tseq // tc
    tp = max(tc, LANES)
    n = RWKV_HEAD_DIM
    nh = nb * nheads
    tok = pl.BlockSpec((nb, tc, d_a), lambda s, c: (s, c, 0))
    vec = pl.BlockSpec((1, d_a), lambda s, c: (0, 0))
    st = pl.BlockSpec((nb, nheads, n, n), lambda s, c: (s, 0, 0, 0))
    has_state = state is not None
    in_specs = ([st] if has_state else []) + [tok] * 8 + [vec, vec]
    args = ([state] if has_state else []) + [a.reshape(nseq, tseq, d_a) for a in pre] + [
        lnx_g.reshape(1, d_a), lnx_b.reshape(1, d_a)]
    oa, s_fin = pl.pallas_call(
        functools.partial(_rwkv_scan_body, tc=tc, tp=tp, nheads=nheads, nb=nb, has_state=has_state),
        grid=(nseq // nb, nch),
        in_specs=in_specs,
        out_specs=[tok, st],
        out_shape=[jax.ShapeDtypeStruct((nseq, tseq, d_a), BF16 if tc % 16 == 0 else F32),
                   jax.ShapeDtypeStruct((nseq, nheads, n, n), F32)],
        scratch_shapes=[pltpu.VMEM((nh, n, 2 * n), F32),
                        pltpu.VMEM((nh * n, tp), BF16),
                        pltpu.VMEM((nh, n, tp), F32)],
        compiler_params=_cparams(("arbitrary", "arbitrary")),
        name="rwkv_scan",
    )(*args)
    return oa.reshape(rows, d_a), s_fin


def _rope_body(q_ref, k_ref, v_ref, c_ref, s1_ref, s2_ref, qs_o, qf_o, kf_o, kb_o, vb_o, kt_o):
    c, s1, s2 = c_ref[...], s1_ref[...], s2_ref[...]
    half = ROT_DIM // 2
    w = q_ref.shape[1]

    def rot(x):
        return x * c + pltpu.roll(x, w - half, 1) * s1 + pltpu.roll(x, half, 1) * s2

    qr = rot(q_ref[...]) * ATTN_SCALE
    qf_o[...] = qr
    qs_o[...] = qr.astype(BF16)
    kr = rot(k_ref[...])
    kf_o[...] = kr
    kb_o[...] = kr.astype(BF16)
    vb_o[...] = v_ref[...].astype(BF16)
    for hh in range(w // DIFF_QK_DIM):
        kt_o[hh] = kr[:, hh * DIFF_QK_DIM:(hh + 1) * DIFF_QK_DIM].T


def _rope(proj, pos, d_ap, d_q, tm, nseq_p, tseq_p):
    n_tok = proj.shape[0]
    w = MXU_TILE
    half = ROT_DIM // 2
    inv_freq = ROPE_THETA ** (-jnp.arange(half, dtype=F32) / half)
    ang = pos.astype(F32)[:, None] * inv_freq[None, :]
    cos, sin = jnp.cos(ang), jnp.sin(ang)
    pad = DIFF_QK_DIM - ROT_DIM
    ones = jnp.ones((n_tok, pad), F32)
    zeros = jnp.zeros((n_tok, pad), F32)
    zh = jnp.zeros((n_tok, half), F32)
    reps = w // DIFF_QK_DIM
    c_t = jnp.tile(jnp.concatenate([cos, cos, ones], axis=1), (1, reps))
    s1_t = jnp.tile(jnp.concatenate([-sin, zh, zeros], axis=1), (1, reps))
    s2_t = jnp.tile(jnp.concatenate([zh, sin, zeros], axis=1), (1, reps))
    nq = d_q // w
    qb, kb, vb = d_ap // w, d_ap // w + nq, d_ap // w + 2 * nq
    tab = pl.BlockSpec((tm, w), lambda i, j: (i, 0))
    out = pl.BlockSpec((tm, w), lambda i, j: (i, j))
    hpb = w // DIFF_QK_DIM
    tps = tseq_p // tm
    kt_spec = pl.BlockSpec((None, hpb, DIFF_QK_DIM, tm),
                           lambda i, j: (jnp.minimum(i // tps, nseq_p), j, 0, i % tps))
    return pl.pallas_call(
        _rope_body,
        grid=(n_tok // tm, nq),
        in_specs=[pl.BlockSpec((tm, w), lambda i, j: (i, qb + j)),
                  pl.BlockSpec((tm, w), lambda i, j: (i, kb + j)),
                  pl.BlockSpec((tm, w), lambda i, j: (i, vb + j)),
                  tab, tab, tab],
        out_specs=[out, out, out, out, out, kt_spec],
        out_shape=[jax.ShapeDtypeStruct((n_tok, d_q), BF16),
                   jax.ShapeDtypeStruct((n_tok, d_q), F32),
                   jax.ShapeDtypeStruct((n_tok, d_q), F32),
                   jax.ShapeDtypeStruct((n_tok, d_q), BF16),
                   jax.ShapeDtypeStruct((n_tok, d_q), BF16),
                   jax.ShapeDtypeStruct((nseq_p + 1, d_q // DIFF_QK_DIM, DIFF_QK_DIM, tseq_p), F32)],
        compiler_params=_cparams(("arbitrary", "arbitrary")),
        name="rope",
    )(proj, proj, proj, c_t, s1_t, s2_t)


def _lambda(lam_ref, lam_init):
    lp = lam_ref[...]
    l1 = jnp.sum(lp[0:1] * lp[1:2], axis=-1, keepdims=True)
    l2 = jnp.sum(lp[2:3] * lp[3:4], axis=-1, keepdims=True)
    return jnp.exp(l1) - jnp.exp(l2) + lam_init


def _subln(o, g_ref, lam_init):
    inv = lax.rsqrt(jnp.mean(o * o, axis=-1, keepdims=True) + SUBLN_EPS)
    return (o * inv) * g_ref[...] * (1.0 - lam_init)


def _online_softmax_update(s, v, m_scr, l_scr, acc_scr):
    m_prev = m_scr[...]
    m_new = jnp.maximum(m_prev, jnp.max(s, axis=-1, keepdims=True))
    alpha = jnp.exp(m_prev - m_new)
    p = jnp.exp(s - m_new[:, 0:1])
    l_scr[...] = alpha * l_scr[...] + jnp.sum(p, axis=-1, keepdims=True)
    acc_scr[...] = alpha[:, 0:1] * acc_scr[...] + jnp.dot(p.astype(BF16), v, preferred_element_type=F32)
    m_scr[...] = m_new


def _attn_prompt_body(q_ref, k_ref, v_ref, lam_ref, g_ref, o_ref, qq_scr, m_scr, l_scr, acc_scr,
                      *, tq, lam_init):
    qi = pl.program_id(2)
    ki = pl.program_id(3)

    @pl.when(ki == 0)
    def _():
        q = q_ref[...]
        lane = lax.broadcasted_iota(I32, q.shape, 1)
        zero = jnp.zeros_like(q)
        qq_scr[0:tq] = jnp.where(lane < DIFF_QK_DIM, q, zero)
        qq_scr[tq:2 * tq] = jnp.where(lane >= DIFF_QK_DIM, q, zero)
        m_scr[...] = jnp.full_like(m_scr, NEG_INF)
        l_scr[...] = jnp.zeros_like(l_scr)
        acc_scr[...] = jnp.zeros_like(acc_scr)

    rb = min(tq, 256)

    def tile(diagonal):
        for r0 in range(0, 2 * tq, rb):
            rows = slice(r0, r0 + rb)
            nk = min(tq, (r0 % tq) + rb) if diagonal else tq
            s = lax.dot_general(qq_scr[rows], k_ref[0:nk], (((1,), (1,)), ((), ())),
                                preferred_element_type=F32)
            if diagonal:
                row = (r0 % tq) + lax.broadcasted_iota(I32, s.shape, 0)
                col = lax.broadcasted_iota(I32, s.shape, 1)
                s = jnp.where(row >= col, s, NEG_INF)
            m_prev = m_scr[rows]
            m_new = jnp.maximum(m_prev, jnp.max(s, axis=-1, keepdims=True))
            alpha = jnp.exp(m_prev - m_new)
            p = jnp.exp(s - jnp.concatenate([m_new] * (nk // LANES), axis=1))
            l_scr[rows] = alpha * l_scr[rows] + jnp.sum(p, axis=-1, keepdims=True)
            acc_scr[rows] = alpha * acc_scr[rows] + jnp.dot(p.astype(BF16), v_ref[0:nk],
                                                            preferred_element_type=F32)
            m_scr[rows] = m_new

    @pl.when(ki < qi)
    def _():
        tile(False)

    @pl.when(ki == qi)
    def _():
        tile(True)
        lam = _lambda(lam_ref, lam_init)
        o1 = acc_scr[0:tq] / l_scr[0:tq, 0:1]
        o2 = acc_scr[tq:2 * tq] / l_scr[tq:2 * tq, 0:1]
        o_ref[...] = _subln(o1 - lam * o2, g_ref, lam_init).astype(o_ref.dtype)


def _attn_prompt(qs, kb, vb, lam_p, subln_g, nseq, tseq, tq, lam_init):
    rows, d_q = qs.shape
    nh = d_q // DIFF_V_DIM
    nq = tseq // tq
    w = DIFF_V_DIM
    return pl.pallas_call(
        functools.partial(_attn_prompt_body, tq=tq, lam_init=lam_init),
        grid=(nseq, nh, nq, nq),
        in_specs=[pl.BlockSpec((tq, w), lambda b, h, i, j: (b * nq + i, h)),
                  pl.BlockSpec((tq, w), lambda b, h, i, j: (b * nq + jnp.minimum(i, j), h)),
                  pl.BlockSpec((tq, w), lambda b, h, i, j: (b * nq + jnp.minimum(i, j), h)),
                  pl.BlockSpec(lam_p.shape, lambda b, h, i, j: (0, 0)),
                  pl.BlockSpec((1, w), lambda b, h, i, j: (0, 0))],
        out_specs=pl.BlockSpec((tq, w), lambda b, h, i, j: (b * nq + i, h)),
        out_shape=jax.ShapeDtypeStruct((nseq * tseq, d_q), BF16),
        scratch_shapes=[pltpu.VMEM((2 * tq, w), BF16),
                        pltpu.VMEM((2 * tq, LANES), F32),
                        pltpu.VMEM((2 * tq, LANES), F32),
                        pltpu.VMEM((2 * tq, w), F32)],
        compiler_params=_cparams(("arbitrary", "arbitrary", "arbitrary", "arbitrary")),
        name="attn_prompt",
    )(qs, kb, vb, lam_p, subln_g.reshape(1, w))


def _attn_sample_body(pt_ref, q_ref, *refs, ts, lam_init, pages_per_step):
    kc_refs = refs[:pages_per_step]
    vc_refs = refs[pages_per_step:2 * pages_per_step]
    kn_ref, vn_ref, lam_ref, g_ref, o_ref, qbd_scr, m_scr, l_scr, acc_scr = refs[2 * pages_per_step:]
    p = pl.program_id(1)
    nheads_qk = q_ref.shape[1] // DIFF_QK_DIM

    @pl.when(p == 0)
    def _():
        q = q_ref[...]
        head = lax.broadcasted_iota(I32, q.shape, 1) // DIFF_QK_DIM
        zero = jnp.zeros_like(q)
        rows = [jnp.where(head == m, q, zero) for m in range(nheads_qk)]
        qbd_scr[...] = jnp.concatenate(rows, axis=0).astype(BF16)
        m_scr[...] = jnp.full_like(m_scr, NEG_INF)
        l_scr[...] = jnp.zeros_like(l_scr)
        acc_scr[...] = jnp.zeros_like(acc_scr)

    nh_v = nheads_qk // 2
    w = DIFF_V_DIM

    def update(s, v_heads):
        m_prev = m_scr[...]
        m_new = jnp.maximum(m_prev, jnp.max(s, axis=-1, keepdims=True))
        alpha = jnp.exp(m_prev - m_new)
        pr = jnp.exp(s - jnp.concatenate([m_new] * (s.shape[1] // LANES), axis=1))
        l_scr[...] = alpha * l_scr[...] + jnp.sum(pr, axis=-1, keepdims=True)
        m_scr[...] = m_new
        pb = pr.astype(BF16)
        for h in range(nh_v):
            rows = slice(2 * h * ts, (2 * h + 2) * ts)
            acc_scr[rows] = alpha[rows] * acc_scr[rows] + jnp.dot(pb[rows], v_heads[h],
                                                                  preferred_element_type=F32)

    kt = jnp.concatenate([r[...].reshape(nheads_qk * DIFF_QK_DIM, r.shape[2]) for r in kc_refs],
                         axis=1).astype(BF16)
    s_past = jnp.dot(qbd_scr[...], kt, preferred_element_type=F32)
    npos = vc_refs[0].shape[0] // nh_v
    v_past = [jnp.concatenate([r[pl.ds(h, npos, stride=nh_v), :] for r in vc_refs], axis=0).astype(BF16)
              for h in range(nh_v)]
    update(s_past, v_past)

    @pl.when(p == pl.num_programs(1) - 1)
    def _():
        padk = jnp.zeros((LANES - ts, kn_ref.shape[1]), F32)
        kn = jnp.concatenate([kn_ref[...], padk], axis=0).astype(BF16)
        vn = jnp.concatenate([vn_ref[...], padk], axis=0).astype(BF16)
        s = lax.dot_general(qbd_scr[...], kn, (((1,), (1,)), ((), ())), preferred_element_type=F32)
        row = lax.broadcasted_iota(I32, s.shape, 0) % ts
        col = lax.broadcasted_iota(I32, s.shape, 1)
        s = jnp.where(row >= col, s, NEG_INF)
        update(s, [vn[:, h * w:(h + 1) * w] for h in range(nh_v)])
        lam = _lambda(lam_ref, lam_init)
        for h in range(nh_v):
            r1 = slice(2 * h * ts, (2 * h + 1) * ts)
            r2 = slice((2 * h + 1) * ts, (2 * h + 2) * ts)
            o1 = acc_scr[r1] / l_scr[r1, 0:1]
            o2 = acc_scr[r2] / l_scr[r2, 0:1]
            o_ref[:, h * w:(h + 1) * w] = _subln(o1 - lam * o2, g_ref, lam_init).astype(o_ref.dtype)


def _attn_sample(qs, kb, vb, cache_k, cache_v, layer_j, page_flat, lam_p, subln_g, nb, ts, npages, lam_init):
    d_q = qs.shape[1]
    nrow = (d_q // DIFF_QK_DIM) * ts
    gpp = next(g for g in (8, 4, 2, 1) if npages % g == 0)
    tok = pl.BlockSpec((ts, d_q), lambda b, p, pt: (b, 0))
    cache_k = jnp.transpose(cache_k, (0, 1, 3, 4, 2))
    cache_v = cache_v.reshape(cache_v.shape[:2] + (-1, cache_v.shape[-1]))

    def page(cache, g):
        return pl.BlockSpec((None, None) + cache.shape[2:],
                            lambda b, p, pt: (layer_j, pt[b * npages + p * gpp + g]) + (0,) * (cache.ndim - 2))

    return pl.pallas_call(
        functools.partial(_attn_sample_body, ts=ts, lam_init=lam_init, pages_per_step=gpp),
        grid_spec=pltpu.PrefetchScalarGridSpec(
            num_scalar_prefetch=1,
            grid=(nb, npages // gpp),
            in_specs=[tok] + [page(cache_k, g) for g in range(gpp)] + [page(cache_v, g) for g in range(gpp)]
                     + [tok, tok,
                        pl.BlockSpec(lam_p.shape, lambda b, p, pt: (0, 0)),
                        pl.BlockSpec((1, DIFF_V_DIM), lambda b, p, pt: (0, 0))],
            out_specs=tok,
            scratch_shapes=[pltpu.VMEM((nrow, d_q), BF16),
                            pltpu.VMEM((nrow, LANES), F32),
                            pltpu.VMEM((nrow, LANES), F32),
                            pltpu.VMEM((nrow, DIFF_V_DIM), F32)]),
        out_shape=jax.ShapeDtypeStruct((nb * ts, d_q), F32),
        compiler_params=_cparams(("arbitrary", "arbitrary")),
        name="attn_sample",
    )(page_flat, qs, *([cache_k] * gpp), *([cache_v] * gpp), kb, vb, lam_p, subln_g.reshape(1, DIFF_V_DIM))


def _gmlp_body(z_ref, lg_ref, lb_ref, wm_ref, bm_ref, y_ref, *maybe_v, d_c):
    u = z_ref[:, 0:d_c]
    v = z_ref[:, d_c:2 * d_c]
    mean = jnp.mean(v, axis=-1, keepdims=True)
    d = v - mean
    var = jnp.mean(d * d, axis=-1, keepdims=True)
    vn = (d * lax.rsqrt(var + GMLP_LN_EPS)) * lg_ref[...] + lb_ref[...]
    if maybe_v:
        maybe_v[0][...] = vn
    gd = d_c // GMLP_GROUPS
    vb = vn.astype(BF16)
    for g in range(GMLP_GROUPS):
        sl = slice(g * gd, (g + 1) * gd)
        mixed = jnp.dot(wm_ref[g].astype(BF16), vb[:, sl], preferred_element_type=F32) + bm_ref[g]
        y_ref[:, sl] = (u[:, sl] * mixed).astype(y_ref.dtype)


def _gmlp(z, row0, rows, ln_g, ln_b, wmix, bmix, emit_v):
    d_c = z.shape[1] // 2
    tm = CHUNK
    t0 = row0 // tm
    gd = d_c // GMLP_GROUPS
    out_specs = [pl.BlockSpec((tm, d_c), lambda i: (i, 0))]
    out_shape = [jax.ShapeDtypeStruct((rows, d_c), BF16)]
    if emit_v:
        out_specs.append(pl.BlockSpec((tm, d_c), lambda i: (i, 0)))
        out_shape.append(jax.ShapeDtypeStruct((rows, d_c), F32))
    return pl.pallas_call(
        functools.partial(_gmlp_body, d_c=d_c),
        grid=(rows // tm,),
        in_specs=[pl.BlockSpec((tm, 2 * d_c), lambda i: (t0 + i, 0)),
                  pl.BlockSpec((1, d_c), lambda i: (0, 0)),
                  pl.BlockSpec((1, d_c), lambda i: (0, 0)),
                  pl.BlockSpec((GMLP_GROUPS, tm, tm), lambda i: (0, 0, 0)),
                  pl.BlockSpec((GMLP_GROUPS, tm, gd), lambda i: (0, 0, 0))],
        out_specs=out_specs,
        out_shape=out_shape,
        compiler_params=_cparams(("arbitrary",)),
        name="gmlp_gate",
    )(z, ln_g.reshape(1, d_c), ln_b.reshape(1, d_c), wmix, bmix)


def _store_token_rows(ref2, x2d):
    dc = x2d.shape[1] // LANES
    for c in range(dc):
        ref2[pl.ds(c, x2d.shape[0], stride=dc), :] = x2d[:, c * LANES:(c + 1) * LANES]


def _load_token_rows(ref2, dc):
    n = ref2.shape[0] // dc
    return jnp.concatenate([ref2[pl.ds(c, n, stride=dc), :] for c in range(dc)], axis=1)


def _router_body(h_ref, g_ref, wr_ref, br_ref, xn_o, idx_o, cw_o):
    x = h_ref[...]
    inv = lax.rsqrt(jnp.mean(x * x, axis=-1, keepdims=True) + NORM_EPS)
    xn = (x * inv) * g_ref[...]
    _store_token_rows(xn_o, xn)
    logits = jnp.dot(xn, wr_ref[...], precision=HIGHEST, preferred_element_type=F32) + br_ref[...]
    lane = lax.broadcasted_iota(I32, logits.shape, 1)
    big = jnp.int32(1 << 20)
    ninf = jnp.float32(-jnp.inf)

    def first_max(vals):
        mx = jnp.max(vals, axis=-1, keepdims=True)
        ix = jnp.min(jnp.where(vals == mx, lane, big), axis=-1, keepdims=True)
        return mx, ix

    gl = jnp.where(lane < N_EXP_GROUPS, logits, ninf)
    gmax, gtop = first_max(gl)
    gate = 1.0 / jnp.sum(jnp.exp(gl - gmax), axis=-1, keepdims=True)
    lo = N_EXP_GROUPS + EXP_PER_GROUP * gtop
    el = jnp.where((lane >= lo) & (lane < lo + EXP_PER_GROUP), logits, ninf)
    m1, i1 = first_max(el)
    m2, i2 = first_max(jnp.where(lane == i1, ninf, el))
    p2 = jnp.exp(m2 - m1)
    c1 = gate / (1.0 + p2)
    c2 = gate * p2 / (1.0 + p2)
    idx_o[...] = jnp.where(lane == 0, i1 - N_EXP_GROUPS, jnp.where(lane == 1, i2 - N_EXP_GROUPS, 0))
    cw_o[...] = jnp.where(lane == 0, c1, jnp.where(lane == 1, c2, 0.0))


def _router(h, g, w_grp, b_grp, w_exp, b_exp, tm):
    n_tok, d = h.shape
    ncol = w_grp.shape[1] + w_exp.shape[1]
    wr = jnp.concatenate([w_grp, w_exp, jnp.zeros((d, LANES - ncol), F32)], axis=1)
    br = jnp.concatenate([b_grp, b_exp, jnp.zeros((LANES - ncol,), F32)]).reshape(1, LANES)
    tok = lambda w: pl.BlockSpec((tm, w), lambda i: (i, 0))
    return pl.pallas_call(
        _router_body,
        grid=(n_tok // tm,),
        in_specs=[tok(d), pl.BlockSpec((1, d), lambda i: (0, 0)),
                  pl.BlockSpec((d, LANES), lambda i: (0, 0)),
                  pl.BlockSpec((1, LANES), lambda i: (0, 0))],
        out_specs=[pl.BlockSpec((tm * (d // LANES), LANES), lambda i: (i, 0)), tok(LANES), tok(LANES)],
        out_shape=[jax.ShapeDtypeStruct((n_tok * (d // LANES), LANES), F32),
                   jax.ShapeDtypeStruct((n_tok, LANES), I32),
                   jax.ShapeDtypeStruct((n_tok, LANES), F32)],
        compiler_params=_cparams(("arbitrary",)),
        name="moe_router",
    )(h, g.reshape(1, d), wr, br)


def _gather_copy(x_hbm, xbuf, sem, slot, i, tok, dc):
    src = x_hbm.at[pl.ds(pl.multiple_of(tok * dc, dc), dc)]
    dst = xbuf.at[slot, pl.ds(pl.multiple_of(i * dc, dc), dc)]
    return pltpu.make_async_copy(src, dst, sem.at[slot])


def _moe_ffn_body(te_ref, nvalid_ref, tok_ref, x_hbm, rw_ref, w1_ref, w3_ref, w2_ref, y_ref,
                  xbuf, sem, w1b, w3b, w2b, *, tm):
    t = pl.program_id(0)
    nv = nvalid_ref[0]
    valid = t < nv
    dc = w1_ref.shape[0] // LANES

    def issue(tile, slot):
        def body(i, c):
            _gather_copy(x_hbm, xbuf, sem, slot, i, tok_ref[tile * tm + i], dc).start()
            return c
        lax.fori_loop(0, tm, body, 0, unroll=8)

    @pl.when((t == 0) & valid)
    def _():
        issue(0, 0)

    @pl.when(t + 1 < nv)
    def _():
        issue(t + 1, (t + 1) % 2)

    @pl.when(valid)
    def _():
        prev = te_ref[jnp.maximum(t - 1, 0)]

        @pl.when((t == 0) | (te_ref[t] != prev))
        def _():
            w1b[...] = w1_ref[...].astype(BF16)
            w3b[...] = w3_ref[...].astype(BF16)
            w2b[...] = w2_ref[...].astype(BF16)

        slot = t % 2

        pltpu.make_async_copy(x_hbm.at[pl.ds(0, tm * dc)], xbuf.at[slot], sem.at[slot]).wait()

        x = _load_token_rows(xbuf.at[slot], dc).astype(BF16)
        h1 = jnp.dot(x, w1b[...], preferred_element_type=F32)
        h3 = jnp.dot(x, w3b[...], preferred_element_type=F32)
        hh = (h1 * _sigmoid(h1)) * h3 * rw_ref[...]
        _store_token_rows(y_ref, jnp.dot(hh.astype(BF16), w2b[...], preferred_element_type=F32))

    @pl.when(jnp.logical_not(valid))
    def _():
        y_ref[...] = jnp.zeros_like(y_ref)


def _moe_ffn(xn, row_tok, row_w, tile_exp, nvalid_tiles, w1, w3, w2, layer, n_tiles, tm):
    d, f = w1.shape[2], w1.shape[3]
    dc = d // LANES
    rows = n_tiles * tm
    def last_valid(t, nv):
        return jnp.minimum(t, jnp.maximum(nv[0] - 1, 0))
    return pl.pallas_call(
        functools.partial(_moe_ffn_body, tm=tm),
        grid_spec=pltpu.PrefetchScalarGridSpec(
            num_scalar_prefetch=3,
            grid=(n_tiles,),
            in_specs=[pl.BlockSpec(memory_space=pl.ANY),
                      pl.BlockSpec((tm, 1), lambda t, te, nv, tk: (last_valid(t, nv), 0)),
                      pl.BlockSpec((None, None, d, f), lambda t, te, nv, tk: (layer, te[t], 0, 0)),
                      pl.BlockSpec((None, None, d, f), lambda t, te, nv, tk: (layer, te[t], 0, 0)),
                      pl.BlockSpec((None, None, f, d), lambda t, te, nv, tk: (layer, te[t], 0, 0))],
            out_specs=pl.BlockSpec((tm * dc, LANES), lambda t, te, nv, tk: (t, 0)),
            scratch_shapes=[pltpu.VMEM((2, tm * dc, LANES), F32), pltpu.SemaphoreType.DMA((2,)),
                            pltpu.VMEM((d, f), BF16), pltpu.VMEM((d, f), BF16), pltpu.VMEM((f, d), BF16)]),
        out_shape=jax.ShapeDtypeStruct((rows * dc, LANES), F32),
        compiler_params=_cparams(("arbitrary",)),
        name="moe_ffn",
    )(tile_exp, nvalid_tiles, row_tok, xn, row_w, w1, w3, w2)


def _combine_copy(y_hbm, buf, sem, slot, k, i, row, dc):
    src = y_hbm.at[pl.ds(pl.multiple_of(row * dc, dc), dc)]
    dst = buf.at[slot, k, pl.ds(pl.multiple_of(i * dc, dc), dc)]
    return pltpu.make_async_copy(src, dst, sem.at[slot])


def _moe_combine_body(pos_ref, h_ref, y_hbm, *rest, tm, final_eps):
    if final_eps is None:
        o_ref, buf, sem = rest
    else:
        gf_ref, o_ref, on_ref, buf, sem = rest
    t = pl.program_id(0)
    nt = pl.num_programs(0)
    dc = h_ref.shape[1] // LANES

    def issue(tile, slot):
        def body(i, c):
            a = 2 * (tile * tm + i)
            _combine_copy(y_hbm, buf, sem, slot, 0, i, pos_ref[a], dc).start()
            _combine_copy(y_hbm, buf, sem, slot, 1, i, pos_ref[a + 1], dc).start()
            return c
        lax.fori_loop(0, tm, body, 0, unroll=8)

    @pl.when(t == 0)
    def _():
        issue(0, 0)

    @pl.when(t + 1 < nt)
    def _():
        issue(t + 1, (t + 1) % 2)

    slot = t % 2

    for k in range(2):
        pltpu.make_async_copy(y_hbm.at[pl.ds(0, tm * dc)], buf.at[slot, k], sem.at[slot]).wait()

    out = h_ref[...] + _load_token_rows(buf.at[slot, 0], dc) + _load_token_rows(buf.at[slot, 1], dc)
    o_ref[...] = out
    if final_eps is not None:
        inv = lax.rsqrt(jnp.mean(out * out, axis=-1, keepdims=True) + final_eps)
        on_ref[...] = (out * inv) * gf_ref[...]


def _moe_combine(h, ys, pos, tm, final_g=None):
    n_tok, d = h.shape
    tok = pl.BlockSpec((tm, d), lambda i, p: (i, 0))
    in_specs = [tok, pl.BlockSpec(memory_space=pl.ANY)]
    args = [pos, h, ys]
    out_specs = [tok]
    out_shape = [jax.ShapeDtypeStruct((n_tok, d), F32)]
    if final_g is not None:
        in_specs.append(pl.BlockSpec((1, d), lambda i, p: (0, 0)))
        args.append(final_g.reshape(1, d))
        out_specs.append(tok)
        out_shape.append(jax.ShapeDtypeStruct((n_tok, d), F32))
    return pl.pallas_call(
        functools.partial(_moe_combine_body, tm=tm, final_eps=None if final_g is None else NORM_EPS),
        grid_spec=pltpu.PrefetchScalarGridSpec(
            num_scalar_prefetch=1,
            grid=(n_tok // tm,),
            in_specs=in_specs,
            out_specs=out_specs,
            scratch_shapes=[pltpu.VMEM((2, 2, tm * (d // LANES), LANES), F32), pltpu.SemaphoreType.DMA((2,))]),
        out_shape=out_shape,
        compiler_params=_cparams(("arbitrary",)),
        name="moe_combine",
    )(*args)


def _hier_moe(h, g, w_grp, b_grp, w_exp, b_exp, w1, w3, w2, layer, final_g=None):
    n_tok, d = h.shape
    n_exp = w1.shape[1]
    tm = MOE_TILE
    xn, idx, cw = _router(h, g, w_grp, b_grp, w_exp, b_exp, tm=256)

    n_asg = 2 * n_tok
    flat_e = idx[:, :2].reshape(n_asg)
    flat_c = cw[:, :2].reshape(n_asg)
    onehot = (flat_e[:, None] == jnp.arange(n_exp, dtype=I32)[None, :]).astype(I32)
    csum = jnp.cumsum(onehot, axis=0)
    counts = csum[-1]
    rank = jnp.take_along_axis(csum - onehot, flat_e[:, None], axis=1)[:, 0]
    padded = ((counts + tm - 1) // tm) * tm
    ends = jnp.cumsum(padded)
    pos = (ends - padded)[flat_e] + rank
    n_tiles = n_asg // tm + n_exp
    rows = n_tiles * tm
    upd = jnp.stack([jnp.arange(n_asg, dtype=I32) // 2, lax.bitcast_convert_type(flat_c, I32)], axis=1)
    row_meta = jnp.zeros((rows, 2), I32).at[pos].set(upd)
    row_tok = row_meta[:, 0]
    row_w = lax.bitcast_convert_type(row_meta[:, 1], F32).reshape(rows, 1)
    nvalid = (ends[-1] // tm).astype(I32).reshape(1)
    tile_ids = jnp.minimum(jnp.arange(n_tiles, dtype=I32), nvalid - 1)
    tile_exp = jnp.searchsorted(ends, tile_ids * tm, side="right").astype(I32)

    ys = _moe_ffn(xn, row_tok, row_w, tile_exp, nvalid, w1, w3, w2, layer, n_tiles, tm)
    return _moe_combine(h, ys, pos.astype(I32), 128, final_g)


def kernel(x_prompt, x_sample, cache_k, cache_v, page_table, state_wkv, state_shift, norm_mix, norm_ffn, norm_final, even_w_in, even_w_out, rwkv_mu, rwkv_w0, rwkv_w2, rwkv_a0, rwkv_a2, rwkv_g2, rwkv_k_k, rwkv_k_a, rwkv_r_k, rwkv_lnx_g, rwkv_lnx_b, diff_lam_q1, diff_lam_k1, diff_lam_q2, diff_lam_k2, diff_subln_g, gmlp_w_in, gmlp_ln_g, gmlp_ln_b, gmlp_w_s, gmlp_b_s, gmlp_w_out, moe_w_grp, moe_b_grp, moe_w_exp, moe_b_exp, moe_w1, moe_w3, moe_w2):
    bp, tp, d = x_prompt.shape
    bs, ts, _ = x_sample.shape
    n_p, n_s = bp * tp, bs * ts
    n_tok = n_p + n_s
    depth = norm_mix.shape[0]
    npages, psz = page_table.shape[1], cache_k.shape[2]
    past_len = npages * psz
    d_a = rwkv_w0.shape[1]
    d_ap = rwkv_mu.shape[1]
    d_q = cache_k.shape[3] * cache_k.shape[4]
    n_pool = cache_k.shape[1]
    nheads = d_a // RWKV_HEAD_DIM
    tm = 512 if n_p % 512 == 0 and n_s % 512 == 0 else 128
    tq = next((t for t in (1024, 512) if tp % t == 0), 128)
    tc = min(tp, CHUNK)

    h = jnp.concatenate([x_prompt.reshape(n_p, d), x_sample.reshape(n_s, d)], axis=0)
    pos = jnp.concatenate([jnp.tile(jnp.arange(tp), bp), jnp.tile(past_len + jnp.arange(ts), bs)])
    moe = lambda l, hh, fg=None: _hier_moe(hh, norm_ffn[l], moe_w_grp[l], moe_b_grp[l], moe_w_exp[l],
                                          moe_b_exp[l], moe_w1, moe_w3, moe_w2, l, fg)

    outs = {k: [] for k in ("k_p", "v_p", "wkv_p", "sh_p", "k_s", "v_s", "wkv_s", "sh_s", "gv_s")}
    y_norm = None
    for layer in range(depth):
        j = layer // 2
        xn = _rmsnorm(h, norm_mix[layer], NORM_EPS, BF16, tm)
        last = layer == depth - 1
        if layer % 2 == 0:
            proj = _matmul([xn], even_w_in[j], bm=tm, bn=even_w_in.shape[2] // 5, name="even_in_proj")
            prm = dict(mu=rwkv_mu[j].reshape(1, d_ap), w0=rwkv_w0[j].reshape(1, d_a), w2=rwkv_w2[j],
                       a0=rwkv_a0[j].reshape(1, d_a), a2=rwkv_a2[j], g2=rwkv_g2[j],
                       k_k=rwkv_k_k[j].reshape(1, d_a), k_a=rwkv_k_a[j].reshape(1, d_a),
                       r_k=rwkv_r_k[j].reshape(1, d_a))
            tmr = 256 if tp % 256 == 0 else 128
            tile_last = proj[tmr - 1:n_p:tmr, :d_ap]
            seq_last_s = proj[n_p + ts - 1::ts, :d_ap]
            starts_seq = (jnp.arange(n_p // tmr) % (tp // tmr) == 0)[:, None]
            bnd = jnp.concatenate([jnp.zeros((1, d_ap), F32), tile_last[:-1]], axis=0)
            bnd = jnp.where(starts_seq, 0.0, bnd)
            repl_p = jnp.zeros((n_p // tmr, 8, d_ap), F32).at[:, 0].set(bnd)
            repl_p = repl_p.reshape(n_p // tmr * 8, d_ap)
            repl_s = jnp.repeat(state_shift[j], ts, axis=0)
            pre_p = _rwkv_pre(proj, repl_p, tmr, 0, n_p, tmr, prm)
            pre_s = _rwkv_pre(proj, repl_s, ts, n_p, n_s, tmr, prm)
            oa_p, wkv_p = _rwkv_scan(pre_p, rwkv_lnx_g[j], rwkv_lnx_b[j], None, bp, tp, tc,
                                     2 if bp % 2 == 0 else 1)
            oa_s, wkv_s = _rwkv_scan(pre_s, rwkv_lnx_g[j], rwkv_lnx_b[j], state_wkv[j], bs, ts, ts,
                                     2 if bs % 2 == 0 else 1)

            qs, qf, kf, kb, vb, kt = _rope(proj, pos, d_ap, d_q, tm, bp, tp)
            vf = proj[:, d_ap + 2 * d_q:]
            lam_init = 0.8 - 0.6 * math.exp(-0.3 * layer)
            lam_p = jnp.stack([diff_lam_q1[j], diff_lam_k1[j], diff_lam_q2[j], diff_lam_k2[j]])
            ob_p = _attn_prompt(qs[:n_p], kb[:n_p], vb[:n_p], lam_p, diff_subln_g[j], bp, tp, tq, lam_init)
            ob_s = _attn_sample(qf[n_p:], kf[n_p:], vf[n_p:], cache_k, cache_v, j,
                                page_table.reshape(-1).astype(I32),
                                lam_p, diff_subln_g[j], bs, ts, npages, lam_init)
            oa = jnp.concatenate([oa_p, oa_s.astype(BF16)], axis=0)
            ob = jnp.concatenate([ob_p, ob_s.astype(BF16)], axis=0)
            h = _matmul([oa, ob], even_w_out[j], bm=tm, bn=1024, res=h, name="even_out_proj")

            outs["k_p"].append(jnp.transpose(kt[:bp], (0, 3, 1, 2)))
            outs["v_p"].append(vf[:n_p].reshape(bp, tp, -1, DIFF_V_DIM))
            outs["wkv_p"].append(wkv_p)
            outs["sh_p"].append(tile_last[tp // tmr - 1::tp // tmr])
            outs["k_s"].append(kf[n_p:].reshape(bs, ts, -1, DIFF_QK_DIM))
            outs["v_s"].append(vf[n_p:].reshape(bs, ts, -1, DIFF_V_DIM))
            outs["wkv_s"].append(wkv_s)
            outs["sh_s"].append(seq_last_s)
        else:
            d_c = gmlp_ln_g.shape[1]
            z = _matmul([xn], gmlp_w_in[j], bm=tm, bn=1024, act="gelu", name="gmlp_in_proj")
            ws = gmlp_w_s[j]
            bsb = gmlp_b_s[j]
            gd = d_c // GMLP_GROUPS
            lp = min(tp, CHUNK)
            wm_p = jnp.tril(ws[:, :lp, :lp])
            bm_p = jnp.broadcast_to(bsb[:, :lp, None], (GMLP_GROUPS, lp, gd))
            rep = CHUNK // ts
            eye = jnp.eye(rep, dtype=F32)
            wm_s = jnp.einsum("ab,gts->gatbs", eye, jnp.tril(ws[:, :ts, :ts])).reshape(GMLP_GROUPS, CHUNK, CHUNK)
            bm_s = jnp.broadcast_to(jnp.tile(bsb[:, :ts], (1, rep))[:, :, None], (GMLP_GROUPS, CHUNK, gd))
            (y_p,) = _gmlp(z, 0, n_p, gmlp_ln_g[j], gmlp_ln_b[j], wm_p, bm_p, False)
            y_s, v_rows = _gmlp(z, n_p, n_s, gmlp_ln_g[j], gmlp_ln_b[j], wm_s, bm_s, True)
            yin = jnp.concatenate([y_p, y_s], axis=0)
            h = _matmul([yin], gmlp_w_out[j], bm=tm, bn=1024, res=h, name="gmlp_out_proj")
            outs["gv_s"].append(v_rows.reshape(bs, ts, d_c))
        res = moe(layer, h, norm_final if last else None)
        h = res[0]
        if last:
            y_norm = res[1]

    y_prompt = y_norm[:n_p].reshape(bp, tp, d)
    y_sample = y_norm[n_p:].reshape(bs, ts, d)
    st = lambda k: jnp.stack(outs[k])
    return (y_prompt, y_sample, st("k_p"), st("v_p"), st("wkv_p"), st("sh_p"),
            st("k_s"), st("v_s"), st("wkv_s"), st("sh_s"), st("gv_s"))
```

```python
import functools
import math

import jax
import jax.numpy as jnp
from jax import lax
from jax.experimental import pallas as pl
from jax.experimental.pallas import tpu as pltpu

F32 = jnp.float32
BF16 = jnp.bfloat16
I32 = jnp.int32

RWKV_HEAD_DIM = 64
RWKV_GN_EPS = 64e-5
DIFF_QK_DIM = 64
DIFF_V_DIM = 128
ROT_DIM = 16
ROPE_THETA = 500000.0
ATTN_SCALE = DIFF_QK_DIM ** -0.5
NEG_INF = -1e30
NORM_EPS = 1e-6
SUBLN_EPS = 1e-5
GMLP_LN_EPS = 1e-5
GMLP_GROUPS = 8
CHUNK = 128
N_EXP_GROUPS = 4
EXP_PER_GROUP = 8
LORA_W, LORA_A, LORA_G = 64, 64, 128

LANES = 128
MXU_TILE = 256
VMEM_LIMIT = 56 * 1024 * 1024
MOE_TILE = 256
HIGHEST = lax.Precision.HIGHEST


def _cparams(sem):
    return pltpu.CompilerParams(dimension_semantics=sem, vmem_limit_bytes=VMEM_LIMIT)


def _rmsnorm_body(x_ref, g_ref, o_ref, *, eps):
    x = x_ref[...]
    inv = lax.rsqrt(jnp.mean(x * x, axis=-1, keepdims=True) + eps)
    o_ref[...] = ((x * inv) * g_ref[...]).astype(o_ref.dtype)


def _rmsnorm(x, g, eps, out_dtype, tm):
    m, d = x.shape
    return pl.pallas_call(
        functools.partial(_rmsnorm_body, eps=eps),
        grid=(m // tm,),
        in_specs=[pl.BlockSpec((tm, d), lambda i: (i, 0)),
                  pl.BlockSpec((1, d), lambda i: (0, 0))],
        out_specs=pl.BlockSpec((tm, d), lambda i: (i, 0)),
        out_shape=jax.ShapeDtypeStruct((m, d), out_dtype),
        compiler_params=_cparams(("arbitrary",)),
        name="rmsnorm",
    )(x, g.reshape(1, d))


def _gelu_exact(x):
    return 0.5 * x * (1.0 + lax.erf(x * (1.0 / math.sqrt(2.0))))


def _mm_body(*refs, n_x, act, has_res):
    xs = refs[:n_x]
    w_ref = refs[n_x]
    res_ref = refs[n_x + 1] if has_res else None
    o_ref = refs[n_x + 1 + has_res]
    wb_ref = refs[n_x + 2 + has_res]

    @pl.when(pl.program_id(1) == 0)
    def _():
        wb_ref[...] = w_ref[...].astype(BF16)

    kx = w_ref.shape[0] // n_x
    acc = None
    for i, x_ref in enumerate(xs):
        part = jnp.dot(x_ref[...], wb_ref[i * kx:(i + 1) * kx, :], preferred_element_type=F32)
        acc = part if acc is None else acc + part
    if act == "gelu":
        acc = _gelu_exact(acc)
    if has_res:
        acc = acc + res_ref[...]
    o_ref[...] = acc.astype(o_ref.dtype)


def _matmul(xs, w, *, bm, bn, act=None, res=None, out_dtype=F32, name="matmul"):
    m = xs[0].shape[0]
    k, n = w.shape
    kx = k // len(xs)
    in_specs = [pl.BlockSpec((bm, kx), lambda j, i: (i, 0)) for _ in xs]
    in_specs.append(pl.BlockSpec((k, bn), lambda j, i: (0, j)))
    args = list(xs) + [w]
    if res is not None:
        in_specs.append(pl.BlockSpec((bm, bn), lambda j, i: (i, j)))
        args.append(res)
    return pl.pallas_call(
        functools.partial(_mm_body, n_x=len(xs), act=act, has_res=res is not None),
        grid=(n // bn, m // bm),
        in_specs=in_specs,
        out_specs=pl.BlockSpec((bm, bn), lambda j, i: (i, j)),
        out_shape=jax.ShapeDtypeStruct((m, n), out_dtype),
        scratch_shapes=[pltpu.VMEM((k, bn), BF16)],
        compiler_params=_cparams(("arbitrary", "arbitrary")),
        name=name,
    )(*args)


def _block_ones(n, seg):
    r = lax.broadcasted_iota(I32, (n, n), 0) // seg
    c = lax.broadcasted_iota(I32, (n, n), 1) // seg
    return jnp.where(r == c, 1.0, 0.0).astype(BF16)


def _seg_sum(x, ones_bd):
    hi = x.astype(BF16)
    lo = (x - hi.astype(F32)).astype(BF16)
    outs = []
    for c in range(x.shape[1] // MXU_TILE):
        sl = slice(c * MXU_TILE, (c + 1) * MXU_TILE)
        outs.append(jnp.dot(hi[:, sl], ones_bd, preferred_element_type=F32)
                    + jnp.dot(lo[:, sl], ones_bd, preferred_element_type=F32))
    return jnp.concatenate(outs, axis=1)


def _sigmoid(x):
    return 1.0 / (1.0 + jnp.exp(-x))


def _rwkv_pre_body(pa_ref, repl_ref, mu_ref, w0_ref, w2_ref, a0_ref, a2_ref, g2_ref,
                   kk_ref, ka_ref, rk_ref,
                   r_o, dec_o, k_o, v_o, kkn_o, b_o, g_o, bonus_o, *, period, d_a):
    pa = pa_ref[...]
    tm = pa.shape[0]
    row = lax.broadcasted_iota(I32, pa.shape, 0)
    prev = pltpu.roll(pa, 1, 0)
    repl = repl_ref[...]
    if repl.shape[0] != tm:
        repl = jnp.broadcast_to(repl[0:1, :], pa.shape)
    prev = jnp.where(row % period == 0, repl, prev)
    xm = pa + (prev - pa) * mu_ref[...]
    r = xm[:, 0:d_a]
    k = xm[:, d_a:2 * d_a]
    v = xm[:, 2 * d_a:3 * d_a]
    o = 3 * d_a
    wd = xm[:, o:o + LORA_W]
    ad = xm[:, o + LORA_W:o + LORA_W + LORA_A]
    gd = xm[:, o + LORA_W + LORA_A:o + LORA_W + LORA_A + LORA_G]

    z = w0_ref[...] + jnp.dot(jnp.tanh(wd), w2_ref[...], precision=HIGHEST, preferred_element_type=F32)
    w = jnp.minimum(z, 0.0) - jnp.log1p(jnp.exp(-jnp.abs(z))) - 0.5
    dec = jnp.exp(-jnp.exp(w))
    a = _sigmoid(a0_ref[...] + jnp.dot(ad, a2_ref[...], precision=HIGHEST, preferred_element_type=F32))
    g = jnp.dot(_sigmoid(gd).astype(BF16), g2_ref[...].astype(BF16), preferred_element_type=F32)

    ones_bd = _block_ones(MXU_TILE, RWKV_HEAD_DIM)
    kk = k * kk_ref[...]
    nrm = jnp.sqrt(_seg_sum(kk * kk, ones_bd))
    kkn = kk / jnp.maximum(nrm, 1e-12)
    kh = k * (1.0 + (a - 1.0) * ka_ref[...])
    bonus = _seg_sum(r * kh * rk_ref[...], ones_bd) * v

    r_o[...] = r
    dec_o[...] = dec
    k_o[...] = kh
    v_o[...] = v
    kkn_o[...] = kkn
    b_o[...] = kkn * a
    g_o[...] = g
    bonus_o[...] = bonus


def _rwkv_pre(proj, repl, period, row0, rows, tm, prm):
    d_a = prm["w0"].shape[1]
    d_ap = prm["mu"].shape[1]
    nt = rows // tm
    t0 = row0 // tm
    vec = lambda n: pl.BlockSpec((1, n), lambda i: (0, 0))
    full = lambda a: pl.BlockSpec(a.shape, lambda i: (0, 0))
    out_spec = pl.BlockSpec((tm, d_a), lambda i: (i, 0))
    outs = pl.pallas_call(
        functools.partial(_rwkv_pre_body, period=period, d_a=d_a),
        grid=(nt,),
        in_specs=[pl.BlockSpec((tm, d_ap), lambda i: (t0 + i, 0)),
                  pl.BlockSpec((repl.shape[0] // nt, d_ap), lambda i: (i, 0)),
                  vec(d_ap), vec(d_a), full(prm["w2"]), vec(d_a), full(prm["a2"]), full(prm["g2"]),
                  vec(d_a), vec(d_a), vec(d_a)],
        out_specs=[out_spec] * 8,
        out_shape=[jax.ShapeDtypeStruct((rows, d_a), F32)] * 8,
        compiler_params=_cparams(("arbitrary",)),
        name="rwkv_pre",
    )(proj, repl, prm["mu"], prm["w0"], prm["w2"], prm["a0"], prm["a2"], prm["g2"],
      prm["k_k"], prm["k_a"], prm["r_k"])
    return outs


def _rwkv_scan_body(*refs, tc, tp, nheads, nb, has_state):
    if has_state:
        s0_ref = refs[0]
        refs = refs[1:]
    (r_ref, w_ref, k_ref, v_ref, kk_ref, b_ref, g_ref, bonus_ref, lg_ref, lb_ref,
     o_ref, sout_ref, s_scr, vth_scr, ot_scr) = refs
    n = RWKV_HEAD_DIM
    c = pl.program_id(1)
    assert tp == 2 * n
    nh = nb * nheads
    heads = [(b, h) for b in range(nb) for h in range(nheads)]

    @pl.when(c == 0)
    def _():
        if has_state:
            zero = jnp.zeros((n, n), F32)
            for i, (b, h) in enumerate(heads):
                s0 = s0_ref[b, h]
                s_scr[i] = jnp.concatenate([s0, zero] if h % 2 == 0 else [zero, s0], axis=1)
        else:
            s_scr[...] = jnp.zeros_like(s_scr)

    for i, (b, h) in enumerate(heads):
        vh = v_ref[b, :, h * n:(h + 1) * n]
        if tp != tc:
            vh = jnp.concatenate([vh, jnp.zeros((tp - tc, n), F32)], axis=0)
        vth_scr[i * n:(i + 1) * n, :] = vh.T.astype(BF16)
    ot_scr[...] = jnp.zeros_like(ot_scr)

    trow = lax.broadcasted_iota(I32, (tp, 2 * n), 0)
    tcol = lax.broadcasted_iota(I32, (2 * n, tp), 1)
    head_par = lax.broadcasted_iota(I32, (nh, 1, 2 * n), 0) % 2
    lane_par = lax.broadcasted_iota(I32, (nh, 1, 2 * n), 2) // n
    own = head_par == lane_par

    sub = 32 if tc % 32 == 0 else 8

    def per_head(ref, base, masked):
        xs = [ref[b, pl.ds(base, sub), :] for b in range(nb)]
        x = jnp.stack([xs[b][:, (h // 2) * 2 * n:(h // 2 + 1) * 2 * n] for b, h in heads], axis=0)
        return jnp.where(own, x, 0.0) if masked else x

    def group(tg, carry):
        base = pl.multiple_of(tg * sub, sub)
        kk8 = per_head(kk_ref, base, True)
        b8 = per_head(b_ref, base, True)
        k8 = per_head(k_ref, base, True)
        w8 = per_head(w_ref, base, False)
        r8 = per_head(r_ref, base, False)
        st = s_scr[...]
        vth = vth_scr[...]
        o8 = None
        for s in range(sub):
            t = base + s
            pick = jnp.where(trow == t, 1.0, 0.0).astype(BF16)
            v_col = jnp.dot(vth, pick, preferred_element_type=F32).reshape(nh, n, 2 * n)
            sa = -jnp.sum(st * kk8[:, s:s + 1], axis=-1, keepdims=True)
            st = st * w8[:, s:s + 1] + sa * b8[:, s:s + 1] + v_col * k8[:, s:s + 1]
            q = (st * r8[:, s:s + 1]).reshape(nh * n, 2 * n).astype(BF16)
            put = jnp.where(tcol == t, 1.0, 0.0).astype(BF16)
            o_t = jnp.dot(q, put, preferred_element_type=F32)
            o8 = o_t if o8 is None else o8 + o_t
        s_scr[...] = st
        ot_scr[...] = ot_scr[...] + o8.reshape(nh, n, tp)
        return carry

    lax.fori_loop(0, tc // sub, group, 0)

    for b in range(nb):
        for hp in range(nheads // 2):
            parts = []
            for h in (2 * hp, 2 * hp + 1):
                oc = ot_scr[b * nheads + h]
                mean = jnp.mean(oc, axis=0, keepdims=True)
                d = oc - mean
                var = jnp.mean(d * d, axis=0, keepdims=True)
                on = d * lax.rsqrt(var + RWKV_GN_EPS)
                parts.append(on.T[:tc])
            sl = slice(hp * 2 * n, (hp + 1) * 2 * n)
            on2 = jnp.concatenate(parts, axis=1)
            o = (on2 * lg_ref[:, sl] + lb_ref[:, sl] + bonus_ref[b, :, sl]) * g_ref[b, :, sl]
            o_ref[b, :, sl] = o.astype(o_ref.dtype)

    @pl.when(c == pl.num_programs(1) - 1)
    def _():
        for i, (b, h) in enumerate(heads):
            s_h = s_scr[i]
            sout_ref[b, h] = s_h[:, 0:n] if h % 2 == 0 else s_h[:, n:2 * n]


def _rwkv_scan(pre, lnx_g, lnx_b, state, nseq, tseq, tc, nb):
    rows, d_a = pre[0].shape
    nheads = d_a // RWKV_HEAD_DIM
    nch = tseq // tc
    tp = max(tc, LANES)
    n = RWKV_HEAD_DIM
    nh = nb * nheads
    tok = pl.BlockSpec((nb, tc, d_a), lambda s, c: (s, c, 0))
    vec = pl.BlockSpec((1, d_a), lambda s, c: (0, 0))
    st = pl.BlockSpec((nb, nheads, n, n), lambda s, c: (s, 0, 0, 0))
    has_state = state is not None
    in_specs = ([st] if has_state else []) + [tok] * 8 + [vec, vec]
    args = ([state] if has_state else []) + [a.reshape(nseq, tseq, d_a) for a in pre] + [
        lnx_g.reshape(1, d_a), lnx_b.reshape(1, d_a)]
    oa, s_fin = pl.pallas_call(
        functools.partial(_rwkv_scan_body, tc=tc, tp=tp, nheads=nheads, nb=nb, has_state=has_state),
        grid=(nseq // nb, nch),
        in_specs=in_specs,
        out_specs=[tok, st],
        out_shape=[jax.ShapeDtypeStruct((nseq, tseq, d_a), BF16 if tc % 16 == 0 else F32),
                   jax.ShapeDtypeStruct((nseq, nheads, n, n), F32)],
        scratch_shapes=[pltpu.VMEM((nh, n, 2 * n), F32),
                        pltpu.VMEM((nh * n, tp), BF16),
                        pltpu.VMEM((nh, n, tp), F32)],
        compiler_params=_cparams(("arbitrary", "arbitrary")),
        name="rwkv_scan",
    )(*args)
    return oa.reshape(rows, d_a), s_fin


def _rope_body(q_ref, k_ref, v_ref, c_ref, s1_ref, s2_ref, qs_o, qf_o, kf_o, kb_o, vb_o, kt_o):
    c, s1, s2 = c_ref[...], s1_ref[...], s2_ref[...]
    half = ROT_DIM // 2
    w = q_ref.shape[1]

    def rot(x):
        return x * c + pltpu.roll(x, w - half, 1) * s1 + pltpu.roll(x, half, 1) * s2

    qr = rot(q_ref[...]) * ATTN_SCALE
    qf_o[...] = qr
    qs_o[...] = qr.astype(BF16)
    kr = rot(k_ref[...])
    kf_o[...] = kr
    kb_o[...] = kr.astype(BF16)
    vb_o[...] = v_ref[...].astype(BF16)
    for hh in range(w // DIFF_QK_DIM):
        kt_o[hh] = kr[:, hh * DIFF_QK_DIM:(hh + 1) * DIFF_QK_DIM].T


def _rope(proj, pos, d_ap, d_q, tm, nseq_p, tseq_p):
    n_tok = proj.shape[0]
    w = MXU_TILE
    half = ROT_DIM // 2
    inv_freq = ROPE_THETA ** (-jnp.arange(half, dtype=F32) / half)
    ang = pos.astype(F32)[:, None] * inv_freq[None, :]
    cos, sin = jnp.cos(ang), jnp.sin(ang)
    pad = DIFF_QK_DIM - ROT_DIM
    ones = jnp.ones((n_tok, pad), F32)
    zeros = jnp.zeros((n_tok, pad), F32)
    zh = jnp.zeros((n_tok, half), F32)
    reps = w // DIFF_QK_DIM
    c_t = jnp.tile(jnp.concatenate([cos, cos, ones], axis=1), (1, reps))
    s1_t = jnp.tile(jnp.concatenate([-sin, zh, zeros], axis=1), (1, reps))
    s2_t = jnp.tile(jnp.concatenate([zh, sin, zeros], axis=1), (1, reps))
    nq = d_q // w
    qb, kb, vb = d_ap // w, d_ap // w + nq, d_ap // w + 2 * nq
    tab = pl.BlockSpec((tm, w), lambda i, j: (i, 0))
    out = pl.BlockSpec((tm, w), lambda i, j: (i, j))
    hpb = w // DIFF_QK_DIM
    tps = tseq_p // tm
    kt_spec = pl.BlockSpec((None, hpb, DIFF_QK_DIM, tm),
                           lambda i, j: (jnp.minimum(i // tps, nseq_p), j, 0, i % tps))
    return pl.pallas_call(
        _rope_body,
        grid=(n_tok // tm, nq),
        in_specs=[pl.BlockSpec((tm, w), lambda i, j: (i, qb + j)),
                  pl.BlockSpec((tm, w), lambda i, j: (i, kb + j)),
                  pl.BlockSpec((tm, w), lambda i, j: (i, vb + j)),
                  tab, tab, tab],
        out_specs=[out, out, out, out, out, kt_spec],
        out_shape=[jax.ShapeDtypeStruct((n_tok, d_q), BF16),
                   jax.ShapeDtypeStruct((n_tok, d_q), F32),
                   jax.ShapeDtypeStruct((n_tok, d_q), F32),
                   jax.ShapeDtypeStruct((n_tok, d_q), BF16),
                   jax.ShapeDtypeStruct((n_tok, d_q), BF16),
                   jax.ShapeDtypeStruct((nseq_p + 1, d_q // DIFF_QK_DIM, DIFF_QK_DIM, tseq_p), F32)],
        compiler_params=_cparams(("arbitrary", "arbitrary")),
        name="rope",
    )(proj, proj, proj, c_t, s1_t, s2_t)


def _lambda(lam_ref, lam_init):
    lp = lam_ref[...]
    l1 = jnp.sum(lp[0:1] * lp[1:2], axis=-1, keepdims=True)
    l2 = jnp.sum(lp[2:3] * lp[3:4], axis=-1, keepdims=True)
    return jnp.exp(l1) - jnp.exp(l2) + lam_init


def _subln(o, g_ref, lam_init):
    inv = lax.rsqrt(jnp.mean(o * o, axis=-1, keepdims=True) + SUBLN_EPS)
    return (o * inv) * g_ref[...] * (1.0 - lam_init)


def _online_softmax_update(s, v, m_scr, l_scr, acc_scr):
    m_prev = m_scr[...]
    m_new = jnp.maximum(m_prev, jnp.max(s, axis=-1, keepdims=True))
    alpha = jnp.exp(m_prev - m_new)
    p = jnp.exp(s - m_new[:, 0:1])
    l_scr[...] = alpha * l_scr[...] + jnp.sum(p, axis=-1, keepdims=True)
    acc_scr[...] = alpha[:, 0:1] * acc_scr[...] + jnp.dot(p.astype(BF16), v, preferred_element_type=F32)
    m_scr[...] = m_new


def _attn_prompt_body(q_ref, k_ref, v_ref, lam_ref, g_ref, o_ref, qq_scr, m_scr, l_scr, acc_scr,
                      *, tq, lam_init):
    qi = pl.program_id(2)
    ki = pl.program_id(3)

    @pl.when(ki == 0)
    def _():
        q = q_ref[...]
        lane = lax.broadcasted_iota(I32, q.shape, 1)
        zero = jnp.zeros_like(q)
        qq_scr[0:tq] = jnp.where(lane < DIFF_QK_DIM, q, zero)
        qq_scr[tq:2 * tq] = jnp.where(lane >= DIFF_QK_DIM, q, zero)
        m_scr[...] = jnp.full_like(m_scr, NEG_INF)
        l_scr[...] = jnp.zeros_like(l_scr)
        acc_scr[...] = jnp.zeros_like(acc_scr)

    rb = min(tq, 256)

    def tile(diagonal):
        for r0 in range(0, 2 * tq, rb):
            rows = slice(r0, r0 + rb)
            nk = min(tq, (r0 % tq) + rb) if diagonal else tq
            s = lax.dot_general(qq_scr[rows], k_ref[0:nk], (((1,), (1,)), ((), ())),
                                preferred_element_type=F32)
            if diagonal:
                row = (r0 % tq) + lax.broadcasted_iota(I32, s.shape, 0)
                col = lax.broadcasted_iota(I32, s.shape, 1)
                s = jnp.where(row >= col, s, NEG_INF)
            m_prev = m_scr[rows]
            m_new = jnp.maximum(m_prev, jnp.max(s, axis=-1, keepdims=True))
            alpha = jnp.exp(m_prev - m_new)
            p = jnp.exp(s - jnp.concatenate([m_new] * (nk // LANES), axis=1))
            l_scr[rows] = alpha * l_scr[rows] + jnp.sum(p, axis=-1, keepdims=True)
            acc_scr[rows] = alpha * acc_scr[rows] + jnp.dot(p.astype(BF16), v_ref[0:nk],
                                                            preferred_element_type=F32)
            m_scr[rows] = m_new

    @pl.when(ki < qi)
    def _():
        tile(False)

    @pl.when(ki == qi)
    def _():
        tile(True)
        lam = _lambda(lam_ref, lam_init)
        o1 = acc_scr[0:tq] / l_scr[0:tq, 0:1]
        o2 = acc_scr[tq:2 * tq] / l_scr[tq:2 * tq, 0:1]
        o_ref[...] = _subln(o1 - lam * o2, g_ref, lam_init).astype(o_ref.dtype)


def _attn_prompt(qs, kb, vb, lam_p, subln_g, nseq, tseq, tq, lam_init):
    rows, d_q = qs.shape
    nh = d_q // DIFF_V_DIM
    nq = tseq // tq
    w = DIFF_V_DIM
    return pl.pallas_call(
        functools.partial(_attn_prompt_body, tq=tq, lam_init=lam_init),
        grid=(nseq, nh, nq, nq),
        in_specs=[pl.BlockSpec((tq, w), lambda b, h, i, j: (b * nq + i, h)),
                  pl.BlockSpec((tq, w), lambda b, h, i, j: (b * nq + jnp.minimum(i, j), h)),
                  pl.BlockSpec((tq, w), lambda b, h, i, j: (b * nq + jnp.minimum(i, j), h)),
                  pl.BlockSpec(lam_p.shape, lambda b, h, i, j: (0, 0)),
                  pl.BlockSpec((1, w), lambda b, h, i, j: (0, 0))],
        out_specs=pl.BlockSpec((tq, w), lambda b, h, i, j: (b * nq + i, h)),
        out_shape=jax.ShapeDtypeStruct((nseq * tseq, d_q), BF16),
        scratch_shapes=[pltpu.VMEM((2 * tq, w), BF16),
                        pltpu.VMEM((2 * tq, LANES), F32),
                        pltpu.VMEM((2 * tq, LANES), F32),
                        pltpu.VMEM((2 * tq, w), F32)],
        compiler_params=_cparams(("arbitrary", "arbitrary", "arbitrary", "arbitrary")),
        name="attn_prompt",
    )(qs, kb, vb, lam_p, subln_g.reshape(1, w))


def _attn_sample_body(pt_ref, q_ref, *refs, ts, lam_init, pages_per_step):
    kc_refs = refs[:pages_per_step]
    vc_refs = refs[pages_per_step:2 * pages_per_step]
    kn_ref, vn_ref, lam_ref, g_ref, o_ref, qbd_scr, m_scr, l_scr, acc_scr = refs[2 * pages_per_step:]
    p = pl.program_id(1)
    nheads_qk = q_ref.shape[1] // DIFF_QK_DIM

    @pl.when(p == 0)
    def _():
        q = q_ref[...]
        head = lax.broadcasted_iota(I32, q.shape, 1) // DIFF_QK_DIM
        zero = jnp.zeros_like(q)
        rows = [jnp.where(head == m, q, zero) for m in range(nheads_qk)]
        qbd_scr[...] = jnp.concatenate(rows, axis=0).astype(BF16)
        m_scr[...] = jnp.full_like(m_scr, NEG_INF)
        l_scr[...] = jnp.zeros_like(l_scr)
        acc_scr[...] = jnp.zeros_like(acc_scr)

    nh_v = nheads_qk // 2
    w = DIFF_V_DIM

    def update(s, v_heads):
        m_prev = m_scr[...]
        m_new = jnp.maximum(m_prev, jnp.max(s, axis=-1, keepdims=True))
        alpha = jnp.exp(m_prev - m_new)
        pr = jnp.exp(s - jnp.concatenate([m_new] * (s.shape[1] // LANES), axis=1))
        l_scr[...] = alpha * l_scr[...] + jnp.sum(pr, axis=-1, keepdims=True)
        m_scr[...] = m_new
        pb = pr.astype(BF16)
        for h in range(nh_v):
            rows = slice(2 * h * ts, (2 * h + 2) * ts)
            acc_scr[rows] = alpha[rows] * acc_scr[rows] + jnp.dot(pb[rows], v_heads[h],
                                                                  preferred_element_type=F32)

    kt = jnp.concatenate([r[...].reshape(nheads_qk * DIFF_QK_DIM, r.shape[2]) for r in kc_refs],
                         axis=1).astype(BF16)
    s_past = jnp.dot(qbd_scr[...], kt, preferred_element_type=F32)
    npos = vc_refs[0].shape[0] // nh_v
    v_past = [jnp.concatenate([r[pl.ds(h, npos, stride=nh_v), :] for r in vc_refs], axis=0).astype(BF16)
              for h in range(nh_v)]
    update(s_past, v_past)

    @pl.when(p == pl.num_programs(1) - 1)
    def _():
        padk = jnp.zeros((LANES - ts, kn_ref.shape[1]), F32)
        kn = jnp.concatenate([kn_ref[...], padk], axis=0).astype(BF16)
        vn = jnp.concatenate([vn_ref[...], padk], axis=0).astype(BF16)
        s = lax.dot_general(qbd_scr[...], kn, (((1,), (1,)), ((), ())), preferred_element_type=F32)
        row = lax.broadcasted_iota(I32, s.shape, 0) % ts
        col = lax.broadcasted_iota(I32, s.shape, 1)
        s = jnp.where(row >= col, s, NEG_INF)
        update(s, [vn[:, h * w:(h + 1) * w] for h in range(nh_v)])
        lam = _lambda(lam_ref, lam_init)
        for h in range(nh_v):
            r1 = slice(2 * h * ts, (2 * h + 1) * ts)
            r2 = slice((2 * h + 1) * ts, (2 * h + 2) * ts)
            o1 = acc_scr[r1] / l_scr[r1, 0:1]
            o2 = acc_scr[r2] / l_scr[r2, 0:1]
            o_ref[:, h * w:(h + 1) * w] = _subln(o1 - lam * o2, g_ref, lam_init).astype(o_ref.dtype)


def _attn_sample(qs, kb, vb, cache_k, cache_v, layer_j, page_flat, lam_p, subln_g, nb, ts, npages, lam_init):
    d_q = qs.shape[1]
    nrow = (d_q // DIFF_QK_DIM) * ts
    gpp = next(g for g in (8, 4, 2, 1) if npages % g == 0)
    tok = pl.BlockSpec((ts, d_q), lambda b, p, pt: (b, 0))
    cache_k = jnp.transpose(cache_k, (0, 1, 3, 4, 2))
    cache_v = cache_v.reshape(cache_v.shape[:2] + (-1, cache_v.shape[-1]))

    def page(cache, g):
        return pl.BlockSpec((None, None) + cache.shape[2:],
                            lambda b, p, pt: (layer_j, pt[b * npages + p * gpp + g]) + (0,) * (cache.ndim - 2))

    return pl.pallas_call(
        functools.partial(_attn_sample_body, ts=ts, lam_init=lam_init, pages_per_step=gpp),
        grid_spec=pltpu.PrefetchScalarGridSpec(
            num_scalar_prefetch=1,
            grid=(nb, npages // gpp),
            in_specs=[tok] + [page(cache_k, g) for g in range(gpp)] + [page(cache_v, g) for g in range(gpp)]
                     + [tok, tok,
                        pl.BlockSpec(lam_p.shape, lambda b, p, pt: (0, 0)),
                        pl.BlockSpec((1, DIFF_V_DIM), lambda b, p, pt: (0, 0))],
            out_specs=tok,
            scratch_shapes=[pltpu.VMEM((nrow, d_q), BF16),
                            pltpu.VMEM((nrow, LANES), F32),
                            pltpu.VMEM((nrow, LANES), F32),
                            pltpu.VMEM((nrow, DIFF_V_DIM), F32)]),
        out_shape=jax.ShapeDtypeStruct((nb * ts, d_q), F32),
        compiler_params=_cparams(("arbitrary", "arbitrary")),
        name="attn_sample",
    )(page_flat, qs, *([cache_k] * gpp), *([cache_v] * gpp), kb, vb, lam_p, subln_g.reshape(1, DIFF_V_DIM))


def _gmlp_body(z_ref, lg_ref, lb_ref, wm_ref, bm_ref, y_ref, *maybe_v, d_c):
    u = z_ref[:, 0:d_c]
    v = z_ref[:, d_c:2 * d_c]
    mean = jnp.mean(v, axis=-1, keepdims=True)
    d = v - mean
    var = jnp.mean(d * d, axis=-1, keepdims=True)
    vn = (d * lax.rsqrt(var + GMLP_LN_EPS)) * lg_ref[...] + lb_ref[...]
    if maybe_v:
        maybe_v[0][...] = vn
    gd = d_c // GMLP_GROUPS
    vb = vn.astype(BF16)
    for g in range(GMLP_GROUPS):
        sl = slice(g * gd, (g + 1) * gd)
        mixed = jnp.dot(wm_ref[g].astype(BF16), vb[:, sl], preferred_element_type=F32) + bm_ref[g]
        y_ref[:, sl] = (u[:, sl] * mixed).astype(y_ref.dtype)


def _gmlp(z, row0, rows, ln_g, ln_b, wmix, bmix, emit_v):
    d_c = z.shape[1] // 2
    tm = CHUNK
    t0 = row0 // tm
    gd = d_c // GMLP_GROUPS
    out_specs = [pl.BlockSpec((tm, d_c), lambda i: (i, 0))]
    out_shape = [jax.ShapeDtypeStruct((rows, d_c), BF16)]
    if emit_v:
        out_specs.append(pl.BlockSpec((tm, d_c), lambda i: (i, 0)))
        out_shape.append(jax.ShapeDtypeStruct((rows, d_c), F32))
    return pl.pallas_call(
        functools.partial(_gmlp_body, d_c=d_c),
        grid=(rows // tm,),
        in_specs=[pl.BlockSpec((tm, 2 * d_c), lambda i: (t0 + i, 0)),
                  pl.BlockSpec((1, d_c), lambda i: (0, 0)),
                  pl.BlockSpec((1, d_c), lambda i: (0, 0)),
                  pl.BlockSpec((GMLP_GROUPS, tm, tm), lambda i: (0, 0, 0)),
                  pl.BlockSpec((GMLP_GROUPS, tm, gd), lambda i: (0, 0, 0))],
        out_specs=out_specs,
        out_shape=out_shape,
        compiler_params=_cparams(("arbitrary",)),
        name="gmlp_gate",
    )(z, ln_g.reshape(1, d_c), ln_b.reshape(1, d_c), wmix, bmix)


def _store_token_rows(ref2, x2d):
    dc = x2d.shape[1] // LANES
    for c in range(dc):
        ref2[pl.ds(c, x2d.shape[0], stride=dc), :] = x2d[:, c * LANES:(c + 1) * LANES]


def _load_token_rows(ref2, dc):
    n = ref2.shape[0] // dc
    return jnp.concatenate([ref2[pl.ds(c, n, stride=dc), :] for c in range(dc)], axis=1)


def _router_body(h_ref, g_ref, wr_ref, br_ref, xn_o, idx_o, cw_o):
    x = h_ref[...]
    inv = lax.rsqrt(jnp.mean(x * x, axis=-1, keepdims=True) + NORM_EPS)
    xn = (x * inv) * g_ref[...]
    _store_token_rows(xn_o, xn)
    logits = jnp.dot(xn, wr_ref[...], precision=HIGHEST, preferred_element_type=F32) + br_ref[...]
    lane = lax.broadcasted_iota(I32, logits.shape, 1)
    big = jnp.int32(1 << 20)
    ninf = jnp.float32(-jnp.inf)

    def first_max(vals):
        mx = jnp.max(vals, axis=-1, keepdims=True)
        ix = jnp.min(jnp.where(vals == mx, lane, big), axis=-1, keepdims=True)
        return mx, ix

    gl = jnp.where(lane < N_EXP_GROUPS, logits, ninf)
    gmax, gtop = first_max(gl)
    gate = 1.0 / jnp.sum(jnp.exp(gl - gmax), axis=-1, keepdims=True)
    lo = N_EXP_GROUPS + EXP_PER_GROUP * gtop
    el = jnp.where((lane >= lo) & (lane < lo + EXP_PER_GROUP), logits, ninf)
    m1, i1 = first_max(el)
    m2, i2 = first_max(jnp.where(lane == i1, ninf, el))
    p2 = jnp.exp(m2 - m1)
    c1 = gate / (1.0 + p2)
    c2 = gate * p2 / (1.0 + p2)
    idx_o[...] = jnp.where(lane == 0, i1 - N_EXP_GROUPS, jnp.where(lane == 1, i2 - N_EXP_GROUPS, 0))
    cw_o[...] = jnp.where(lane == 0, c1, jnp.where(lane == 1, c2, 0.0))


def _router(h, g, w_grp, b_grp, w_exp, b_exp, tm):
    n_tok, d = h.shape
    ncol = w_grp.shape[1] + w_exp.shape[1]
    wr = jnp.concatenate([w_grp, w_exp, jnp.zeros((d, LANES - ncol), F32)], axis=1)
    br = jnp.concatenate([b_grp, b_exp, jnp.zeros((LANES - ncol,), F32)]).reshape(1, LANES)
    tok = lambda w: pl.BlockSpec((tm, w), lambda i: (i, 0))
    return pl.pallas_call(
        _router_body,
        grid=(n_tok // tm,),
        in_specs=[tok(d), pl.BlockSpec((1, d), lambda i: (0, 0)),
                  pl.BlockSpec((d, LANES), lambda i: (0, 0)),
                  pl.BlockSpec((1, LANES), lambda i: (0, 0))],
        out_specs=[pl.BlockSpec((tm * (d // LANES), LANES), lambda i: (i, 0)), tok(LANES), tok(LANES)],
        out_shape=[jax.ShapeDtypeStruct((n_tok * (d // LANES), LANES), F32),
                   jax.ShapeDtypeStruct((n_tok, LANES), I32),
                   jax.ShapeDtypeStruct((n_tok, LANES), F32)],
        compiler_params=_cparams(("arbitrary",)),
        name="moe_router",
    )(h, g.reshape(1, d), wr, br)


def _gather_copy(x_hbm, xbuf, sem, slot, i, tok, dc):
    src = x_hbm.at[pl.ds(pl.multiple_of(tok * dc, dc), dc)]
    dst = xbuf.at[slot, pl.ds(pl.multiple_of(i * dc, dc), dc)]
    return pltpu.make_async_copy(src, dst, sem.at[slot])


def _moe_ffn_body(te_ref, nvalid_ref, tok_ref, x_hbm, rw_ref, w1_ref, w3_ref, w2_ref, y_ref,
                  xbuf, sem, w1b, w3b, w2b, *, tm):
    t = pl.program_id(0)
    nv = nvalid_ref[0]
    valid = t < nv
    dc = w1_ref.shape[0] // LANES

    ahead = xbuf.shape[0]

    def issue(tile, slot):
        def body(i, c):
            _gather_copy(x_hbm, xbuf, sem, slot, i, tok_ref[tile * tm + i], dc).start()
            return c
        lax.fori_loop(0, tm, body, 0, unroll=8)

    for first in range(ahead - 1):
        @pl.when((t == 0) & (first < nv))
        def _():
            issue(first, first)

    @pl.when(t + ahead - 1 < nv)
    def _():
        issue(t + ahead - 1, (t + ahead - 1) % ahead)

    @pl.when(valid)
    def _():
        prev = te_ref[jnp.maximum(t - 1, 0)]

        @pl.when((t == 0) | (te_ref[t] != prev))
        def _():
            w1b[...] = w1_ref[...].astype(BF16)
            w3b[...] = w3_ref[...].astype(BF16)
            w2b[...] = w2_ref[...].astype(BF16)

        slot = t % ahead

        pltpu.make_async_copy(x_hbm.at[pl.ds(0, tm * dc)], xbuf.at[slot], sem.at[slot]).wait()

        x = _load_token_rows(xbuf.at[slot], dc).astype(BF16)
        h1 = jnp.dot(x, w1b[...], preferred_element_type=F32)
        h3 = jnp.dot(x, w3b[...], preferred_element_type=F32)
        hh = (h1 * _sigmoid(h1)) * h3 * rw_ref[...]
        _store_token_rows(y_ref, jnp.dot(hh.astype(BF16), w2b[...], preferred_element_type=F32))

    @pl.when(jnp.logical_not(valid))
    def _():
        y_ref[...] = jnp.zeros_like(y_ref)


def _moe_ffn(xn, row_tok, row_w, tile_exp, nvalid_tiles, w1, w3, w2, layer, n_tiles, tm):
    d, f = w1.shape[2], w1.shape[3]
    dc = d // LANES
    rows = n_tiles * tm
    def last_valid(t, nv):
        return jnp.minimum(t, jnp.maximum(nv[0] - 1, 0))
    return pl.pallas_call(
        functools.partial(_moe_ffn_body, tm=tm),
        grid_spec=pltpu.PrefetchScalarGridSpec(
            num_scalar_prefetch=3,
            grid=(n_tiles,),
            in_specs=[pl.BlockSpec(memory_space=pl.ANY),
                      pl.BlockSpec((tm, 1), lambda t, te, nv, tk: (last_valid(t, nv), 0)),
                      pl.BlockSpec((None, None, d, f), lambda t, te, nv, tk: (layer, te[t], 0, 0)),
                      pl.BlockSpec((None, None, d, f), lambda t, te, nv, tk: (layer, te[t], 0, 0)),
                      pl.BlockSpec((None, None, f, d), lambda t, te, nv, tk: (layer, te[t], 0, 0))],
            out_specs=pl.BlockSpec((tm * dc, LANES), lambda t, te, nv, tk: (t, 0)),
            scratch_shapes=[pltpu.VMEM((3, tm * dc, LANES), F32), pltpu.SemaphoreType.DMA((3,)),
                            pltpu.VMEM((d, f), BF16), pltpu.VMEM((d, f), BF16), pltpu.VMEM((f, d), BF16)]),
        out_shape=jax.ShapeDtypeStruct((rows * dc, LANES), F32),
        compiler_params=_cparams(("arbitrary",)),
        name="moe_ffn",
    )(tile_exp, nvalid_tiles, row_tok, xn, row_w, w1, w3, w2)


def _combine_copy(y_hbm, buf, sem, slot, k, i, row, dc):
    src = y_hbm.at[pl.ds(pl.multiple_of(row * dc, dc), dc)]
    dst = buf.at[slot, k, pl.ds(pl.multiple_of(i * dc, dc), dc)]
    return pltpu.make_async_copy(src, dst, sem.at[slot])


def _moe_combine_body(pos_ref, h_ref, y_hbm, *rest, tm, final_eps):
    if final_eps is None:
        o_ref, buf, sem = rest
    else:
        gf_ref, o_ref, on_ref, buf, sem = rest
    t = pl.program_id(0)
    nt = pl.num_programs(0)
    dc = h_ref.shape[1] // LANES

    def issue(tile, slot):
        def body(i, c):
            a = 2 * (tile * tm + i)
            _combine_copy(y_hbm, buf, sem, slot, 0, i, pos_ref[a], dc).start()
            _combine_copy(y_hbm, buf, sem, slot, 1, i, pos_ref[a + 1], dc).start()
            return c
        lax.fori_loop(0, tm, body, 0, unroll=8)

    @pl.when(t == 0)
    def _():
        issue(0, 0)

    @pl.when(t + 1 < nt)
    def _():
        issue(t + 1, (t + 1) % 2)

    slot = t % 2

    for k in range(2):
        pltpu.make_async_copy(y_hbm.at[pl.ds(0, tm * dc)], buf.at[slot, k], sem.at[slot]).wait()

    out = h_ref[...] + _load_token_rows(buf.at[slot, 0], dc) + _load_token_rows(buf.at[slot, 1], dc)
    o_ref[...] = out
    if final_eps is not None:
        inv = lax.rsqrt(jnp.mean(out * out, axis=-1, keepdims=True) + final_eps)
        on_ref[...] = (out * inv) * gf_ref[...]


def _moe_combine(h, ys, pos, tm, final_g=None):
    n_tok, d = h.shape
    tok = pl.BlockSpec((tm, d), lambda i, p: (i, 0))
    in_specs = [tok, pl.BlockSpec(memory_space=pl.ANY)]
    args = [pos, h, ys]
    out_specs = [tok]
    out_shape = [jax.ShapeDtypeStruct((n_tok, d), F32)]
    if final_g is not None:
        in_specs.append(pl.BlockSpec((1, d), lambda i, p: (0, 0)))
        args.append(final_g.reshape(1, d))
        out_specs.append(tok)
        out_shape.append(jax.ShapeDtypeStruct((n_tok, d), F32))
    return pl.pallas_call(
        functools.partial(_moe_combine_body, tm=tm, final_eps=None if final_g is None else NORM_EPS),
        grid_spec=pltpu.PrefetchScalarGridSpec(
            num_scalar_prefetch=1,
            grid=(n_tok // tm,),
            in_specs=in_specs,
            out_specs=out_specs,
            scratch_shapes=[pltpu.VMEM((2, 2, tm * (d // LANES), LANES), F32), pltpu.SemaphoreType.DMA((2,))]),
        out_shape=out_shape,
        compiler_params=_cparams(("arbitrary",)),
        name="moe_combine",
    )(*args)


def _hier_moe(h, g, w_grp, b_grp, w_exp, b_exp, w1, w3, w2, layer, final_g=None):
    n_tok, d = h.shape
    n_exp = w1.shape[1]
    tm = MOE_TILE
    xn, idx, cw = _router(h, g, w_grp, b_grp, w_exp, b_exp, tm=256)

    n_asg = 2 * n_tok
    flat_e = idx[:, :2].reshape(n_asg)
    flat_c = cw[:, :2].reshape(n_asg)
    onehot = (flat_e[:, None] == jnp.arange(n_exp, dtype=I32)[None, :]).astype(I32)
    csum = jnp.cumsum(onehot, axis=0)
    counts = csum[-1]
    rank = jnp.take_along_axis(csum - onehot, flat_e[:, None], axis=1)[:, 0]
    padded = ((counts + tm - 1) // tm) * tm
    ends = jnp.cumsum(padded)
    pos = (ends - padded)[flat_e] + rank
    n_tiles = n_asg // tm + n_exp
    rows = n_tiles * tm
    upd = jnp.stack([jnp.arange(n_asg, dtype=I32) // 2, lax.bitcast_convert_type(flat_c, I32)], axis=1)
    row_meta = jnp.zeros((rows, 2), I32).at[pos].set(upd)
    row_tok = row_meta[:, 0]
    row_w = lax.bitcast_convert_type(row_meta[:, 1], F32).reshape(rows, 1)
    nvalid = (ends[-1] // tm).astype(I32).reshape(1)
    tile_ids = jnp.minimum(jnp.arange(n_tiles, dtype=I32), nvalid - 1)
    tile_exp = jnp.searchsorted(ends, tile_ids * tm, side="right").astype(I32)

    ys = _moe_ffn(xn, row_tok, row_w, tile_exp, nvalid, w1, w3, w2, layer, n_tiles, tm)
    return _moe_combine(h, ys, pos.astype(I32), 128, final_g)


def kernel(x_prompt, x_sample, cache_k, cache_v, page_table, state_wkv, state_shift, norm_mix, norm_ffn, norm_final, even_w_in, even_w_out, rwkv_mu, rwkv_w0, rwkv_w2, rwkv_a0, rwkv_a2, rwkv_g2, rwkv_k_k, rwkv_k_a, rwkv_r_k, rwkv_lnx_g, rwkv_lnx_b, diff_lam_q1, diff_lam_k1, diff_lam_q2, diff_lam_k2, diff_subln_g, gmlp_w_in, gmlp_ln_g, gmlp_ln_b, gmlp_w_s, gmlp_b_s, gmlp_w_out, moe_w_grp, moe_b_grp, moe_w_exp, moe_b_exp, moe_w1, moe_w3, moe_w2):
    bp, tp, d = x_prompt.shape
    bs, ts, _ = x_sample.shape
    n_p, n_s = bp * tp, bs * ts
    n_tok = n_p + n_s
    depth = norm_mix.shape[0]
    npages, psz = page_table.shape[1], cache_k.shape[2]
    past_len = npages * psz
    d_a = rwkv_w0.shape[1]
    d_ap = rwkv_mu.shape[1]
    d_q = cache_k.shape[3] * cache_k.shape[4]
    n_pool = cache_k.shape[1]
    nheads = d_a // RWKV_HEAD_DIM
    tm = 512 if n_p % 512 == 0 and n_s % 512 == 0 else 128
    tq = next((t for t in (1024, 512) if tp % t == 0), 128)
    tc = min(tp, CHUNK)

    h = jnp.concatenate([x_prompt.reshape(n_p, d), x_sample.reshape(n_s, d)], axis=0)
    pos = jnp.concatenate([jnp.tile(jnp.arange(tp), bp), jnp.tile(past_len + jnp.arange(ts), bs)])
    moe = lambda l, hh, fg=None: _hier_moe(hh, norm_ffn[l], moe_w_grp[l], moe_b_grp[l], moe_w_exp[l],
                                          moe_b_exp[l], moe_w1, moe_w3, moe_w2, l, fg)

    outs = {k: [] for k in ("k_p", "v_p", "wkv_p", "sh_p", "k_s", "v_s", "wkv_s", "sh_s", "gv_s")}
    y_norm = None
    for layer in range(depth):
        j = layer // 2
        xn = _rmsnorm(h, norm_mix[layer], NORM_EPS, BF16, tm)
        last = layer == depth - 1
        if layer % 2 == 0:
            proj = _matmul([xn], even_w_in[j], bm=tm, bn=even_w_in.shape[2] // 5, name="even_in_proj")
            prm = dict(mu=rwkv_mu[j].reshape(1, d_ap), w0=rwkv_w0[j].reshape(1, d_a), w2=rwkv_w2[j],
                       a0=rwkv_a0[j].reshape(1, d_a), a2=rwkv_a2[j], g2=rwkv_g2[j],
                       k_k=rwkv_k_k[j].reshape(1, d_a), k_a=rwkv_k_a[j].reshape(1, d_a),
                       r_k=rwkv_r_k[j].reshape(1, d_a))
            tmr = 256 if tp % 256 == 0 else 128
            tile_last = proj[tmr - 1:n_p:tmr, :d_ap]
            seq_last_s = proj[n_p + ts - 1::ts, :d_ap]
            starts_seq = (jnp.arange(n_p // tmr) % (tp // tmr) == 0)[:, None]
            bnd = jnp.concatenate([jnp.zeros((1, d_ap), F32), tile_last[:-1]], axis=0)
            bnd = jnp.where(starts_seq, 0.0, bnd)
            repl_p = jnp.zeros((n_p // tmr, 8, d_ap), F32).at[:, 0].set(bnd)
            repl_p = repl_p.reshape(n_p // tmr * 8, d_ap)
            repl_s = jnp.repeat(state_shift[j], ts, axis=0)
            pre_p = _rwkv_pre(proj, repl_p, tmr, 0, n_p, tmr, prm)
            pre_s = _rwkv_pre(proj, repl_s, ts, n_p, n_s, tmr, prm)
            oa_p, wkv_p = _rwkv_scan(pre_p, rwkv_lnx_g[j], rwkv_lnx_b[j], None, bp, tp, tc,
                                     2 if bp % 2 == 0 else 1)
            oa_s, wkv_s = _rwkv_scan(pre_s, rwkv_lnx_g[j], rwkv_lnx_b[j], state_wkv[j], bs, ts, ts,
                                     2 if bs % 2 == 0 else 1)

            qs, qf, kf, kb, vb, kt = _rope(proj, pos, d_ap, d_q, tm, bp, tp)
            vf = proj[:, d_ap + 2 * d_q:]
            lam_init = 0.8 - 0.6 * math.exp(-0.3 * layer)
            lam_p = jnp.stack([diff_lam_q1[j], diff_lam_k1[j], diff_lam_q2[j], diff_lam_k2[j]])
            ob_p = _attn_prompt(qs[:n_p], kb[:n_p], vb[:n_p], lam_p, diff_subln_g[j], bp, tp, tq, lam_init)
            ob_s = _attn_sample(qf[n_p:], kf[n_p:], vf[n_p:], cache_k, cache_v, j,
                                page_table.reshape(-1).astype(I32),
                                lam_p, diff_subln_g[j], bs, ts, npages, lam_init)
            oa = jnp.concatenate([oa_p, oa_s.astype(BF16)], axis=0)
            ob = jnp.concatenate([ob_p, ob_s.astype(BF16)], axis=0)
            h = _matmul([oa, ob], even_w_out[j], bm=tm, bn=1024, res=h, name="even_out_proj")

            outs["k_p"].append(jnp.transpose(kt[:bp], (0, 3, 1, 2)))
            outs["v_p"].append(vf[:n_p].reshape(bp, tp, -1, DIFF_V_DIM))
            outs["wkv_p"].append(wkv_p)
            outs["sh_p"].append(tile_last[tp // tmr - 1::tp // tmr])
            outs["k_s"].append(kf[n_p:].reshape(bs, ts, -1, DIFF_QK_DIM))
            outs["v_s"].append(vf[n_p:].reshape(bs, ts, -1, DIFF_V_DIM))
            outs["wkv_s"].append(wkv_s)
            outs["sh_s"].append(seq_last_s)
        else:
            d_c = gmlp_ln_g.shape[1]
            z = _matmul([xn], gmlp_w_in[j], bm=tm, bn=1024, act="gelu", name="gmlp_in_proj")
            ws = gmlp_w_s[j]
            bsb = gmlp_b_s[j]
            gd = d_c // GMLP_GROUPS
            lp = min(tp, CHUNK)
            wm_p = jnp.tril(ws[:, :lp, :lp])
            bm_p = jnp.broadcast_to(bsb[:, :lp, None], (GMLP_GROUPS, lp, gd))
            rep = CHUNK // ts
            eye = jnp.eye(rep, dtype=F32)
            wm_s = jnp.einsum("ab,gts->gatbs", eye, jnp.tril(ws[:, :ts, :ts])).reshape(GMLP_GROUPS, CHUNK, CHUNK)
            bm_s = jnp.broadcast_to(jnp.tile(bsb[:, :ts], (1, rep))[:, :, None], (GMLP_GROUPS, CHUNK, gd))
            (y_p,) = _gmlp(z, 0, n_p, gmlp_ln_g[j], gmlp_ln_b[j], wm_p, bm_p, False)
            y_s, v_rows = _gmlp(z, n_p, n_s, gmlp_ln_g[j], gmlp_ln_b[j], wm_s, bm_s, True)
            yin = jnp.concatenate([y_p, y_s], axis=0)
            h = _matmul([yin], gmlp_w_out[j], bm=tm, bn=1024, res=h, name="gmlp_out_proj")
            outs["gv_s"].append(v_rows.reshape(bs, ts, d_c))
        res = moe(layer, h, norm_final if last else None)
        h = res[0]
        if last:
            y_norm = res[1]

    y_prompt = y_norm[:n_p].reshape(bp, tp, d)
    y_sample = y_norm[n_p:].reshape(bs, ts, d)
    st = lambda k: jnp.stack(outs[k])
    return (y_prompt, y_sample, st("k_p"), st("v_p"), st("wkv_p"), st("sh_p"),
            st("k_s"), st("v_s"), st("wkv_s"), st("sh_s"), st("gv_s"))
```

```python
import functools
import math

import jax
import jax.numpy as jnp
from jax import lax
from jax.experimental import pallas as pl
from jax.experimental.pallas import tpu as pltpu

F32 = jnp.float32
BF16 = jnp.bfloat16
I32 = jnp.int32

RWKV_HEAD_DIM = 64
RWKV_GN_EPS = 64e-5
DIFF_QK_DIM = 64
DIFF_V_DIM = 128
ROT_DIM = 16
ROPE_THETA = 500000.0
ATTN_SCALE = DIFF_QK_DIM ** -0.5
NEG_INF = -1e30
NORM_EPS = 1e-6
SUBLN_EPS = 1e-5
GMLP_LN_EPS = 1e-5
GMLP_GROUPS = 8
CHUNK = 128
N_EXP_GROUPS = 4
EXP_PER_GROUP = 8
LORA_W, LORA_A, LORA_G = 64, 64, 128

LANES = 128
MXU_TILE = 256
VMEM_LIMIT = 56 * 1024 * 1024
MOE_TILE = 256
HIGHEST = lax.Precision.HIGHEST


def _cparams(sem):
    return pltpu.CompilerParams(dimension_semantics=sem, vmem_limit_bytes=VMEM_LIMIT)


def _rmsnorm_body(x_ref, g_ref, o_ref, *, eps):
    x = x_ref[...]
    inv = lax.rsqrt(jnp.mean(x * x, axis=-1, keepdims=True) + eps)
    o_ref[...] = ((x * inv) * g_ref[...]).astype(o_ref.dtype)


def _rmsnorm(x, g, eps, out_dtype, tm):
    m, d = x.shape
    return pl.pallas_call(
        functools.partial(_rmsnorm_body, eps=eps),
        grid=(m // tm,),
        in_specs=[pl.BlockSpec((tm, d), lambda i: (i, 0)),
                  pl.BlockSpec((1, d), lambda i: (0, 0))],
        out_specs=pl.BlockSpec((tm, d), lambda i: (i, 0)),
        out_shape=jax.ShapeDtypeStruct((m, d), out_dtype),
        compiler_params=_cparams(("arbitrary",)),
        name="rmsnorm",
    )(x, g.reshape(1, d))


def _gelu_exact(x):
    return 0.5 * x * (1.0 + lax.erf(x * (1.0 / math.sqrt(2.0))))


def _mm_body(*refs, n_x, act, has_res):
    xs = refs[:n_x]
    w_ref = refs[n_x]
    res_ref = refs[n_x + 1] if has_res else None
    o_ref = refs[n_x + 1 + has_res]
    wb_ref = refs[n_x + 2 + has_res]

    @pl.when(pl.program_id(1) == 0)
    def _():
        wb_ref[...] = w_ref[...].astype(BF16)

    kx = w_ref.shape[0] // n_x
    acc = None
    for i, x_ref in enumerate(xs):
        part = jnp.dot(x_ref[...], wb_ref[i * kx:(i + 1) * kx, :], preferred_element_type=F32)
        acc = part if acc is None else acc + part
    if act == "gelu":
        acc = _gelu_exact(acc)
    if has_res:
        acc = acc + res_ref[...]
    o_ref[...] = acc.astype(o_ref.dtype)


def _matmul(xs, w, *, bm, bn, act=None, res=None, out_dtype=F32, name="matmul"):
    m = xs[0].shape[0]
    k, n = w.shape
    kx = k // len(xs)
    in_specs = [pl.BlockSpec((bm, kx), lambda j, i: (i, 0)) for _ in xs]
    in_specs.append(pl.BlockSpec((k, bn), lambda j, i: (0, j)))
    args = list(xs) + [w]
    if res is not None:
        in_specs.append(pl.BlockSpec((bm, bn), lambda j, i: (i, j)))
        args.append(res)
    return pl.pallas_call(
        functools.partial(_mm_body, n_x=len(xs), act=act, has_res=res is not None),
        grid=(n // bn, m // bm),
        in_specs=in_specs,
        out_specs=pl.BlockSpec((bm, bn), lambda j, i: (i, j)),
        out_shape=jax.ShapeDtypeStruct((m, n), out_dtype),
        scratch_shapes=[pltpu.VMEM((k, bn), BF16)],
        compiler_params=_cparams(("arbitrary", "arbitrary")),
        name=name,
    )(*args)


def _block_ones(n, seg):
    r = lax.broadcasted_iota(I32, (n, n), 0) // seg
    c = lax.broadcasted_iota(I32, (n, n), 1) // seg
    return jnp.where(r == c, 1.0, 0.0).astype(BF16)


def _seg_sum(x, ones_bd):
    hi = x.astype(BF16)
    lo = (x - hi.astype(F32)).astype(BF16)
    outs = []
    for c in range(x.shape[1] // MXU_TILE):
        sl = slice(c * MXU_TILE, (c + 1) * MXU_TILE)
        outs.append(jnp.dot(hi[:, sl], ones_bd, preferred_element_type=F32)
                    + jnp.dot(lo[:, sl], ones_bd, preferred_element_type=F32))
    return jnp.concatenate(outs, axis=1)


def _sigmoid(x):
    return 1.0 / (1.0 + jnp.exp(-x))


def _rwkv_pre_body(pa_ref, repl_ref, mu_ref, w0_ref, w2_ref, a0_ref, a2_ref, g2_ref,
                   kk_ref, ka_ref, rk_ref,
                   r_o, dec_o, k_o, v_o, kkn_o, b_o, g_o, bonus_o, *, period, d_a):
    pa = pa_ref[...]
    tm = pa.shape[0]
    row = lax.broadcasted_iota(I32, pa.shape, 0)
    prev = pltpu.roll(pa, 1, 0)
    repl = repl_ref[...]
    if repl.shape[0] != tm:
        repl = jnp.broadcast_to(repl[0:1, :], pa.shape)
    prev = jnp.where(row % period == 0, repl, prev)
    xm = pa + (prev - pa) * mu_ref[...]
    r = xm[:, 0:d_a]
    k = xm[:, d_a:2 * d_a]
    v = xm[:, 2 * d_a:3 * d_a]
    o = 3 * d_a
    wd = xm[:, o:o + LORA_W]
    ad = xm[:, o + LORA_W:o + LORA_W + LORA_A]
    gd = xm[:, o + LORA_W + LORA_A:o + LORA_W + LORA_A + LORA_G]

    z = w0_ref[...] + jnp.dot(jnp.tanh(wd), w2_ref[...], precision=HIGHEST, preferred_element_type=F32)
    w = jnp.minimum(z, 0.0) - jnp.log1p(jnp.exp(-jnp.abs(z))) - 0.5
    dec = jnp.exp(-jnp.exp(w))
    a = _sigmoid(a0_ref[...] + jnp.dot(ad, a2_ref[...], precision=HIGHEST, preferred_element_type=F32))
    g = jnp.dot(_sigmoid(gd).astype(BF16), g2_ref[...].astype(BF16), preferred_element_type=F32)

    ones_bd = _block_ones(MXU_TILE, RWKV_HEAD_DIM)
    kk = k * kk_ref[...]
    nrm = jnp.sqrt(_seg_sum(kk * kk, ones_bd))
    kkn = kk / jnp.maximum(nrm, 1e-12)
    kh = k * (1.0 + (a - 1.0) * ka_ref[...])
    bonus = _seg_sum(r * kh * rk_ref[...], ones_bd) * v

    r_o[...] = r
    dec_o[...] = dec
    k_o[...] = kh
    v_o[...] = v
    kkn_o[...] = kkn
    b_o[...] = kkn * a
    g_o[...] = g
    bonus_o[...] = bonus


def _rwkv_pre(proj, repl, period, row0, rows, tm, prm):
    d_a = prm["w0"].shape[1]
    d_ap = prm["mu"].shape[1]
    nt = rows // tm
    t0 = row0 // tm
    vec = lambda n: pl.BlockSpec((1, n), lambda i: (0, 0))
    full = lambda a: pl.BlockSpec(a.shape, lambda i: (0, 0))
    out_spec = pl.BlockSpec((tm, d_a), lambda i: (i, 0))
    outs = pl.pallas_call(
        functools.partial(_rwkv_pre_body, period=period, d_a=d_a),
        grid=(nt,),
        in_specs=[pl.BlockSpec((tm, d_ap), lambda i: (t0 + i, 0)),
                  pl.BlockSpec((repl.shape[0] // nt, d_ap), lambda i: (i, 0)),
                  vec(d_ap), vec(d_a), full(prm["w2"]), vec(d_a), full(prm["a2"]), full(prm["g2"]),
                  vec(d_a), vec(d_a), vec(d_a)],
        out_specs=[out_spec] * 8,
        out_shape=[jax.ShapeDtypeStruct((rows, d_a), F32)] * 8,
        compiler_params=_cparams(("arbitrary",)),
        name="rwkv_pre",
    )(proj, repl, prm["mu"], prm["w0"], prm["w2"], prm["a0"], prm["a2"], prm["g2"],
      prm["k_k"], prm["k_a"], prm["r_k"])
    return outs


def _rwkv_scan_body(*refs, tc, tp, nheads, nb, has_state):
    if has_state:
        s0_ref = refs[0]
        refs = refs[1:]
    (r_ref, w_ref, k_ref, v_ref, kk_ref, b_ref, g_ref, bonus_ref, lg_ref, lb_ref,
     o_ref, sout_ref, s_scr, vth_scr, ot_scr) = refs
    n = RWKV_HEAD_DIM
    c = pl.program_id(1)
    assert tp == 2 * n
    nh = nb * nheads
    heads = [(b, h) for b in range(nb) for h in range(nheads)]

    @pl.when(c == 0)
    def _():
        if has_state:
            zero = jnp.zeros((n, n), F32)
            for i, (b, h) in enumerate(heads):
                s0 = s0_ref[b, h]
                s_scr[i] = jnp.concatenate([s0, zero] if h % 2 == 0 else [zero, s0], axis=1)
        else:
            s_scr[...] = jnp.zeros_like(s_scr)

    for i, (b, h) in enumerate(heads):
        vh = v_ref[b, :, h * n:(h + 1) * n]
        if tp != tc:
            vh = jnp.concatenate([vh, jnp.zeros((tp - tc, n), F32)], axis=0)
        vth_scr[i * n:(i + 1) * n, :] = vh.T.astype(BF16)
    ot_scr[...] = jnp.zeros_like(ot_scr)

    trow = lax.broadcasted_iota(I32, (tp, 2 * n), 0)
    tcol = lax.broadcasted_iota(I32, (2 * n, tp), 1)
    head_par = lax.broadcasted_iota(I32, (nh, 1, 2 * n), 0) % 2
    lane_par = lax.broadcasted_iota(I32, (nh, 1, 2 * n), 2) // n
    own = head_par == lane_par

    sub = 32 if tc % 32 == 0 else 8

    def per_head(ref, base, masked):
        xs = [ref[b, pl.ds(base, sub), :] for b in range(nb)]
        x = jnp.stack([xs[b][:, (h // 2) * 2 * n:(h // 2 + 1) * 2 * n] for b, h in heads], axis=0)
        return jnp.where(own, x, 0.0) if masked else x

    def group(tg, carry):
        base = pl.multiple_of(tg * sub, sub)
        kk8 = per_head(kk_ref, base, True)
        b8 = per_head(b_ref, base, True)
        k8 = per_head(k_ref, base, True)
        w8 = per_head(w_ref, base, False)
        r8 = per_head(r_ref, base, False)
        st = s_scr[...]
        vth = vth_scr[...]
        o8 = None
        for s in range(sub):
            t = base + s
            pick = jnp.where(trow == t, 1.0, 0.0).astype(BF16)
            v_col = jnp.dot(vth, pick, preferred_element_type=F32).reshape(nh, n, 2 * n)
            sa = -jnp.sum(st * kk8[:, s:s + 1], axis=-1, keepdims=True)
            st = st * w8[:, s:s + 1] + sa * b8[:, s:s + 1] + v_col * k8[:, s:s + 1]
            q = (st * r8[:, s:s + 1]).reshape(nh * n, 2 * n).astype(BF16)
            put = jnp.where(tcol == t, 1.0, 0.0).astype(BF16)
            o_t = jnp.dot(q, put, preferred_element_type=F32)
            o8 = o_t if o8 is None else o8 + o_t
        s_scr[...] = st
        ot_scr[...] = ot_scr[...] + o8.reshape(nh, n, tp)
        return carry

    lax.fori_loop(0, tc // sub, group, 0)

    for b in range(nb):
        for hp in range(nheads // 2):
            parts = []
            for h in (2 * hp, 2 * hp + 1):
                oc = ot_scr[b * nheads + h]
                mean = jnp.mean(oc, axis=0, keepdims=True)
                d = oc - mean
                var = jnp.mean(d * d, axis=0, keepdims=True)
                on = d * lax.rsqrt(var + RWKV_GN_EPS)
                parts.append(on.T[:tc])
            sl = slice(hp * 2 * n, (hp + 1) * 2 * n)
            on2 = jnp.concatenate(parts, axis=1)
            o = (on2 * lg_ref[:, sl] + lb_ref[:, sl] + bonus_ref[b, :, sl]) * g_ref[b, :, sl]
            o_ref[b, :, sl] = o.astype(o_ref.dtype)

    @pl.when(c == pl.num_programs(1) - 1)
    def _():
        for i, (b, h) in enumerate(heads):
            s_h = s_scr[i]
            sout_ref[b, h] = s_h[:, 0:n] if h % 2 == 0 else s_h[:, n:2 * n]


def _rwkv_scan(pre, lnx_g, lnx_b, state, nseq, tseq, tc, nb):
    rows, d_a = pre[0].shape
    nheads = d_a // RWKV_HEAD_DIM
    nch = tseq // tc
    tp = max(tc, LANES)
    n = RWKV_HEAD_DIM
    nh = nb * nheads
    tok = pl.BlockSpec((nb, tc, d_a), lambda s, c: (s, c, 0))
    vec = pl.BlockSpec((1, d_a), lambda s, c: (0, 0))
    st = pl.BlockSpec((nb, nheads, n, n), lambda s, c: (s, 0, 0, 0))
    has_state = state is not None
    in_specs = ([st] if has_state else []) + [tok] * 8 + [vec, vec]
    args = ([state] if has_state else []) + [a.reshape(nseq, tseq, d_a) for a in pre] + [
        lnx_g.reshape(1, d_a), lnx_b.reshape(1, d_a)]
    oa, s_fin = pl.pallas_call(
        functools.partial(_rwkv_scan_body, tc=tc, tp=tp, nheads=nheads, nb=nb, has_state=has_state),
        grid=(nseq // nb, nch),
        in_specs=in_specs,
        out_specs=[tok, st],
        out_shape=[jax.ShapeDtypeStruct((nseq, tseq, d_a), BF16 if tc % 16 == 0 else F32),
                   jax.ShapeDtypeStruct((nseq, nheads, n, n), F32)],
        scratch_shapes=[pltpu.VMEM((nh, n, 2 * n), F32),
                        pltpu.VMEM((nh * n, tp), BF16),
                        pltpu.VMEM((nh, n, tp), F32)],
        compiler_params=_cparams(("arbitrary", "arbitrary")),
        name="rwkv_scan",
    )(*args)
    return oa.reshape(rows, d_a), s_fin


def _rope_body(q_ref, k_ref, v_ref, c_ref, s1_ref, s2_ref, qs_o, qf_o, kf_o, kb_o, vb_o, kt_o):
    c, s1, s2 = c_ref[...], s1_ref[...], s2_ref[...]
    half = ROT_DIM // 2
    w = q_ref.shape[1]

    def rot(x):
        return x * c + pltpu.roll(x, w - half, 1) * s1 + pltpu.roll(x, half, 1) * s2

    qr = rot(q_ref[...]) * ATTN_SCALE
    qf_o[...] = qr
    qs_o[...] = qr.astype(BF16)
    kr = rot(k_ref[...])
    kf_o[...] = kr
    kb_o[...] = kr.astype(BF16)
    vb_o[...] = v_ref[...].astype(BF16)
    for hh in range(w // DIFF_QK_DIM):
        kt_o[hh] = kr[:, hh * DIFF_QK_DIM:(hh + 1) * DIFF_QK_DIM].T


def _rope(proj, pos, d_ap, d_q, tm, nseq_p, tseq_p):
    n_tok = proj.shape[0]
    w = MXU_TILE
    half = ROT_DIM // 2
    inv_freq = ROPE_THETA ** (-jnp.arange(half, dtype=F32) / half)
    ang = pos.astype(F32)[:, None] * inv_freq[None, :]
    cos, sin = jnp.cos(ang), jnp.sin(ang)
    pad = DIFF_QK_DIM - ROT_DIM
    ones = jnp.ones((n_tok, pad), F32)
    zeros = jnp.zeros((n_tok, pad), F32)
    zh = jnp.zeros((n_tok, half), F32)
    reps = w // DIFF_QK_DIM
    c_t = jnp.tile(jnp.concatenate([cos, cos, ones], axis=1), (1, reps))
    s1_t = jnp.tile(jnp.concatenate([-sin, zh, zeros], axis=1), (1, reps))
    s2_t = jnp.tile(jnp.concatenate([zh, sin, zeros], axis=1), (1, reps))
    nq = d_q // w
    qb, kb, vb = d_ap // w, d_ap // w + nq, d_ap // w + 2 * nq
    tab = pl.BlockSpec((tm, w), lambda i, j: (i, 0))
    out = pl.BlockSpec((tm, w), lambda i, j: (i, j))
    hpb = w // DIFF_QK_DIM
    tps = tseq_p // tm
    kt_spec = pl.BlockSpec((None, hpb, DIFF_QK_DIM, tm),
                           lambda i, j: (jnp.minimum(i // tps, nseq_p), j, 0, i % tps))
    return pl.pallas_call(
        _rope_body,
        grid=(n_tok // tm, nq),
        in_specs=[pl.BlockSpec((tm, w), lambda i, j: (i, qb + j)),
                  pl.BlockSpec((tm, w), lambda i, j: (i, kb + j)),
                  pl.BlockSpec((tm, w), lambda i, j: (i, vb + j)),
                  tab, tab, tab],
        out_specs=[out, out, out, out, out, kt_spec],
        out_shape=[jax.ShapeDtypeStruct((n_tok, d_q), BF16),
                   jax.ShapeDtypeStruct((n_tok, d_q), F32),
                   jax.ShapeDtypeStruct((n_tok, d_q), F32),
                   jax.ShapeDtypeStruct((n_tok, d_q), BF16),
                   jax.ShapeDtypeStruct((n_tok, d_q), BF16),
                   jax.ShapeDtypeStruct((nseq_p + 1, d_q // DIFF_QK_DIM, DIFF_QK_DIM, tseq_p), F32)],
        compiler_params=_cparams(("arbitrary", "arbitrary")),
        name="rope",
    )(proj, proj, proj, c_t, s1_t, s2_t)


def _lambda(lam_ref, lam_init):
    lp = lam_ref[...]
    l1 = jnp.sum(lp[0:1] * lp[1:2], axis=-1, keepdims=True)
    l2 = jnp.sum(lp[2:3] * lp[3:4], axis=-1, keepdims=True)
    return jnp.exp(l1) - jnp.exp(l2) + lam_init


def _subln(o, g_ref, lam_init):
    inv = lax.rsqrt(jnp.mean(o * o, axis=-1, keepdims=True) + SUBLN_EPS)
    return (o * inv) * g_ref[...] * (1.0 - lam_init)


def _online_softmax_update(s, v, m_scr, l_scr, acc_scr):
    m_prev = m_scr[...]
    m_new = jnp.maximum(m_prev, jnp.max(s, axis=-1, keepdims=True))
    alpha = jnp.exp(m_prev - m_new)
    p = jnp.exp(s - m_new[:, 0:1])
    l_scr[...] = alpha * l_scr[...] + jnp.sum(p, axis=-1, keepdims=True)
    acc_scr[...] = alpha[:, 0:1] * acc_scr[...] + jnp.dot(p.astype(BF16), v, preferred_element_type=F32)
    m_scr[...] = m_new


def _attn_prompt_body(q_ref, k_ref, v_ref, lam_ref, g_ref, o_ref, qq_scr, m_scr, l_scr, acc_scr,
                      *, tq, lam_init):
    qi = pl.program_id(2)
    ki = pl.program_id(3)

    @pl.when(ki == 0)
    def _():
        q = q_ref[...]
        lane = lax.broadcasted_iota(I32, q.shape, 1)
        zero = jnp.zeros_like(q)
        qq_scr[0:tq] = jnp.where(lane < DIFF_QK_DIM, q, zero)
        qq_scr[tq:2 * tq] = jnp.where(lane >= DIFF_QK_DIM, q, zero)
        m_scr[...] = jnp.full_like(m_scr, NEG_INF)
        l_scr[...] = jnp.zeros_like(l_scr)
        acc_scr[...] = jnp.zeros_like(acc_scr)

    rb = min(tq, 256)

    def tile(diagonal):
        for r0 in range(0, 2 * tq, rb):
            rows = slice(r0, r0 + rb)
            nk = min(tq, (r0 % tq) + rb) if diagonal else tq
            s = lax.dot_general(qq_scr[rows], k_ref[0:nk], (((1,), (1,)), ((), ())),
                                preferred_element_type=F32)
            if diagonal:
                row = (r0 % tq) + lax.broadcasted_iota(I32, s.shape, 0)
                col = lax.broadcasted_iota(I32, s.shape, 1)
                s = jnp.where(row >= col, s, NEG_INF)
            m_prev = m_scr[rows]
            m_new = jnp.maximum(m_prev, jnp.max(s, axis=-1, keepdims=True))
            alpha = jnp.exp(m_prev - m_new)
            p = jnp.exp(s - jnp.concatenate([m_new] * (nk // LANES), axis=1))
            l_scr[rows] = alpha * l_scr[rows] + jnp.sum(p, axis=-1, keepdims=True)
            acc_scr[rows] = alpha * acc_scr[rows] + jnp.dot(p.astype(BF16), v_ref[0:nk],
                                                            preferred_element_type=F32)
            m_scr[rows] = m_new

    @pl.when(ki < qi)
    def _():
        tile(False)

    @pl.when(ki == qi)
    def _():
        tile(True)
        lam = _lambda(lam_ref, lam_init)
        o1 = acc_scr[0:tq] / l_scr[0:tq, 0:1]
        o2 = acc_scr[tq:2 * tq] / l_scr[tq:2 * tq, 0:1]
        o_ref[...] = _subln(o1 - lam * o2, g_ref, lam_init).astype(o_ref.dtype)


def _attn_prompt(qs, kb, vb, lam_p, subln_g, nseq, tseq, tq, lam_init):
    rows, d_q = qs.shape
    nh = d_q // DIFF_V_DIM
    nq = tseq // tq
    w = DIFF_V_DIM
    return pl.pallas_call(
        functools.partial(_attn_prompt_body, tq=tq, lam_init=lam_init),
        grid=(nseq, nh, nq, nq),
        in_specs=[pl.BlockSpec((tq, w), lambda b, h, i, j: (b * nq + i, h)),
                  pl.BlockSpec((tq, w), lambda b, h, i, j: (b * nq + jnp.minimum(i, j), h)),
                  pl.BlockSpec((tq, w), lambda b, h, i, j: (b * nq + jnp.minimum(i, j), h)),
                  pl.BlockSpec(lam_p.shape, lambda b, h, i, j: (0, 0)),
                  pl.BlockSpec((1, w), lambda b, h, i, j: (0, 0))],
        out_specs=pl.BlockSpec((tq, w), lambda b, h, i, j: (b * nq + i, h)),
        out_shape=jax.ShapeDtypeStruct((nseq * tseq, d_q), BF16),
        scratch_shapes=[pltpu.VMEM((2 * tq, w), BF16),
                        pltpu.VMEM((2 * tq, LANES), F32),
                        pltpu.VMEM((2 * tq, LANES), F32),
                        pltpu.VMEM((2 * tq, w), F32)],
        compiler_params=_cparams(("arbitrary", "arbitrary", "arbitrary", "arbitrary")),
        name="attn_prompt",
    )(qs, kb, vb, lam_p, subln_g.reshape(1, w))


def _attn_sample_body(pt_ref, q_ref, *refs, ts, lam_init, pages_per_step):
    kc_refs = refs[:pages_per_step]
    vc_refs = refs[pages_per_step:2 * pages_per_step]
    kn_ref, vn_ref, lam_ref, g_ref, o_ref, qbd_scr, m_scr, l_scr, acc_scr = refs[2 * pages_per_step:]
    p = pl.program_id(1)
    nheads_qk = q_ref.shape[1] // DIFF_QK_DIM

    @pl.when(p == 0)
    def _():
        q = q_ref[...]
        head = lax.broadcasted_iota(I32, q.shape, 1) // DIFF_QK_DIM
        zero = jnp.zeros_like(q)
        rows = [jnp.where(head == m, q, zero) for m in range(nheads_qk)]
        qbd_scr[...] = jnp.concatenate(rows, axis=0).astype(BF16)
        m_scr[...] = jnp.full_like(m_scr, NEG_INF)
        l_scr[...] = jnp.zeros_like(l_scr)
        acc_scr[...] = jnp.zeros_like(acc_scr)

    nh_v = nheads_qk // 2
    w = DIFF_V_DIM

    def update(s, v_heads):
        m_prev = m_scr[...]
        m_new = jnp.maximum(m_prev, jnp.max(s, axis=-1, keepdims=True))
        alpha = jnp.exp(m_prev - m_new)
        pr = jnp.exp(s - jnp.concatenate([m_new] * (s.shape[1] // LANES), axis=1))
        l_scr[...] = alpha * l_scr[...] + jnp.sum(pr, axis=-1, keepdims=True)
        m_scr[...] = m_new
        pb = pr.astype(BF16)
        for h in range(nh_v):
            rows = slice(2 * h * ts, (2 * h + 2) * ts)
            acc_scr[rows] = alpha[rows] * acc_scr[rows] + jnp.dot(pb[rows], v_heads[h],
                                                                  preferred_element_type=F32)

    kt = jnp.concatenate([r[...].reshape(nheads_qk * DIFF_QK_DIM, r.shape[2]) for r in kc_refs],
                         axis=1).astype(BF16)
    s_past = jnp.dot(qbd_scr[...], kt, preferred_element_type=F32)
    npos = vc_refs[0].shape[0] // nh_v
    v_past = [jnp.concatenate([r[pl.ds(h, npos, stride=nh_v), :] for r in vc_refs], axis=0).astype(BF16)
              for h in range(nh_v)]
    update(s_past, v_past)

    @pl.when(p == pl.num_programs(1) - 1)
    def _():
        padk = jnp.zeros((LANES - ts, kn_ref.shape[1]), F32)
        kn = jnp.concatenate([kn_ref[...], padk], axis=0).astype(BF16)
        vn = jnp.concatenate([vn_ref[...], padk], axis=0).astype(BF16)
        s = lax.dot_general(qbd_scr[...], kn, (((1,), (1,)), ((), ())), preferred_element_type=F32)
        row = lax.broadcasted_iota(I32, s.shape, 0) % ts
        col = lax.broadcasted_iota(I32, s.shape, 1)
        s = jnp.where(row >= col, s, NEG_INF)
        update(s, [vn[:, h * w:(h + 1) * w] for h in range(nh_v)])
        lam = _lambda(lam_ref, lam_init)
        for h in range(nh_v):
            r1 = slice(2 * h * ts, (2 * h + 1) * ts)
            r2 = slice((2 * h + 1) * ts, (2 * h + 2) * ts)
            o1 = acc_scr[r1] / l_scr[r1, 0:1]
            o2 = acc_scr[r2] / l_scr[r2, 0:1]
            o_ref[:, h * w:(h + 1) * w] = _subln(o1 - lam * o2, g_ref, lam_init).astype(o_ref.dtype)


def _attn_sample(qs, kb, vb, cache_k, cache_v, layer_j, page_flat, lam_p, subln_g, nb, ts, npages, lam_init):
    d_q = qs.shape[1]
    nrow = (d_q // DIFF_QK_DIM) * ts
    gpp = next(g for g in (8, 4, 2, 1) if npages % g == 0)
    tok = pl.BlockSpec((ts, d_q), lambda b, p, pt: (b, 0))
    cache_k = jnp.transpose(cache_k, (0, 1, 3, 4, 2))
    cache_v = cache_v.reshape(cache_v.shape[:2] + (-1, cache_v.shape[-1]))

    def page(cache, g):
        return pl.BlockSpec((None, None) + cache.shape[2:],
                            lambda b, p, pt: (layer_j, pt[b * npages + p * gpp + g]) + (0,) * (cache.ndim - 2))

    return pl.pallas_call(
        functools.partial(_attn_sample_body, ts=ts, lam_init=lam_init, pages_per_step=gpp),
        grid_spec=pltpu.PrefetchScalarGridSpec(
            num_scalar_prefetch=1,
            grid=(nb, npages // gpp),
            in_specs=[tok] + [page(cache_k, g) for g in range(gpp)] + [page(cache_v, g) for g in range(gpp)]
                     + [tok, tok,
                        pl.BlockSpec(lam_p.shape, lambda b, p, pt: (0, 0)),
                        pl.BlockSpec((1, DIFF_V_DIM), lambda b, p, pt: (0, 0))],
            out_specs=tok,
            scratch_shapes=[pltpu.VMEM((nrow, d_q), BF16),
                            pltpu.VMEM((nrow, LANES), F32),
                            pltpu.VMEM((nrow, LANES), F32),
                            pltpu.VMEM((nrow, DIFF_V_DIM), F32)]),
        out_shape=jax.ShapeDtypeStruct((nb * ts, d_q), F32),
        compiler_params=_cparams(("arbitrary", "arbitrary")),
        name="attn_sample",
    )(page_flat, qs, *([cache_k] * gpp), *([cache_v] * gpp), kb, vb, lam_p, subln_g.reshape(1, DIFF_V_DIM))


def _gmlp_body(z_ref, lg_ref, lb_ref, wm_ref, bm_ref, y_ref, *maybe_v, d_c):
    u = z_ref[:, 0:d_c]
    v = z_ref[:, d_c:2 * d_c]
    mean = jnp.mean(v, axis=-1, keepdims=True)
    d = v - mean
    var = jnp.mean(d * d, axis=-1, keepdims=True)
    vn = (d * lax.rsqrt(var + GMLP_LN_EPS)) * lg_ref[...] + lb_ref[...]
    if maybe_v:
        maybe_v[0][...] = vn
    gd = d_c // GMLP_GROUPS
    vb = vn.astype(BF16)
    for g in range(GMLP_GROUPS):
        sl = slice(g * gd, (g + 1) * gd)
        mixed = jnp.dot(wm_ref[g].astype(BF16), vb[:, sl], preferred_element_type=F32) + bm_ref[g]
        y_ref[:, sl] = (u[:, sl] * mixed).astype(y_ref.dtype)


def _gmlp(z, row0, rows, ln_g, ln_b, wmix, bmix, emit_v):
    d_c = z.shape[1] // 2
    tm = CHUNK
    t0 = row0 // tm
    gd = d_c // GMLP_GROUPS
    out_specs = [pl.BlockSpec((tm, d_c), lambda i: (i, 0))]
    out_shape = [jax.ShapeDtypeStruct((rows, d_c), BF16)]
    if emit_v:
        out_specs.append(pl.BlockSpec((tm, d_c), lambda i: (i, 0)))
        out_shape.append(jax.ShapeDtypeStruct((rows, d_c), F32))
    return pl.pallas_call(
        functools.partial(_gmlp_body, d_c=d_c),
        grid=(rows // tm,),
        in_specs=[pl.BlockSpec((tm, 2 * d_c), lambda i: (t0 + i, 0)),
                  pl.BlockSpec((1, d_c), lambda i: (0, 0)),
                  pl.BlockSpec((1, d_c), lambda i: (0, 0)),
                  pl.BlockSpec((GMLP_GROUPS, tm, tm), lambda i: (0, 0, 0)),
                  pl.BlockSpec((GMLP_GROUPS, tm, gd), lambda i: (0, 0, 0))],
        out_specs=out_specs,
        out_shape=out_shape,
        compiler_params=_cparams(("arbitrary",)),
        name="gmlp_gate",
    )(z, ln_g.reshape(1, d_c), ln_b.reshape(1, d_c), wmix, bmix)


def _store_token_rows(ref2, x2d):
    dc = x2d.shape[1] // LANES
    for c in range(dc):
        ref2[pl.ds(c, x2d.shape[0], stride=dc), :] = x2d[:, c * LANES:(c + 1) * LANES]


def _load_token_rows(ref2, dc):
    n = ref2.shape[0] // dc
    return jnp.concatenate([ref2[pl.ds(c, n, stride=dc), :] for c in range(dc)], axis=1)


def _router_body(h_ref, g_ref, wr_ref, br_ref, xn_o, idx_o, cw_o):
    x = h_ref[...]
    inv = lax.rsqrt(jnp.mean(x * x, axis=-1, keepdims=True) + NORM_EPS)
    xn = (x * inv) * g_ref[...]
    _store_token_rows(xn_o, xn)
    logits = jnp.dot(xn, wr_ref[...], precision=HIGHEST, preferred_element_type=F32) + br_ref[...]
    lane = lax.broadcasted_iota(I32, logits.shape, 1)
    big = jnp.int32(1 << 20)
    ninf = jnp.float32(-jnp.inf)

    def first_max(vals):
        mx = jnp.max(vals, axis=-1, keepdims=True)
        ix = jnp.min(jnp.where(vals == mx, lane, big), axis=-1, keepdims=True)
        return mx, ix

    gl = jnp.where(lane < N_EXP_GROUPS, logits, ninf)
    gmax, gtop = first_max(gl)
    gate = 1.0 / jnp.sum(jnp.exp(gl - gmax), axis=-1, keepdims=True)
    lo = N_EXP_GROUPS + EXP_PER_GROUP * gtop
    el = jnp.where((lane >= lo) & (lane < lo + EXP_PER_GROUP), logits, ninf)
    m1, i1 = first_max(el)
    m2, i2 = first_max(jnp.where(lane == i1, ninf, el))
    p2 = jnp.exp(m2 - m1)
    c1 = gate / (1.0 + p2)
    c2 = gate * p2 / (1.0 + p2)
    idx_o[...] = jnp.where(lane == 0, i1 - N_EXP_GROUPS, jnp.where(lane == 1, i2 - N_EXP_GROUPS, 0))
    cw_o[...] = jnp.where(lane == 0, c1, jnp.where(lane == 1, c2, 0.0))


def _router(h, g, w_grp, b_grp, w_exp, b_exp, tm):
    n_tok, d = h.shape
    ncol = w_grp.shape[1] + w_exp.shape[1]
    wr = jnp.concatenate([w_grp, w_exp, jnp.zeros((d, LANES - ncol), F32)], axis=1)
    br = jnp.concatenate([b_grp, b_exp, jnp.zeros((LANES - ncol,), F32)]).reshape(1, LANES)
    tok = lambda w: pl.BlockSpec((tm, w), lambda i: (i, 0))
    return pl.pallas_call(
        _router_body,
        grid=(n_tok // tm,),
        in_specs=[tok(d), pl.BlockSpec((1, d), lambda i: (0, 0)),
                  pl.BlockSpec((d, LANES), lambda i: (0, 0)),
                  pl.BlockSpec((1, LANES), lambda i: (0, 0))],
        out_specs=[pl.BlockSpec((tm * (d // LANES), LANES), lambda i: (i, 0)), tok(LANES), tok(LANES)],
        out_shape=[jax.ShapeDtypeStruct((n_tok * (d // LANES), LANES), F32),
                   jax.ShapeDtypeStruct((n_tok, LANES), I32),
                   jax.ShapeDtypeStruct((n_tok, LANES), F32)],
        compiler_params=_cparams(("arbitrary",)),
        name="moe_router",
    )(h, g.reshape(1, d), wr, br)


def _gather_copy(x_hbm, xbuf, sem, slot, i, tok, dc):
    src = x_hbm.at[pl.ds(pl.multiple_of(tok * dc, dc), dc)]
    dst = xbuf.at[slot, pl.ds(pl.multiple_of(i * dc, dc), dc)]
    return pltpu.make_async_copy(src, dst, sem.at[slot])


def _moe_ffn_body(te_ref, nvalid_ref, tok_ref, x_hbm, rw_ref, w1_ref, w3_ref, w2_ref, y_ref,
                  xbuf, sem, w1b, w3b, w2b, *, tm):
    t = pl.program_id(0)
    nv = nvalid_ref[0]
    valid = t < nv
    dc = w1_ref.shape[0] // LANES

    ahead = xbuf.shape[0]

    def issue(tile, slot):
        def body(i, c):
            _gather_copy(x_hbm, xbuf, sem, slot, i, tok_ref[tile * tm + i], dc).start()
            return c
        lax.fori_loop(0, tm, body, 0, unroll=8)

    for first in range(ahead - 1):
        @pl.when((t == 0) & (first < nv))
        def _():
            issue(first, first)

    @pl.when(t + ahead - 1 < nv)
    def _():
        issue(t + ahead - 1, (t + ahead - 1) % ahead)

    @pl.when(valid)
    def _():
        prev = te_ref[jnp.maximum(t - 1, 0)]

        @pl.when((t == 0) | (te_ref[t] != prev))
        def _():
            w1b[...] = w1_ref[...].astype(BF16)
            w3b[...] = w3_ref[...].astype(BF16)
            w2b[...] = w2_ref[...].astype(BF16)

        slot = t % ahead

        pltpu.make_async_copy(x_hbm.at[pl.ds(0, tm * dc)], xbuf.at[slot], sem.at[slot]).wait()

        x = _load_token_rows(xbuf.at[slot], dc).astype(BF16)
        h1 = jnp.dot(x, w1b[...], preferred_element_type=F32)
        h3 = jnp.dot(x, w3b[...], preferred_element_type=F32)
        hh = (h1 * _sigmoid(h1)) * h3 * rw_ref[...]
        _store_token_rows(y_ref, jnp.dot(hh.astype(BF16), w2b[...], preferred_element_type=F32))

    @pl.when(jnp.logical_not(valid))
    def _():
        y_ref[...] = jnp.zeros_like(y_ref)


def _moe_ffn(xn, row_tok, row_w, tile_exp, nvalid_tiles, w1, w3, w2, layer, n_tiles, tm):
    d, f = w1.shape[2], w1.shape[3]
    dc = d // LANES
    rows = n_tiles * tm
    def last_valid(t, nv):
        return jnp.minimum(t, jnp.maximum(nv[0] - 1, 0))
    return pl.pallas_call(
        functools.partial(_moe_ffn_body, tm=tm),
        grid_spec=pltpu.PrefetchScalarGridSpec(
            num_scalar_prefetch=3,
            grid=(n_tiles,),
            in_specs=[pl.BlockSpec(memory_space=pl.ANY),
                      pl.BlockSpec((tm, 1), lambda t, te, nv, tk: (last_valid(t, nv), 0)),
                      pl.BlockSpec((None, None, d, f), lambda t, te, nv, tk: (layer, te[t], 0, 0)),
                      pl.BlockSpec((None, None, d, f), lambda t, te, nv, tk: (layer, te[t], 0, 0)),
                      pl.BlockSpec((None, None, f, d), lambda t, te, nv, tk: (layer, te[t], 0, 0))],
            out_specs=pl.BlockSpec((tm * dc, LANES), lambda t, te, nv, tk: (t, 0)),
            scratch_shapes=[pltpu.VMEM((4, tm * dc, LANES), F32), pltpu.SemaphoreType.DMA((4,)),
                            pltpu.VMEM((d, f), BF16), pltpu.VMEM((d, f), BF16), pltpu.VMEM((f, d), BF16)]),
        out_shape=jax.ShapeDtypeStruct((rows * dc, LANES), F32),
        compiler_params=_cparams(("arbitrary",)),
        name="moe_ffn",
    )(tile_exp, nvalid_tiles, row_tok, xn, row_w, w1, w3, w2)


def _combine_copy(y_hbm, buf, sem, slot, k, i, row, dc):
    src = y_hbm.at[pl.ds(pl.multiple_of(row * dc, dc), dc)]
    dst = buf.at[slot, k, pl.ds(pl.multiple_of(i * dc, dc), dc)]
    return pltpu.make_async_copy(src, dst, sem.at[slot])


def _moe_combine_body(pos_ref, h_ref, y_hbm, *rest, tm, final_eps):
    if final_eps is None:
        o_ref, buf, sem = rest
    else:
        gf_ref, o_ref, on_ref, buf, sem = rest
    t = pl.program_id(0)
    nt = pl.num_programs(0)
    dc = h_ref.shape[1] // LANES

    def issue(tile, slot):
        def body(i, c):
            a = 2 * (tile * tm + i)
            _combine_copy(y_hbm, buf, sem, slot, 0, i, pos_ref[a], dc).start()
            _combine_copy(y_hbm, buf, sem, slot, 1, i, pos_ref[a + 1], dc).start()
            return c
        lax.fori_loop(0, tm, body, 0, unroll=8)

    ahead = buf.shape[0]

    for first in range(ahead - 1):
        @pl.when((t == 0) & (first < nt))
        def _():
            issue(first, first)

    @pl.when(t + ahead - 1 < nt)
    def _():
        issue(t + ahead - 1, (t + ahead - 1) % ahead)

    slot = t % ahead

    for k in range(2):
        pltpu.make_async_copy(y_hbm.at[pl.ds(0, tm * dc)], buf.at[slot, k], sem.at[slot]).wait()

    out = h_ref[...] + _load_token_rows(buf.at[slot, 0], dc) + _load_token_rows(buf.at[slot, 1], dc)
    o_ref[...] = out
    if final_eps is not None:
        inv = lax.rsqrt(jnp.mean(out * out, axis=-1, keepdims=True) + final_eps)
        on_ref[...] = (out * inv) * gf_ref[...]


def _moe_combine(h, ys, pos, tm, final_g=None):
    n_tok, d = h.shape
    tok = pl.BlockSpec((tm, d), lambda i, p: (i, 0))
    in_specs = [tok, pl.BlockSpec(memory_space=pl.ANY)]
    args = [pos, h, ys]
    out_specs = [tok]
    out_shape = [jax.ShapeDtypeStruct((n_tok, d), F32)]
    if final_g is not None:
        in_specs.append(pl.BlockSpec((1, d), lambda i, p: (0, 0)))
        args.append(final_g.reshape(1, d))
        out_specs.append(tok)
        out_shape.append(jax.ShapeDtypeStruct((n_tok, d), F32))
    return pl.pallas_call(
        functools.partial(_moe_combine_body, tm=tm, final_eps=None if final_g is None else NORM_EPS),
        grid_spec=pltpu.PrefetchScalarGridSpec(
            num_scalar_prefetch=1,
            grid=(n_tok // tm,),
            in_specs=in_specs,
            out_specs=out_specs,
            scratch_shapes=[pltpu.VMEM((3, 2, tm * (d // LANES), LANES), F32), pltpu.SemaphoreType.DMA((3,))]),
        out_shape=out_shape,
        compiler_params=_cparams(("arbitrary",)),
        name="moe_combine",
    )(*args)


def _hier_moe(h, g, w_grp, b_grp, w_exp, b_exp, w1, w3, w2, layer, final_g=None):
    n_tok, d = h.shape
    n_exp = w1.shape[1]
    tm = MOE_TILE
    xn, idx, cw = _router(h, g, w_grp, b_grp, w_exp, b_exp, tm=256)

    n_asg = 2 * n_tok
    flat_e = idx[:, :2].reshape(n_asg)
    flat_c = cw[:, :2].reshape(n_asg)
    onehot = (flat_e[:, None] == jnp.arange(n_exp, dtype=I32)[None, :]).astype(I32)
    csum = jnp.cumsum(onehot, axis=0)
    counts = csum[-1]
    rank = jnp.take_along_axis(csum - onehot, flat_e[:, None], axis=1)[:, 0]
    padded = ((counts + tm - 1) // tm) * tm
    ends = jnp.cumsum(padded)
    pos = (ends - padded)[flat_e] + rank
    n_tiles = n_asg // tm + n_exp
    rows = n_tiles * tm
    upd = jnp.stack([jnp.arange(n_asg, dtype=I32) // 2, lax.bitcast_convert_type(flat_c, I32)], axis=1)
    row_meta = jnp.zeros((rows, 2), I32).at[pos].set(upd)
    row_tok = row_meta[:, 0]
    row_w = lax.bitcast_convert_type(row_meta[:, 1], F32).reshape(rows, 1)
    nvalid = (ends[-1] // tm).astype(I32).reshape(1)
    tile_ids = jnp.minimum(jnp.arange(n_tiles, dtype=I32), nvalid - 1)
    tile_exp = jnp.searchsorted(ends, tile_ids * tm, side="right").astype(I32)

    ys = _moe_ffn(xn, row_tok, row_w, tile_exp, nvalid, w1, w3, w2, layer, n_tiles, tm)
    return _moe_combine(h, ys, pos.astype(I32), 128, final_g)


def kernel(x_prompt, x_sample, cache_k, cache_v, page_table, state_wkv, state_shift, norm_mix, norm_ffn, norm_final, even_w_in, even_w_out, rwkv_mu, rwkv_w0, rwkv_w2, rwkv_a0, rwkv_a2, rwkv_g2, rwkv_k_k, rwkv_k_a, rwkv_r_k, rwkv_lnx_g, rwkv_lnx_b, diff_lam_q1, diff_lam_k1, diff_lam_q2, diff_lam_k2, diff_subln_g, gmlp_w_in, gmlp_ln_g, gmlp_ln_b, gmlp_w_s, gmlp_b_s, gmlp_w_out, moe_w_grp, moe_b_grp, moe_w_exp, moe_b_exp, moe_w1, moe_w3, moe_w2):
    bp, tp, d = x_prompt.shape
    bs, ts, _ = x_sample.shape
    n_p, n_s = bp * tp, bs * ts
    n_tok = n_p + n_s
    depth = norm_mix.shape[0]
    npages, psz = page_table.shape[1], cache_k.shape[2]
    past_len = npages * psz
    d_a = rwkv_w0.shape[1]
    d_ap = rwkv_mu.shape[1]
    d_q = cache_k.shape[3] * cache_k.shape[4]
    n_pool = cache_k.shape[1]
    nheads = d_a // RWKV_HEAD_DIM
    tm = 512 if n_p % 512 == 0 and n_s % 512 == 0 else 128
    tq = next((t for t in (1024, 512) if tp % t == 0), 128)
    tc = min(tp, CHUNK)

    h = jnp.concatenate([x_prompt.reshape(n_p, d), x_sample.reshape(n_s, d)], axis=0)
    pos = jnp.concatenate([jnp.tile(jnp.arange(tp), bp), jnp.tile(past_len + jnp.arange(ts), bs)])
    moe = lambda l, hh, fg=None: _hier_moe(hh, norm_ffn[l], moe_w_grp[l], moe_b_grp[l], moe_w_exp[l],
                                          moe_b_exp[l], moe_w1, moe_w3, moe_w2, l, fg)

    outs = {k: [] for k in ("k_p", "v_p", "wkv_p", "sh_p", "k_s", "v_s", "wkv_s", "sh_s", "gv_s")}
    y_norm = None
    for layer in range(depth):
        j = layer // 2
        xn = _rmsnorm(h, norm_mix[layer], NORM_EPS, BF16, tm)
        last = layer == depth - 1
        if layer % 2 == 0:
            proj = _matmul([xn], even_w_in[j], bm=tm, bn=even_w_in.shape[2] // 5, name="even_in_proj")
            prm = dict(mu=rwkv_mu[j].reshape(1, d_ap), w0=rwkv_w0[j].reshape(1, d_a), w2=rwkv_w2[j],
                       a0=rwkv_a0[j].reshape(1, d_a), a2=rwkv_a2[j], g2=rwkv_g2[j],
                       k_k=rwkv_k_k[j].reshape(1, d_a), k_a=rwkv_k_a[j].reshape(1, d_a),
                       r_k=rwkv_r_k[j].reshape(1, d_a))
            tmr = 256 if tp % 256 == 0 else 128
            tile_last = proj[tmr - 1:n_p:tmr, :d_ap]
            seq_last_s = proj[n_p + ts - 1::ts, :d_ap]
            starts_seq = (jnp.arange(n_p // tmr) % (tp // tmr) == 0)[:, None]
            bnd = jnp.concatenate([jnp.zeros((1, d_ap), F32), tile_last[:-1]], axis=0)
            bnd = jnp.where(starts_seq, 0.0, bnd)
            repl_p = jnp.zeros((n_p // tmr, 8, d_ap), F32).at[:, 0].set(bnd)
            repl_p = repl_p.reshape(n_p // tmr * 8, d_ap)
            repl_s = jnp.repeat(state_shift[j], ts, axis=0)
            pre_p = _rwkv_pre(proj, repl_p, tmr, 0, n_p, tmr, prm)
            pre_s = _rwkv_pre(proj, repl_s, ts, n_p, n_s, tmr, prm)
            oa_p, wkv_p = _rwkv_scan(pre_p, rwkv_lnx_g[j], rwkv_lnx_b[j], None, bp, tp, tc,
                                     2 if bp % 2 == 0 else 1)
            oa_s, wkv_s = _rwkv_scan(pre_s, rwkv_lnx_g[j], rwkv_lnx_b[j], state_wkv[j], bs, ts, ts,
                                     2 if bs % 2 == 0 else 1)

            qs, qf, kf, kb, vb, kt = _rope(proj, pos, d_ap, d_q, tm, bp, tp)
            vf = proj[:, d_ap + 2 * d_q:]
            lam_init = 0.8 - 0.6 * math.exp(-0.3 * layer)
            lam_p = jnp.stack([diff_lam_q1[j], diff_lam_k1[j], diff_lam_q2[j], diff_lam_k2[j]])
            ob_p = _attn_prompt(qs[:n_p], kb[:n_p], vb[:n_p], lam_p, diff_subln_g[j], bp, tp, tq, lam_init)
            ob_s = _attn_sample(qf[n_p:], kf[n_p:], vf[n_p:], cache_k, cache_v, j,
                                page_table.reshape(-1).astype(I32),
                                lam_p, diff_subln_g[j], bs, ts, npages, lam_init)
            oa = jnp.concatenate([oa_p, oa_s.astype(BF16)], axis=0)
            ob = jnp.concatenate([ob_p, ob_s.astype(BF16)], axis=0)
            h = _matmul([oa, ob], even_w_out[j], bm=tm, bn=1024, res=h, name="even_out_proj")

            outs["k_p"].append(jnp.transpose(kt[:bp], (0, 3, 1, 2)))
            outs["v_p"].append(vf[:n_p].reshape(bp, tp, -1, DIFF_V_DIM))
            outs["wkv_p"].append(wkv_p)
            outs["sh_p"].append(tile_last[tp // tmr - 1::tp // tmr])
            outs["k_s"].append(kf[n_p:].reshape(bs, ts, -1, DIFF_QK_DIM))
            outs["v_s"].append(vf[n_p:].reshape(bs, ts, -1, DIFF_V_DIM))
            outs["wkv_s"].append(wkv_s)
            outs["sh_s"].append(seq_last_s)
        else:
            d_c = gmlp_ln_g.shape[1]
            z = _matmul([xn], gmlp_w_in[j], bm=tm, bn=1024, act="gelu", name="gmlp_in_proj")
            ws = gmlp_w_s[j]
            bsb = gmlp_b_s[j]
            gd = d_c // GMLP_GROUPS
            lp = min(tp, CHUNK)
            wm_p = jnp.tril(ws[:, :lp, :lp])
            bm_p = jnp.broadcast_to(bsb[:, :lp, None], (GMLP_GROUPS, lp, gd))
            rep = CHUNK // ts
            eye = jnp.eye(rep, dtype=F32)
            wm_s = jnp.einsum("ab,gts->gatbs", eye, jnp.tril(ws[:, :ts, :ts])).reshape(GMLP_GROUPS, CHUNK, CHUNK)
            bm_s = jnp.broadcast_to(jnp.tile(bsb[:, :ts], (1, rep))[:, :, None], (GMLP_GROUPS, CHUNK, gd))
            (y_p,) = _gmlp(z, 0, n_p, gmlp_ln_g[j], gmlp_ln_b[j], wm_p, bm_p, False)
            y_s, v_rows = _gmlp(z, n_p, n_s, gmlp_ln_g[j], gmlp_ln_b[j], wm_s, bm_s, True)
            yin = jnp.concatenate([y_p, y_s], axis=0)
            h = _matmul([yin], gmlp_w_out[j], bm=tm, bn=1024, res=h, name="gmlp_out_proj")
            outs["gv_s"].append(v_rows.reshape(bs, ts, d_c))
        res = moe(layer, h, norm_final if last else None)
        h = res[0]
        if last:
            y_norm = res[1]

    y_prompt = y_norm[:n_p].reshape(bp, tp, d)
    y_sample = y_norm[n_p:].reshape(bs, ts, d)
    st = lambda k: jnp.stack(outs[k])
    return (y_prompt, y_sample, st("k_p"), st("v_p"), st("wkv_p"), st("sh_p"),
            st("k_s"), st("v_s"), st("wkv_s"), st("sh_s"), st("gv_s"))
```

```python
import functools
import math

import jax
import jax.numpy as jnp
from jax import lax
from jax.experimental import pallas as pl
from jax.experimental.pallas import tpu as pltpu

F32 = jnp.float32
BF16 = jnp.bfloat16
I32 = jnp.int32

RWKV_HEAD_DIM = 64
RWKV_GN_EPS = 64e-5
DIFF_QK_DIM = 64
DIFF_V_DIM = 128
ROT_DIM = 16
ROPE_THETA = 500000.0
ATTN_SCALE = DIFF_QK_DIM ** -0.5
NEG_INF = -1e30
NORM_EPS = 1e-6
SUBLN_EPS = 1e-5
GMLP_LN_EPS = 1e-5
GMLP_GROUPS = 8
CHUNK = 128
N_EXP_GROUPS = 4
EXP_PER_GROUP = 8
LORA_W, LORA_A, LORA_G = 64, 64, 128

LANES = 128
MXU_TILE = 256
VMEM_LIMIT = 56 * 1024 * 1024
MOE_TILE = 256
HIGHEST = lax.Precision.HIGHEST


def _cparams(sem):
    return pltpu.CompilerParams(dimension_semantics=sem, vmem_limit_bytes=VMEM_LIMIT)


def _rmsnorm_body(x_ref, g_ref, o_ref, *, eps):
    x = x_ref[...]
    inv = lax.rsqrt(jnp.mean(x * x, axis=-1, keepdims=True) + eps)
    o_ref[...] = ((x * inv) * g_ref[...]).astype(o_ref.dtype)


def _rmsnorm(x, g, eps, out_dtype, tm):
    m, d = x.shape
    return pl.pallas_call(
        functools.partial(_rmsnorm_body, eps=eps),
        grid=(m // tm,),
        in_specs=[pl.BlockSpec((tm, d), lambda i: (i, 0)),
                  pl.BlockSpec((1, d), lambda i: (0, 0))],
        out_specs=pl.BlockSpec((tm, d), lambda i: (i, 0)),
        out_shape=jax.ShapeDtypeStruct((m, d), out_dtype),
        compiler_params=_cparams(("arbitrary",)),
        name="rmsnorm",
    )(x, g.reshape(1, d))


def _gelu_exact(x):
    return 0.5 * x * (1.0 + lax.erf(x * (1.0 / math.sqrt(2.0))))


def _mm_body(*refs, n_x, act, has_res):
    xs = refs[:n_x]
    w_ref = refs[n_x]
    res_ref = refs[n_x + 1] if has_res else None
    o_ref = refs[n_x + 1 + has_res]
    wb_ref = refs[n_x + 2 + has_res]

    @pl.when(pl.program_id(1) == 0)
    def _():
        wb_ref[...] = w_ref[...].astype(BF16)

    kx = w_ref.shape[0] // n_x
    acc = None
    for i, x_ref in enumerate(xs):
        part = jnp.dot(x_ref[...], wb_ref[i * kx:(i + 1) * kx, :], preferred_element_type=F32)
        acc = part if acc is None else acc + part
    if act == "gelu":
        acc = _gelu_exact(acc)
    if has_res:
        acc = acc + res_ref[...]
    o_ref[...] = acc.astype(o_ref.dtype)


def _matmul(xs, w, *, bm, bn, act=None, res=None, out_dtype=F32, name="matmul"):
    m = xs[0].shape[0]
    k, n = w.shape
    kx = k // len(xs)
    in_specs = [pl.BlockSpec((bm, kx), lambda j, i: (i, 0)) for _ in xs]
    in_specs.append(pl.BlockSpec((k, bn), lambda j, i: (0, j)))
    args = list(xs) + [w]
    if res is not None:
        in_specs.append(pl.BlockSpec((bm, bn), lambda j, i: (i, j)))
        args.append(res)
    return pl.pallas_call(
        functools.partial(_mm_body, n_x=len(xs), act=act, has_res=res is not None),
        grid=(n // bn, m // bm),
        in_specs=in_specs,
        out_specs=pl.BlockSpec((bm, bn), lambda j, i: (i, j)),
        out_shape=jax.ShapeDtypeStruct((m, n), out_dtype),
        scratch_shapes=[pltpu.VMEM((k, bn), BF16)],
        compiler_params=_cparams(("arbitrary", "arbitrary")),
        name=name,
    )(*args)


def _block_ones(n, seg):
    r = lax.broadcasted_iota(I32, (n, n), 0) // seg
    c = lax.broadcasted_iota(I32, (n, n), 1) // seg
    return jnp.where(r == c, 1.0, 0.0).astype(BF16)


def _seg_sum(x, ones_bd):
    hi = x.astype(BF16)
    lo = (x - hi.astype(F32)).astype(BF16)
    outs = []
    for c in range(x.shape[1] // MXU_TILE):
        sl = slice(c * MXU_TILE, (c + 1) * MXU_TILE)
        outs.append(jnp.dot(hi[:, sl], ones_bd, preferred_element_type=F32)
                    + jnp.dot(lo[:, sl], ones_bd, preferred_element_type=F32))
    return jnp.concatenate(outs, axis=1)


def _sigmoid(x):
    return 1.0 / (1.0 + jnp.exp(-x))


def _rwkv_pre_body(pa_ref, repl_ref, mu_ref, w0_ref, w2_ref, a0_ref, a2_ref, g2_ref,
                   kk_ref, ka_ref, rk_ref,
                   r_o, dec_o, k_o, v_o, kkn_o, b_o, g_o, bonus_o, *, period, d_a):
    pa = pa_ref[...]
    tm = pa.shape[0]
    row = lax.broadcasted_iota(I32, pa.shape, 0)
    prev = pltpu.roll(pa, 1, 0)
    repl = repl_ref[...]
    if repl.shape[0] != tm:
        repl = jnp.broadcast_to(repl[0:1, :], pa.shape)
    prev = jnp.where(row % period == 0, repl, prev)
    xm = pa + (prev - pa) * mu_ref[...]
    r = xm[:, 0:d_a]
    k = xm[:, d_a:2 * d_a]
    v = xm[:, 2 * d_a:3 * d_a]
    o = 3 * d_a
    wd = xm[:, o:o + LORA_W]
    ad = xm[:, o + LORA_W:o + LORA_W + LORA_A]
    gd = xm[:, o + LORA_W + LORA_A:o + LORA_W + LORA_A + LORA_G]

    z = w0_ref[...] + jnp.dot(jnp.tanh(wd), w2_ref[...], precision=HIGHEST, preferred_element_type=F32)
    w = jnp.minimum(z, 0.0) - jnp.log1p(jnp.exp(-jnp.abs(z))) - 0.5
    dec = jnp.exp(-jnp.exp(w))
    a = _sigmoid(a0_ref[...] + jnp.dot(ad, a2_ref[...], precision=HIGHEST, preferred_element_type=F32))
    g = jnp.dot(_sigmoid(gd).astype(BF16), g2_ref[...].astype(BF16), preferred_element_type=F32)

    ones_bd = _block_ones(MXU_TILE, RWKV_HEAD_DIM)
    kk = k * kk_ref[...]
    nrm = jnp.sqrt(_seg_sum(kk * kk, ones_bd))
    kkn = kk / jnp.maximum(nrm, 1e-12)
    kh = k * (1.0 + (a - 1.0) * ka_ref[...])
    bonus = _seg_sum(r * kh * rk_ref[...], ones_bd) * v

    r_o[...] = r
    dec_o[...] = dec
    k_o[...] = kh
    v_o[...] = v
    kkn_o[...] = kkn
    b_o[...] = kkn * a
    g_o[...] = g
    bonus_o[...] = bonus


def _rwkv_pre(proj, repl, period, row0, rows, tm, prm):
    d_a = prm["w0"].shape[1]
    d_ap = prm["mu"].shape[1]
    nt = rows // tm
    t0 = row0 // tm
    vec = lambda n: pl.BlockSpec((1, n), lambda i: (0, 0))
    full = lambda a: pl.BlockSpec(a.shape, lambda i: (0, 0))
    out_spec = pl.BlockSpec((tm, d_a), lambda i: (i, 0))
    outs = pl.pallas_call(
        functools.partial(_rwkv_pre_body, period=period, d_a=d_a),
        grid=(nt,),
        in_specs=[pl.BlockSpec((tm, d_ap), lambda i: (t0 + i, 0)),
                  pl.BlockSpec((repl.shape[0] // nt, d_ap), lambda i: (i, 0)),
                  vec(d_ap), vec(d_a), full(prm["w2"]), vec(d_a), full(prm["a2"]), full(prm["g2"]),
                  vec(d_a), vec(d_a), vec(d_a)],
        out_specs=[out_spec] * 8,
        out_shape=[jax.ShapeDtypeStruct((rows, d_a), F32)] * 8,
        compiler_params=_cparams(("arbitrary",)),
        name="rwkv_pre",
    )(proj, repl, prm["mu"], prm["w0"], prm["w2"], prm["a0"], prm["a2"], prm["g2"],
      prm["k_k"], prm["k_a"], prm["r_k"])
    return outs


def _rwkv_scan_body(*refs, tc, tp, nheads, nb, has_state):
    if has_state:
        s0_ref = refs[0]
        refs = refs[1:]
    (r_ref, w_ref, k_ref, v_ref, kk_ref, b_ref, g_ref, bonus_ref, lg_ref, lb_ref,
     o_ref, sout_ref, s_scr, vth_scr, ot_scr) = refs
    n = RWKV_HEAD_DIM
    c = pl.program_id(1)
    assert tp == 2 * n
    nh = nb * nheads
    heads = [(b, h) for b in range(nb) for h in range(nheads)]

    @pl.when(c == 0)
    def _():
        if has_state:
            zero = jnp.zeros((n, n), F32)
            for i, (b, h) in enumerate(heads):
                s0 = s0_ref[b, h]
                s_scr[i] = jnp.concatenate([s0, zero] if h % 2 == 0 else [zero, s0], axis=1)
        else:
            s_scr[...] = jnp.zeros_like(s_scr)

    for i, (b, h) in enumerate(heads):
        vh = v_ref[b, :, h * n:(h + 1) * n]
        if tp != tc:
            vh = jnp.concatenate([vh, jnp.zeros((tp - tc, n), F32)], axis=0)
        vth_scr[i * n:(i + 1) * n, :] = vh.T.astype(BF16)
    ot_scr[...] = jnp.zeros_like(ot_scr)

    trow = lax.broadcasted_iota(I32, (tp, 2 * n), 0)
    tcol = lax.broadcasted_iota(I32, (2 * n, tp), 1)
    head_par = lax.broadcasted_iota(I32, (nh, 1, 2 * n), 0) % 2
    lane_par = lax.broadcasted_iota(I32, (nh, 1, 2 * n), 2) // n
    own = head_par == lane_par

    sub = 32 if tc % 32 == 0 else 8

    def per_head(ref, base, masked):
        xs = [ref[b, pl.ds(base, sub), :] for b in range(nb)]
        x = jnp.stack([xs[b][:, (h // 2) * 2 * n:(h // 2 + 1) * 2 * n] for b, h in heads], axis=0)
        return jnp.where(own, x, 0.0) if masked else x

    def group(tg, carry):
        base = pl.multiple_of(tg * sub, sub)
        kk8 = per_head(kk_ref, base, True)
        b8 = per_head(b_ref, base, True)
        k8 = per_head(k_ref, base, True)
        w8 = per_head(w_ref, base, False)
        r8 = per_head(r_ref, base, False)
        st = s_scr[...]
        vth = vth_scr[...]
        o8 = None
        for s in range(sub):
            t = base + s
            pick = jnp.where(trow == t, 1.0, 0.0).astype(BF16)
            v_col = jnp.dot(vth, pick, preferred_element_type=F32).reshape(nh, n, 2 * n)
            sa = -jnp.sum(st * kk8[:, s:s + 1], axis=-1, keepdims=True)
            st = st * w8[:, s:s + 1] + sa * b8[:, s:s + 1] + v_col * k8[:, s:s + 1]
            q = (st * r8[:, s:s + 1]).reshape(nh * n, 2 * n).astype(BF16)
            put = jnp.where(tcol == t, 1.0, 0.0).astype(BF16)
            o_t = jnp.dot(q, put, preferred_element_type=F32)
            o8 = o_t if o8 is None else o8 + o_t
        s_scr[...] = st
        ot_scr[...] = ot_scr[...] + o8.reshape(nh, n, tp)
        return carry

    lax.fori_loop(0, tc // sub, group, 0)

    for b in range(nb):
        for hp in range(nheads // 2):
            parts = []
            for h in (2 * hp, 2 * hp + 1):
                oc = ot_scr[b * nheads + h]
                mean = jnp.mean(oc, axis=0, keepdims=True)
                d = oc - mean
                var = jnp.mean(d * d, axis=0, keepdims=True)
                on = d * lax.rsqrt(var + RWKV_GN_EPS)
                parts.append(on.T[:tc])
            sl = slice(hp * 2 * n, (hp + 1) * 2 * n)
            on2 = jnp.concatenate(parts, axis=1)
            o = (on2 * lg_ref[:, sl] + lb_ref[:, sl] + bonus_ref[b, :, sl]) * g_ref[b, :, sl]
            o_ref[b, :, sl] = o.astype(o_ref.dtype)

    @pl.when(c == pl.num_programs(1) - 1)
    def _():
        for i, (b, h) in enumerate(heads):
            s_h = s_scr[i]
            sout_ref[b, h] = s_h[:, 0:n] if h % 2 == 0 else s_h[:, n:2 * n]


def _rwkv_scan(pre, lnx_g, lnx_b, state, nseq, tseq, tc, nb):
    rows, d_a = pre[0].shape
    nheads = d_a // RWKV_HEAD_DIM
    nch = tseq // tc
    tp = max(tc, LANES)
    n = RWKV_HEAD_DIM
    nh = nb * nheads
    tok = pl.BlockSpec((nb, tc, d_a), lambda s, c: (s, c, 0))
    vec = pl.BlockSpec((1, d_a), lambda s, c: (0, 0))
    st = pl.BlockSpec((nb, nheads, n, n), lambda s, c: (s, 0, 0, 0))
    has_state = state is not None
    in_specs = ([st] if has_state else []) + [tok] * 8 + [vec, vec]
    args = ([state] if has_state else []) + [a.reshape(nseq, tseq, d_a) for a in pre] + [
        lnx_g.reshape(1, d_a), lnx_b.reshape(1, d_a)]
    oa, s_fin = pl.pallas_call(
        functools.partial(_rwkv_scan_body, tc=tc, tp=tp, nheads=nheads, nb=nb, has_state=has_state),
        grid=(nseq // nb, nch),
        in_specs=in_specs,
        out_specs=[tok, st],
        out_shape=[jax.ShapeDtypeStruct((nseq, tseq, d_a), BF16 if tc % 16 == 0 else F32),
                   jax.ShapeDtypeStruct((nseq, nheads, n, n), F32)],
        scratch_shapes=[pltpu.VMEM((nh, n, 2 * n), F32),
                        pltpu.VMEM((nh * n, tp), BF16),
                        pltpu.VMEM((nh, n, tp), F32)],
        compiler_params=_cparams(("arbitrary", "arbitrary")),
        name="rwkv_scan",
    )(*args)
    return oa.reshape(rows, d_a), s_fin


def _rope_body(q_ref, k_ref, v_ref, c_ref, s1_ref, s2_ref, qs_o, qf_o, kf_o, kb_o, vb_o, kt_o):
    c, s1, s2 = c_ref[...], s1_ref[...], s2_ref[...]
    half = ROT_DIM // 2
    w = q_ref.shape[1]

    def rot(x):
        return x * c + pltpu.roll(x, w - half, 1) * s1 + pltpu.roll(x, half, 1) * s2

    qr = rot(q_ref[...]) * ATTN_SCALE
    qf_o[...] = qr
    qs_o[...] = qr.astype(BF16)
    kr = rot(k_ref[...])
    kf_o[...] = kr
    kb_o[...] = kr.astype(BF16)
    vb_o[...] = v_ref[...].astype(BF16)
    for hh in range(w // DIFF_QK_DIM):
        kt_o[hh] = kr[:, hh * DIFF_QK_DIM:(hh + 1) * DIFF_QK_DIM].T


def _rope(proj, pos, d_ap, d_q, tm):
    n_tok = proj.shape[0]
    w = MXU_TILE
    half = ROT_DIM // 2
    inv_freq = ROPE_THETA ** (-jnp.arange(half, dtype=F32) / half)
    ang = pos.astype(F32)[:, None] * inv_freq[None, :]
    cos, sin = jnp.cos(ang), jnp.sin(ang)
    pad = DIFF_QK_DIM - ROT_DIM
    ones = jnp.ones((n_tok, pad), F32)
    zeros = jnp.zeros((n_tok, pad), F32)
    zh = jnp.zeros((n_tok, half), F32)
    reps = w // DIFF_QK_DIM
    c_t = jnp.tile(jnp.concatenate([cos, cos, ones], axis=1), (1, reps))
    s1_t = jnp.tile(jnp.concatenate([-sin, zh, zeros], axis=1), (1, reps))
    s2_t = jnp.tile(jnp.concatenate([zh, sin, zeros], axis=1), (1, reps))
    nq = d_q // w
    qb, kb, vb = d_ap // w, d_ap // w + nq, d_ap // w + 2 * nq
    tab = pl.BlockSpec((tm, w), lambda i, j: (i, 0))
    out = pl.BlockSpec((tm, w), lambda i, j: (i, j))
    hpb = w // DIFF_QK_DIM
    kt_spec = pl.BlockSpec((None, hpb, DIFF_QK_DIM, tm), lambda i, j: (i, j, 0, 0))
    return pl.pallas_call(
        _rope_body,
        grid=(n_tok // tm, nq),
        in_specs=[pl.BlockSpec((tm, w), lambda i, j: (i, qb + j)),
                  pl.BlockSpec((tm, w), lambda i, j: (i, kb + j)),
                  pl.BlockSpec((tm, w), lambda i, j: (i, vb + j)),
                  tab, tab, tab],
        out_specs=[out, out, out, out, out, kt_spec],
        out_shape=[jax.ShapeDtypeStruct((n_tok, d_q), BF16),
                   jax.ShapeDtypeStruct((n_tok, d_q), F32),
                   jax.ShapeDtypeStruct((n_tok, d_q), F32),
                   jax.ShapeDtypeStruct((n_tok, d_q), BF16),
                   jax.ShapeDtypeStruct((n_tok, d_q), BF16),
                   jax.ShapeDtypeStruct((n_tok // tm, d_q // DIFF_QK_DIM, DIFF_QK_DIM, tm), F32)],
        compiler_params=_cparams(("arbitrary", "arbitrary")),
        name="rope",
    )(proj, proj, proj, c_t, s1_t, s2_t)


def _lambda(lam_ref, lam_init):
    lp = lam_ref[...]
    l1 = jnp.sum(lp[0:1] * lp[1:2], axis=-1, keepdims=True)
    l2 = jnp.sum(lp[2:3] * lp[3:4], axis=-1, keepdims=True)
    return jnp.exp(l1) - jnp.exp(l2) + lam_init


def _subln(o, g_ref, lam_init):
    inv = lax.rsqrt(jnp.mean(o * o, axis=-1, keepdims=True) + SUBLN_EPS)
    return (o * inv) * g_ref[...] * (1.0 - lam_init)


def _online_softmax_update(s, v, m_scr, l_scr, acc_scr):
    m_prev = m_scr[...]
    m_new = jnp.maximum(m_prev, jnp.max(s, axis=-1, keepdims=True))
    alpha = jnp.exp(m_prev - m_new)
    p = jnp.exp(s - m_new[:, 0:1])
    l_scr[...] = alpha * l_scr[...] + jnp.sum(p, axis=-1, keepdims=True)
    acc_scr[...] = alpha[:, 0:1] * acc_scr[...] + jnp.dot(p.astype(BF16), v, preferred_element_type=F32)
    m_scr[...] = m_new


def _attn_prompt_body(q_ref, k_ref, v_ref, lam_ref, g_ref, o_ref, qq_scr, m_scr, l_scr, acc_scr,
                      *, tq, lam_init):
    qi = pl.program_id(2)
    ki = pl.program_id(3)

    @pl.when(ki == 0)
    def _():
        q = q_ref[...]
        lane = lax.broadcasted_iota(I32, q.shape, 1)
        zero = jnp.zeros_like(q)
        qq_scr[0:tq] = jnp.where(lane < DIFF_QK_DIM, q, zero)
        qq_scr[tq:2 * tq] = jnp.where(lane >= DIFF_QK_DIM, q, zero)
        m_scr[...] = jnp.full_like(m_scr, NEG_INF)
        l_scr[...] = jnp.zeros_like(l_scr)
        acc_scr[...] = jnp.zeros_like(acc_scr)

    rb = min(tq, 256)

    def tile(diagonal):
        for r0 in range(0, 2 * tq, rb):
            rows = slice(r0, r0 + rb)
            nk = min(tq, (r0 % tq) + rb) if diagonal else tq
            s = lax.dot_general(qq_scr[rows], k_ref[0:nk], (((1,), (1,)), ((), ())),
                                preferred_element_type=F32)
            if diagonal:
                row = (r0 % tq) + lax.broadcasted_iota(I32, s.shape, 0)
                col = lax.broadcasted_iota(I32, s.shape, 1)
                s = jnp.where(row >= col, s, NEG_INF)
            m_prev = m_scr[rows]
            m_new = jnp.maximum(m_prev, jnp.max(s, axis=-1, keepdims=True))
            alpha = jnp.exp(m_prev - m_new)
            p = jnp.exp(s - jnp.concatenate([m_new] * (nk // LANES), axis=1))
            l_scr[rows] = alpha * l_scr[rows] + jnp.sum(p, axis=-1, keepdims=True)
            acc_scr[rows] = alpha * acc_scr[rows] + jnp.dot(p.astype(BF16), v_ref[0:nk],
                                                            preferred_element_type=F32)
            m_scr[rows] = m_new

    @pl.when(ki < qi)
    def _():
        tile(False)

    @pl.when(ki == qi)
    def _():
        tile(True)
        lam = _lambda(lam_ref, lam_init)
        o1 = acc_scr[0:tq] / l_scr[0:tq, 0:1]
        o2 = acc_scr[tq:2 * tq] / l_scr[tq:2 * tq, 0:1]
        o_ref[...] = _subln(o1 - lam * o2, g_ref, lam_init).astype(o_ref.dtype)


def _attn_prompt(qs, kb, vb, lam_p, subln_g, nseq, tseq, tq, lam_init):
    rows, d_q = qs.shape
    nh = d_q // DIFF_V_DIM
    nq = tseq // tq
    w = DIFF_V_DIM
    return pl.pallas_call(
        functools.partial(_attn_prompt_body, tq=tq, lam_init=lam_init),
        grid=(nseq, nh, nq, nq),
        in_specs=[pl.BlockSpec((tq, w), lambda b, h, i, j: (b * nq + i, h)),
                  pl.BlockSpec((tq, w), lambda b, h, i, j: (b * nq + jnp.minimum(i, j), h)),
                  pl.BlockSpec((tq, w), lambda b, h, i, j: (b * nq + jnp.minimum(i, j), h)),
                  pl.BlockSpec(lam_p.shape, lambda b, h, i, j: (0, 0)),
                  pl.BlockSpec((1, w), lambda b, h, i, j: (0, 0))],
        out_specs=pl.BlockSpec((tq, w), lambda b, h, i, j: (b * nq + i, h)),
        out_shape=jax.ShapeDtypeStruct((nseq * tseq, d_q), BF16),
        scratch_shapes=[pltpu.VMEM((2 * tq, w), BF16),
                        pltpu.VMEM((2 * tq, LANES), F32),
                        pltpu.VMEM((2 * tq, LANES), F32),
                        pltpu.VMEM((2 * tq, w), F32)],
        compiler_params=_cparams(("arbitrary", "arbitrary", "arbitrary", "arbitrary")),
        name="attn_prompt",
    )(qs, kb, vb, lam_p, subln_g.reshape(1, w))


def _attn_sample_body(pt_ref, q_ref, *refs, ts, lam_init, pages_per_step):
    kc_refs = refs[:pages_per_step]
    vc_refs = refs[pages_per_step:2 * pages_per_step]
    kn_ref, vn_ref, lam_ref, g_ref, o_ref, qbd_scr, m_scr, l_scr, acc_scr = refs[2 * pages_per_step:]
    p = pl.program_id(1)
    nheads_qk = q_ref.shape[1] // DIFF_QK_DIM

    @pl.when(p == 0)
    def _():
        q = q_ref[...]
        head = lax.broadcasted_iota(I32, q.shape, 1) // DIFF_QK_DIM
        zero = jnp.zeros_like(q)
        rows = [jnp.where(head == m, q, zero) for m in range(nheads_qk)]
        qbd_scr[...] = jnp.concatenate(rows, axis=0).astype(BF16)
        m_scr[...] = jnp.full_like(m_scr, NEG_INF)
        l_scr[...] = jnp.zeros_like(l_scr)
        acc_scr[...] = jnp.zeros_like(acc_scr)

    nh_v = nheads_qk // 2
    w = DIFF_V_DIM

    def update(s, v_heads):
        m_prev = m_scr[...]
        m_new = jnp.maximum(m_prev, jnp.max(s, axis=-1, keepdims=True))
        alpha = jnp.exp(m_prev - m_new)
        pr = jnp.exp(s - jnp.concatenate([m_new] * (s.shape[1] // LANES), axis=1))
        l_scr[...] = alpha * l_scr[...] + jnp.sum(pr, axis=-1, keepdims=True)
        m_scr[...] = m_new
        pb = pr.astype(BF16)
        for h in range(nh_v):
            rows = slice(2 * h * ts, (2 * h + 2) * ts)
            acc_scr[rows] = alpha[rows] * acc_scr[rows] + jnp.dot(pb[rows], v_heads[h],
                                                                  preferred_element_type=F32)

    kt = jnp.concatenate([r[...].reshape(nheads_qk * DIFF_QK_DIM, r.shape[2]) for r in kc_refs],
                         axis=1).astype(BF16)
    s_past = jnp.dot(qbd_scr[...], kt, preferred_element_type=F32)
    npos = vc_refs[0].shape[0] // nh_v
    v_past = [jnp.concatenate([r[pl.ds(h, npos, stride=nh_v), :] for r in vc_refs], axis=0).astype(BF16)
              for h in range(nh_v)]
    update(s_past, v_past)

    @pl.when(p == pl.num_programs(1) - 1)
    def _():
        padk = jnp.zeros((LANES - ts, kn_ref.shape[1]), F32)
        kn = jnp.concatenate([kn_ref[...], padk], axis=0).astype(BF16)
        vn = jnp.concatenate([vn_ref[...], padk], axis=0).astype(BF16)
        s = lax.dot_general(qbd_scr[...], kn, (((1,), (1,)), ((), ())), preferred_element_type=F32)
        row = lax.broadcasted_iota(I32, s.shape, 0) % ts
        col = lax.broadcasted_iota(I32, s.shape, 1)
        s = jnp.where(row >= col, s, NEG_INF)
        update(s, [vn[:, h * w:(h + 1) * w] for h in range(nh_v)])
        lam = _lambda(lam_ref, lam_init)
        for h in range(nh_v):
            r1 = slice(2 * h * ts, (2 * h + 1) * ts)
            r2 = slice((2 * h + 1) * ts, (2 * h + 2) * ts)
            o1 = acc_scr[r1] / l_scr[r1, 0:1]
            o2 = acc_scr[r2] / l_scr[r2, 0:1]
            o_ref[:, h * w:(h + 1) * w] = _subln(o1 - lam * o2, g_ref, lam_init).astype(o_ref.dtype)


def _attn_sample(qs, kb, vb, cache_k, cache_v, layer_j, page_flat, lam_p, subln_g, nb, ts, npages, lam_init):
    d_q = qs.shape[1]
    nrow = (d_q // DIFF_QK_DIM) * ts
    gpp = next(g for g in (8, 4, 2, 1) if npages % g == 0)
    tok = pl.BlockSpec((ts, d_q), lambda b, p, pt: (b, 0))
    cache_k = jnp.transpose(cache_k, (0, 1, 3, 4, 2))
    cache_v = cache_v.reshape(cache_v.shape[:2] + (-1, cache_v.shape[-1]))

    def page(cache, g):
        return pl.BlockSpec((None, None) + cache.shape[2:],
                            lambda b, p, pt: (layer_j, pt[b * npages + p * gpp + g]) + (0,) * (cache.ndim - 2))

    return pl.pallas_call(
        functools.partial(_attn_sample_body, ts=ts, lam_init=lam_init, pages_per_step=gpp),
        grid_spec=pltpu.PrefetchScalarGridSpec(
            num_scalar_prefetch=1,
            grid=(nb, npages // gpp),
            in_specs=[tok] + [page(cache_k, g) for g in range(gpp)] + [page(cache_v, g) for g in range(gpp)]
                     + [tok, tok,
                        pl.BlockSpec(lam_p.shape, lambda b, p, pt: (0, 0)),
                        pl.BlockSpec((1, DIFF_V_DIM), lambda b, p, pt: (0, 0))],
            out_specs=tok,
            scratch_shapes=[pltpu.VMEM((nrow, d_q), BF16),
                            pltpu.VMEM((nrow, LANES), F32),
                            pltpu.VMEM((nrow, LANES), F32),
                            pltpu.VMEM((nrow, DIFF_V_DIM), F32)]),
        out_shape=jax.ShapeDtypeStruct((nb * ts, d_q), F32),
        compiler_params=_cparams(("arbitrary", "arbitrary")),
        name="attn_sample",
    )(page_flat, qs, *([cache_k] * gpp), *([cache_v] * gpp), kb, vb, lam_p, subln_g.reshape(1, DIFF_V_DIM))


def _gmlp_body(z_ref, lg_ref, lb_ref, wm_ref, bm_ref, y_ref, *maybe_v, d_c):
    u = z_ref[:, 0:d_c]
    v = z_ref[:, d_c:2 * d_c]
    mean = jnp.mean(v, axis=-1, keepdims=True)
    d = v - mean
    var = jnp.mean(d * d, axis=-1, keepdims=True)
    vn = (d * lax.rsqrt(var + GMLP_LN_EPS)) * lg_ref[...] + lb_ref[...]
    if maybe_v:
        maybe_v[0][...] = vn
    gd = d_c // GMLP_GROUPS
    vb = vn.astype(BF16)
    for g in range(GMLP_GROUPS):
        sl = slice(g * gd, (g + 1) * gd)
        mixed = jnp.dot(wm_ref[g].astype(BF16), vb[:, sl], preferred_element_type=F32) + bm_ref[g]
        y_ref[:, sl] = (u[:, sl] * mixed).astype(y_ref.dtype)


def _gmlp(z, row0, rows, ln_g, ln_b, wmix, bmix, emit_v):
    d_c = z.shape[1] // 2
    tm = CHUNK
    t0 = row0 // tm
    gd = d_c // GMLP_GROUPS
    out_specs = [pl.BlockSpec((tm, d_c), lambda i: (i, 0))]
    out_shape = [jax.ShapeDtypeStruct((rows, d_c), BF16)]
    if emit_v:
        out_specs.append(pl.BlockSpec((tm, d_c), lambda i: (i, 0)))
        out_shape.append(jax.ShapeDtypeStruct((rows, d_c), F32))
    return pl.pallas_call(
        functools.partial(_gmlp_body, d_c=d_c),
        grid=(rows // tm,),
        in_specs=[pl.BlockSpec((tm, 2 * d_c), lambda i: (t0 + i, 0)),
                  pl.BlockSpec((1, d_c), lambda i: (0, 0)),
                  pl.BlockSpec((1, d_c), lambda i: (0, 0)),
                  pl.BlockSpec((GMLP_GROUPS, tm, tm), lambda i: (0, 0, 0)),
                  pl.BlockSpec((GMLP_GROUPS, tm, gd), lambda i: (0, 0, 0))],
        out_specs=out_specs,
        out_shape=out_shape,
        compiler_params=_cparams(("arbitrary",)),
        name="gmlp_gate",
    )(z, ln_g.reshape(1, d_c), ln_b.reshape(1, d_c), wmix, bmix)


def _store_token_rows(ref2, x2d):
    dc = x2d.shape[1] // LANES
    for c in range(dc):
        ref2[pl.ds(c, x2d.shape[0], stride=dc), :] = x2d[:, c * LANES:(c + 1) * LANES]


def _load_token_rows(ref2, dc):
    n = ref2.shape[0] // dc
    return jnp.concatenate([ref2[pl.ds(c, n, stride=dc), :] for c in range(dc)], axis=1)


def _router_body(h_ref, g_ref, wr_ref, br_ref, xn_o, idx_o, cw_o):
    x = h_ref[...]
    inv = lax.rsqrt(jnp.mean(x * x, axis=-1, keepdims=True) + NORM_EPS)
    xn = (x * inv) * g_ref[...]
    _store_token_rows(xn_o, xn)
    logits = jnp.dot(xn, wr_ref[...], precision=HIGHEST, preferred_element_type=F32) + br_ref[...]
    lane = lax.broadcasted_iota(I32, logits.shape, 1)
    big = jnp.int32(1 << 20)
    ninf = jnp.float32(-jnp.inf)

    def first_max(vals):
        mx = jnp.max(vals, axis=-1, keepdims=True)
        ix = jnp.min(jnp.where(vals == mx, lane, big), axis=-1, keepdims=True)
        return mx, ix

    gl = jnp.where(lane < N_EXP_GROUPS, logits, ninf)
    gmax, gtop = first_max(gl)
    gate = 1.0 / jnp.sum(jnp.exp(gl - gmax), axis=-1, keepdims=True)
    lo = N_EXP_GROUPS + EXP_PER_GROUP * gtop
    el = jnp.where((lane >= lo) & (lane < lo + EXP_PER_GROUP), logits, ninf)
    m1, i1 = first_max(el)
    m2, i2 = first_max(jnp.where(lane == i1, ninf, el))
    p2 = jnp.exp(m2 - m1)
    c1 = gate / (1.0 + p2)
    c2 = gate * p2 / (1.0 + p2)
    idx_o[...] = jnp.where(lane == 0, i1 - N_EXP_GROUPS, jnp.where(lane == 1, i2 - N_EXP_GROUPS, 0))
    cw_o[...] = jnp.where(lane == 0, c1, jnp.where(lane == 1, c2, 0.0))


def _router(h, g, w_grp, b_grp, w_exp, b_exp, tm):
    n_tok, d = h.shape
    ncol = w_grp.shape[1] + w_exp.shape[1]
    wr = jnp.concatenate([w_grp, w_exp, jnp.zeros((d, LANES - ncol), F32)], axis=1)
    br = jnp.concatenate([b_grp, b_exp, jnp.zeros((LANES - ncol,), F32)]).reshape(1, LANES)
    tok = lambda w: pl.BlockSpec((tm, w), lambda i: (i, 0))
    return pl.pallas_call(
        _router_body,
        grid=(n_tok // tm,),
        in_specs=[tok(d), pl.BlockSpec((1, d), lambda i: (0, 0)),
                  pl.BlockSpec((d, LANES), lambda i: (0, 0)),
                  pl.BlockSpec((1, LANES), lambda i: (0, 0))],
        out_specs=[pl.BlockSpec((tm * (d // LANES), LANES), lambda i: (i, 0)), tok(LANES), tok(LANES)],
        out_shape=[jax.ShapeDtypeStruct((n_tok * (d // LANES), LANES), F32),
                   jax.ShapeDtypeStruct((n_tok, LANES), I32),
                   jax.ShapeDtypeStruct((n_tok, LANES), F32)],
        compiler_params=_cparams(("arbitrary",)),
        name="moe_router",
    )(h, g.reshape(1, d), wr, br)


def _gather_copy(x_hbm, xbuf, sem, slot, i, tok, dc):
    src = x_hbm.at[pl.ds(pl.multiple_of(tok * dc, dc), dc)]
    dst = xbuf.at[slot, pl.ds(pl.multiple_of(i * dc, dc), dc)]
    return pltpu.make_async_copy(src, dst, sem.at[slot])


def _moe_ffn_body(te_ref, nvalid_ref, tok_ref, x_hbm, rw_ref, w1_ref, w3_ref, w2_ref, y_ref,
                  xbuf, sem, w1b, w3b, w2b, *, tm):
    t = pl.program_id(0)
    nv = nvalid_ref[0]
    valid = t < nv
    dc = w1_ref.shape[0] // LANES

    ahead = xbuf.shape[0]

    def issue(tile, slot):
        def body(i, c):
            _gather_copy(x_hbm, xbuf, sem, slot, i, tok_ref[tile * tm + i], dc).start()
            return c
        lax.fori_loop(0, tm, body, 0, unroll=8)

    for first in range(ahead - 1):
        @pl.when((t == 0) & (first < nv))
        def _():
            issue(first, first)

    @pl.when(t + ahead - 1 < nv)
    def _():
        issue(t + ahead - 1, (t + ahead - 1) % ahead)

    @pl.when(valid)
    def _():
        prev = te_ref[jnp.maximum(t - 1, 0)]

        @pl.when((t == 0) | (te_ref[t] != prev))
        def _():
            w1b[...] = w1_ref[...].astype(BF16)
            w3b[...] = w3_ref[...].astype(BF16)
            w2b[...] = w2_ref[...].astype(BF16)

        slot = t % ahead

        pltpu.make_async_copy(x_hbm.at[pl.ds(0, tm * dc)], xbuf.at[slot], sem.at[slot]).wait()

        x = _load_token_rows(xbuf.at[slot], dc).astype(BF16)
        h1 = jnp.dot(x, w1b[...], preferred_element_type=F32)
        h3 = jnp.dot(x, w3b[...], preferred_element_type=F32)
        hh = (h1 * _sigmoid(h1)) * h3 * rw_ref[...]
        _store_token_rows(y_ref, jnp.dot(hh.astype(BF16), w2b[...], preferred_element_type=F32))

    @pl.when(jnp.logical_not(valid))
    def _():
        y_ref[...] = jnp.zeros_like(y_ref)


def _moe_ffn(xn, row_tok, row_w, tile_exp, nvalid_tiles, w1, w3, w2, layer, n_tiles, tm):
    d, f = w1.shape[2], w1.shape[3]
    dc = d // LANES
    rows = n_tiles * tm
    def last_valid(t, nv):
        return jnp.minimum(t, jnp.maximum(nv[0] - 1, 0))
    return pl.pallas_call(
        functools.partial(_moe_ffn_body, tm=tm),
        grid_spec=pltpu.PrefetchScalarGridSpec(
            num_scalar_prefetch=3,
            grid=(n_tiles,),
            in_specs=[pl.BlockSpec(memory_space=pl.ANY),
                      pl.BlockSpec((tm, 1), lambda t, te, nv, tk: (last_valid(t, nv), 0)),
                      pl.BlockSpec((None, None, d, f), lambda t, te, nv, tk: (layer, te[t], 0, 0)),
                      pl.BlockSpec((None, None, d, f), lambda t, te, nv, tk: (layer, te[t], 0, 0)),
                      pl.BlockSpec((None, None, f, d), lambda t, te, nv, tk: (layer, te[t], 0, 0))],
            out_specs=pl.BlockSpec((tm * dc, LANES), lambda t, te, nv, tk: (t, 0)),
            scratch_shapes=[pltpu.VMEM((3, tm * dc, LANES), F32), pltpu.SemaphoreType.DMA((3,)),
                            pltpu.VMEM((d, f), BF16), pltpu.VMEM((d, f), BF16), pltpu.VMEM((f, d), BF16)]),
        out_shape=jax.ShapeDtypeStruct((rows * dc, LANES), F32),
        compiler_params=_cparams(("arbitrary",)),
        name="moe_ffn",
    )(tile_exp, nvalid_tiles, row_tok, xn, row_w, w1, w3, w2)


def _combine_copy(y_hbm, buf, sem, slot, k, i, row, dc):
    src = y_hbm.at[pl.ds(pl.multiple_of(row * dc, dc), dc)]
    dst = buf.at[slot, k, pl.ds(pl.multiple_of(i * dc, dc), dc)]
    return pltpu.make_async_copy(src, dst, sem.at[slot])


def _moe_combine_body(pos_ref, h_ref, y_hbm, *rest, tm, final_eps):
    if final_eps is None:
        o_ref, buf, sem = rest
    else:
        gf_ref, o_ref, on_ref, buf, sem = rest
    t = pl.program_id(0)
    nt = pl.num_programs(0)
    dc = h_ref.shape[1] // LANES

    def issue(tile, slot):
        def body(i, c):
            a = 2 * (tile * tm + i)
            _combine_copy(y_hbm, buf, sem, slot, 0, i, pos_ref[a], dc).start()
            _combine_copy(y_hbm, buf, sem, slot, 1, i, pos_ref[a + 1], dc).start()
            return c
        lax.fori_loop(0, tm, body, 0, unroll=8)

    @pl.when(t == 0)
    def _():
        issue(0, 0)

    @pl.when(t + 1 < nt)
    def _():
        issue(t + 1, (t + 1) % 2)

    slot = t % 2

    for k in range(2):
        pltpu.make_async_copy(y_hbm.at[pl.ds(0, tm * dc)], buf.at[slot, k], sem.at[slot]).wait()

    out = h_ref[...] + _load_token_rows(buf.at[slot, 0], dc) + _load_token_rows(buf.at[slot, 1], dc)
    o_ref[...] = out
    if final_eps is not None:
        inv = lax.rsqrt(jnp.mean(out * out, axis=-1, keepdims=True) + final_eps)
        on_ref[...] = (out * inv) * gf_ref[...]


def _moe_combine(h, ys, pos, tm, final_g=None):
    n_tok, d = h.shape
    tok = pl.BlockSpec((tm, d), lambda i, p: (i, 0))
    in_specs = [tok, pl.BlockSpec(memory_space=pl.ANY)]
    args = [pos, h, ys]
    out_specs = [tok]
    out_shape = [jax.ShapeDtypeStruct((n_tok, d), F32)]
    if final_g is not None:
        in_specs.append(pl.BlockSpec((1, d), lambda i, p: (0, 0)))
        args.append(final_g.reshape(1, d))
        out_specs.append(tok)
        out_shape.append(jax.ShapeDtypeStruct((n_tok, d), F32))
    return pl.pallas_call(
        functools.partial(_moe_combine_body, tm=tm, final_eps=None if final_g is None else NORM_EPS),
        grid_spec=pltpu.PrefetchScalarGridSpec(
            num_scalar_prefetch=1,
            grid=(n_tok // tm,),
            in_specs=in_specs,
            out_specs=out_specs,
            scratch_shapes=[pltpu.VMEM((2, 2, tm * (d // LANES), LANES), F32), pltpu.SemaphoreType.DMA((2,))]),
        out_shape=out_shape,
        compiler_params=_cparams(("arbitrary",)),
        name="moe_combine",
    )(*args)


def _hier_moe(h, g, w_grp, b_grp, w_exp, b_exp, w1, w3, w2, layer, final_g=None):
    n_tok, d = h.shape
    n_exp = w1.shape[1]
    tm = MOE_TILE
    xn, idx, cw = _router(h, g, w_grp, b_grp, w_exp, b_exp, tm=256)

    n_asg = 2 * n_tok
    flat_e = idx[:, :2].reshape(n_asg)
    flat_c = cw[:, :2].reshape(n_asg)
    onehot = (flat_e[:, None] == jnp.arange(n_exp, dtype=I32)[None, :]).astype(I32)
    csum = jnp.cumsum(onehot, axis=0)
    counts = csum[-1]
    rank = jnp.take_along_axis(csum - onehot, flat_e[:, None], axis=1)[:, 0]
    padded = ((counts + tm - 1) // tm) * tm
    ends = jnp.cumsum(padded)
    pos = (ends - padded)[flat_e] + rank
    n_tiles = n_asg // tm + n_exp
    rows = n_tiles * tm
    upd = jnp.stack([jnp.arange(n_asg, dtype=I32) // 2, lax.bitcast_convert_type(flat_c, I32)], axis=1)
    row_meta = jnp.zeros((rows, 2), I32).at[pos].set(upd)
    row_tok = row_meta[:, 0]
    row_w = lax.bitcast_convert_type(row_meta[:, 1], F32).reshape(rows, 1)
    nvalid = (ends[-1] // tm).astype(I32).reshape(1)
    tile_ids = jnp.minimum(jnp.arange(n_tiles, dtype=I32), nvalid - 1)
    tile_exp = jnp.searchsorted(ends, tile_ids * tm, side="right").astype(I32)

    ys = _moe_ffn(xn, row_tok, row_w, tile_exp, nvalid, w1, w3, w2, layer, n_tiles, tm)
    return _moe_combine(h, ys, pos.astype(I32), 128, final_g)


def kernel(x_prompt, x_sample, cache_k, cache_v, page_table, state_wkv, state_shift, norm_mix, norm_ffn, norm_final, even_w_in, even_w_out, rwkv_mu, rwkv_w0, rwkv_w2, rwkv_a0, rwkv_a2, rwkv_g2, rwkv_k_k, rwkv_k_a, rwkv_r_k, rwkv_lnx_g, rwkv_lnx_b, diff_lam_q1, diff_lam_k1, diff_lam_q2, diff_lam_k2, diff_subln_g, gmlp_w_in, gmlp_ln_g, gmlp_ln_b, gmlp_w_s, gmlp_b_s, gmlp_w_out, moe_w_grp, moe_b_grp, moe_w_exp, moe_b_exp, moe_w1, moe_w3, moe_w2):
    bp, tp, d = x_prompt.shape
    bs, ts, _ = x_sample.shape
    n_p, n_s = bp * tp, bs * ts
    n_tok = n_p + n_s
    depth = norm_mix.shape[0]
    npages, psz = page_table.shape[1], cache_k.shape[2]
    past_len = npages * psz
    d_a = rwkv_w0.shape[1]
    d_ap = rwkv_mu.shape[1]
    d_q = cache_k.shape[3] * cache_k.shape[4]
    n_pool = cache_k.shape[1]
    nheads = d_a // RWKV_HEAD_DIM
    tm = 512 if n_p % 512 == 0 and n_s % 512 == 0 else 128
    tq = next((t for t in (1024, 512) if tp % t == 0), 128)
    tc = min(tp, CHUNK)

    h = jnp.concatenate([x_prompt.reshape(n_p, d), x_sample.reshape(n_s, d)], axis=0)
    pos = jnp.concatenate([jnp.tile(jnp.arange(tp), bp), jnp.tile(past_len + jnp.arange(ts), bs)])
    moe = lambda l, hh, fg=None: _hier_moe(hh, norm_ffn[l], moe_w_grp[l], moe_b_grp[l], moe_w_exp[l],
                                          moe_b_exp[l], moe_w1, moe_w3, moe_w2, l, fg)

    outs = {k: [] for k in ("k_p", "v_p", "wkv_p", "sh_p", "k_s", "v_s", "wkv_s", "sh_s", "gv_s")}
    y_norm = None
    for layer in range(depth):
        j = layer // 2
        xn = _rmsnorm(h, norm_mix[layer], NORM_EPS, BF16, tm)
        last = layer == depth - 1
        if layer % 2 == 0:
            proj = _matmul([xn], even_w_in[j], bm=tm, bn=even_w_in.shape[2] // 5, name="even_in_proj")
            prm = dict(mu=rwkv_mu[j].reshape(1, d_ap), w0=rwkv_w0[j].reshape(1, d_a), w2=rwkv_w2[j],
                       a0=rwkv_a0[j].reshape(1, d_a), a2=rwkv_a2[j], g2=rwkv_g2[j],
                       k_k=rwkv_k_k[j].reshape(1, d_a), k_a=rwkv_k_a[j].reshape(1, d_a),
                       r_k=rwkv_r_k[j].reshape(1, d_a))
            tmr = 256 if tp % 256 == 0 else 128
            tile_last = proj[tmr - 1:n_p:tmr, :d_ap]
            seq_last_s = proj[n_p + ts - 1::ts, :d_ap]
            starts_seq = (jnp.arange(n_p // tmr) % (tp // tmr) == 0)[:, None]
            bnd = jnp.concatenate([jnp.zeros((1, d_ap), F32), tile_last[:-1]], axis=0)
            bnd = jnp.where(starts_seq, 0.0, bnd)
            repl_p = jnp.zeros((n_p // tmr, 8, d_ap), F32).at[:, 0].set(bnd)
            repl_p = repl_p.reshape(n_p // tmr * 8, d_ap)
            repl_s = jnp.repeat(state_shift[j], ts, axis=0)
            pre_p = _rwkv_pre(proj, repl_p, tmr, 0, n_p, tmr, prm)
            pre_s = _rwkv_pre(proj, repl_s, ts, n_p, n_s, tmr, prm)
            oa_p, wkv_p = _rwkv_scan(pre_p, rwkv_lnx_g[j], rwkv_lnx_b[j], None, bp, tp, tc,
                                     2 if bp % 2 == 0 else 1)
            oa_s, wkv_s = _rwkv_scan(pre_s, rwkv_lnx_g[j], rwkv_lnx_b[j], state_wkv[j], bs, ts, ts,
                                     2 if bs % 2 == 0 else 1)

            qs, qf, kf, kb, vb, kt = _rope(proj, pos, d_ap, d_q, tm)
            vf = proj[:, d_ap + 2 * d_q:]
            lam_init = 0.8 - 0.6 * math.exp(-0.3 * layer)
            lam_p = jnp.stack([diff_lam_q1[j], diff_lam_k1[j], diff_lam_q2[j], diff_lam_k2[j]])
            ob_p = _attn_prompt(qs[:n_p], kb[:n_p], vb[:n_p], lam_p, diff_subln_g[j], bp, tp, tq, lam_init)
            ob_s = _attn_sample(qf[n_p:], kf[n_p:], vf[n_p:], cache_k, cache_v, j,
                                page_table.reshape(-1).astype(I32),
                                lam_p, diff_subln_g[j], bs, ts, npages, lam_init)
            oa = jnp.concatenate([oa_p, oa_s.astype(BF16)], axis=0)
            ob = jnp.concatenate([ob_p, ob_s.astype(BF16)], axis=0)
            h = _matmul([oa, ob], even_w_out[j], bm=tm, bn=1024, res=h, name="even_out_proj")

            kt_p = kt[:n_p // tm].reshape(bp, tp // tm, kt.shape[1], DIFF_QK_DIM, tm)
            outs["k_p"].append(jnp.transpose(kt_p, (0, 1, 4, 2, 3)).reshape(bp, tp, -1, DIFF_QK_DIM))
            outs["v_p"].append(vf[:n_p].reshape(bp, tp, -1, DIFF_V_DIM))
            outs["wkv_p"].append(wkv_p)
            outs["sh_p"].append(tile_last[tp // tmr - 1::tp // tmr])
            outs["k_s"].append(kf[n_p:].reshape(bs, ts, -1, DIFF_QK_DIM))
            outs["v_s"].append(vf[n_p:].reshape(bs, ts, -1, DIFF_V_DIM))
            outs["wkv_s"].append(wkv_s)
            outs["sh_s"].append(seq_last_s)
        else:
            d_c = gmlp_ln_g.shape[1]
            z = _matmul([xn], gmlp_w_in[j], bm=tm, bn=1024, act="gelu", name="gmlp_in_proj")
            ws = gmlp_w_s[j]
            bsb = gmlp_b_s[j]
            gd = d_c // GMLP_GROUPS
            lp = min(tp, CHUNK)
            wm_p = jnp.tril(ws[:, :lp, :lp])
            bm_p = jnp.broadcast_to(bsb[:, :lp, None], (GMLP_GROUPS, lp, gd))
            rep = CHUNK // ts
            eye = jnp.eye(rep, dtype=F32)
            wm_s = jnp.einsum("ab,gts->gatbs", eye, jnp.tril(ws[:, :ts, :ts])).reshape(GMLP_GROUPS, CHUNK, CHUNK)
            bm_s = jnp.broadcast_to(jnp.tile(bsb[:, :ts], (1, rep))[:, :, None], (GMLP_GROUPS, CHUNK, gd))
            (y_p,) = _gmlp(z, 0, n_p, gmlp_ln_g[j], gmlp_ln_b[j], wm_p, bm_p, False)
            y_s, v_rows = _gmlp(z, n_p, n_s, gmlp_ln_g[j], gmlp_ln_b[j], wm_s, bm_s, True)
            yin = jnp.concatenate([y_p, y_s], axis=0)
            h = _matmul([yin], gmlp_w_out[j], bm=tm, bn=1024, res=h, name="gmlp_out_proj")
            outs["gv_s"].append(v_rows.reshape(bs, ts, d_c))
        res = moe(layer, h, norm_final if last else None)
        h = res[0]
        if last:
            y_norm = res[1]

    y_prompt = y_norm[:n_p].reshape(bp, tp, d)
    y_sample = y_norm[n_p:].reshape(bs, ts, d)
    st = lambda k: jnp.stack(outs[k])
    return (y_prompt, y_sample, st("k_p"), st("v_p"), st("wkv_p"), st("sh_p"),
            st("k_s"), st("v_s"), st("wkv_s"), st("sh_s"), st("gv_s"))
```
